```python
import math
import jax
import jax.numpy as jnp
from jax import lax
import numpy as np

D_MODEL = 2048
BATCH = 16
SEQ = 256
DEPTH = 4
DEC_BATCH = 4
DEC_SEQ = 2048
PAST_LEN = 512

GRID_W = 64
N_MIXERS = 3
N_GLA_LAYERS = (DEPTH + 2) // 3
N_SSD_LAYERS = (DEPTH + 1) // 3
N_LRU_LAYERS = DEPTH // 3
MOD_CHUNKS = 6
NORM_EPS = 1e-6

GLA_HEADS = 4
GLA_DK = D_MODEL // 8
GLA_DV = D_MODEL // 4
GLA_RANK = 16
GLA_GATE_NORM = 16.0
GLA_CHUNK = 64

SSD_DI = 2 * D_MODEL
SSD_HEADDIM = 64
SSD_HEADS = SSD_DI // SSD_HEADDIM
SSD_GROUPS = 8
SSD_STATE = 128
SSD_CONV = 4
SSD_CHUNK = 64

LRU_WIDTH = D_MODEL
LRU_BLOCKS = 8
LRU_BW = LRU_WIDTH // LRU_BLOCKS
LRU_CONV = 4
LRU_C = 8.0

FFN_DIM = 5632
FFN_CONV = 3

kernel_name = "hybrid_gla_ssd_rglru_prefix_diffusion_step"


def _flip(t):
    return jnp.flip(t, axis=1)


def rmsnorm(x, g):
    xf = x.astype(jnp.float32)
    y = xf * lax.rsqrt(jnp.mean(xf * xf, axis=-1, keepdims=True) + NORM_EPS)
    return (y * g.astype(jnp.float32)).astype(x.dtype)


def dwconv2d(x, w, b):
    kh, kw, ch = w.shape
    pad = (((kh - 1) // 2, kh // 2), ((kw - 1) // 2, kw // 2))
    y = lax.conv_general_dilated(
        x, w[:, :, None, :].astype(x.dtype), (1, 1), pad,
        dimension_numbers=("NHWC", "HWIO", "NHWC"), feature_group_count=ch)
    return y + b.astype(x.dtype)


def dwconv1d(x, w, b):
    return dwconv2d(x[:, None], w[None], b)[:, 0]


def linear_scan(a, u, h0):
    def combine(e1, e2):
        return e1[0] * e2[0], e2[0] * e1[1] + e2[1]
    a_cum, u_cum = lax.associative_scan(combine, (a, u), axis=1)
    h = a_cum * h0.astype(jnp.float32)[:, None] + u_cum
    return h, h[:, -1]


def gla_scan(q, k, v, g, s0):
    bsz, L, H, K = q.shape
    V = v.shape[-1]
    C = GLA_CHUNK
    nc = L // C
    f32 = jnp.float32
    q, k, g = [t.astype(f32).reshape(bsz, nc, C, H, K) for t in (q, k, g)]
    v = v.astype(f32).reshape(bsz, nc, C, H, V)
    gc = jnp.cumsum(g, axis=2)
    g_mid = gc[:, :, C // 2:C // 2 + 1]
    g_end = gc[:, :, -1:]
    causal = jnp.tril(jnp.ones((C, C), bool))
    scores = jnp.einsum("bcihk,bcjhk->bchij", q * jnp.exp(gc - g_mid), k * jnp.exp(g_mid - gc))
    o = jnp.einsum("bchij,bcjhv->bcihv", jnp.where(causal, scores, 0.0), v)
    u = jnp.einsum("bcjhk,bcjhv->bchkv", k * jnp.exp(g_end - gc), v)
    chunk_decay = jnp.exp(g_end[:, :, 0])

    def step(s, inp):
        d, uu = inp
        return d[..., None] * s + uu, s

    s_fin, s_prev = lax.scan(step, s0.astype(f32),
                             (jnp.moveaxis(chunk_decay, 1, 0), jnp.moveaxis(u, 1, 0)))
    s_prev = jnp.moveaxis(s_prev, 0, 1)
    o = o + jnp.einsum("bcihk,bchkv->bcihv", q * jnp.exp(gc), s_prev)
    return o.reshape(bsz, L, H, V), s_fin


def ssd_scan(x, dt, a, bm, cm, s0):
    bsz, L, H, P = x.shape
    G, N = bm.shape[2], bm.shape[3]
    R = H // G
    C = SSD_CHUNK
    nc = L // C
    f32 = jnp.float32
    xd = (x.astype(f32) * dt[..., None]).reshape(bsz, nc, C, G, R, P)
    cum = jnp.cumsum((dt * a).reshape(bsz, nc, C, G, R), axis=2)
    bm = bm.astype(f32).reshape(bsz, nc, C, G, N)
    cm = cm.astype(f32).reshape(bsz, nc, C, G, N)
    cum_t = jnp.moveaxis(cum, 2, -1)
    seg = cum_t[..., :, None] - cum_t[..., None, :]
    causal = jnp.tril(jnp.ones((C, C), bool))
    decay = jnp.exp(jnp.where(causal, seg, -jnp.inf))
    cb = jnp.einsum("bcigs,bcjgs->bcgij", cm, bm)
    y = jnp.einsum("bcgrij,bcjgrp->bcigrp", cb[:, :, :, None] * decay, xd)
    w_end = jnp.exp(cum[:, :, -1:] - cum)
    st = jnp.einsum("bcjgs,bcjgrp->bcgrps", bm, xd * w_end[..., None])
    chunk_decay = jnp.exp(cum[:, :, -1])

    def step(s, inp):
        d, uu = inp
        return d[..., None, None] * s + uu, s

    s_fin, s_prev = lax.scan(step, s0.astype(f32).reshape(bsz, G, R, P, N),
                             (jnp.moveaxis(chunk_decay, 1, 0), jnp.moveaxis(st, 1, 0)))
    s_prev = jnp.moveaxis(s_prev, 0, 1)
    y = y + jnp.einsum("bcigs,bcgrps->bcigrp", cm, s_prev) * jnp.exp(cum)[..., None]
    return y.reshape(bsz, L, H, P), s_fin.reshape(bsz, H, P, N)


def gla_mixer(h, s0, w_in, w_g1, w_g2, b_g, norm_g, w_out):
    bsz, L, _ = h.shape
    qk = GLA_HEADS * GLA_DK
    vd = GLA_HEADS * GLA_DV
    q, k, v, r = jnp.split(h @ w_in, [qk, 2 * qk, 2 * qk + vd], axis=-1)
    q = q.reshape(bsz, L, GLA_HEADS, GLA_DK) * (GLA_DK ** -0.5)
    k = k.reshape(bsz, L, GLA_HEADS, GLA_DK)
    v = v.reshape(bsz, L, GLA_HEADS, GLA_DV)
    low = jnp.einsum("bld,edr->bler", h, w_g1)
    g = jnp.einsum("bler,erk->blek", low, w_g2) + b_g
    g = (jax.nn.log_sigmoid(g.astype(jnp.float32)) / GLA_GATE_NORM).reshape(bsz, L, 2, GLA_HEADS, GLA_DK)
    o_f, s_f = gla_scan(q, k, v, g[:, :, 0], s0[:, 0])
    o_b, s_b = gla_scan(_flip(q), _flip(k), _flip(v), _flip(g[:, :, 1]), s0[:, 1])
    o = o_f + _flip(o_b)
    o = rmsnorm(o, norm_g) * jax.nn.silu(r.reshape(bsz, L, GLA_HEADS, GLA_DV).astype(jnp.float32))
    return o.reshape(bsz, L, vd).astype(h.dtype) @ w_out, jnp.stack([s_f, s_b], axis=1)


def ssd_mixer(h, s0, w_in, conv_w, conv_b, a_log, dt_bias, d_skip, norm_g, w_out):
    bsz, L, _ = h.shape
    f32 = jnp.float32
    gn = SSD_GROUPS * SSD_STATE
    z, xbc, dt_raw = jnp.split(h @ w_in, [SSD_DI, 2 * SSD_DI + 2 * gn], axis=-1)
    xbc = jax.nn.silu(dwconv1d(xbc, conv_w, conv_b)).astype(f32)
    xs, bm, cm = jnp.split(xbc, [SSD_DI, SSD_DI + gn], axis=-1)
    xs = xs.reshape(bsz, L, SSD_HEADS, SSD_HEADDIM)
    bm = bm.reshape(bsz, L, SSD_GROUPS, SSD_STATE)
    cm = cm.reshape(bsz, L, SSD_GROUPS, SSD_STATE)
    dt = jax.nn.softplus(dt_raw.astype(f32).reshape(bsz, L, 2, SSD_HEADS) + dt_bias.astype(f32))
    a = -jnp.exp(a_log.astype(f32))
    y_f, s_f = ssd_scan(xs, dt[:, :, 0], a[0], bm, cm, s0[:, 0])
    y_b, s_b = ssd_scan(_flip(xs), _flip(dt[:, :, 1]), a[1], _flip(bm), _flip(cm), s0[:, 1])
    y = y_f + _flip(y_b) + d_skip.astype(f32)[:, None] * xs
    y = y.reshape(bsz, L, SSD_DI) * jax.nn.silu(z.astype(f32))
    y = rmsnorm(y.reshape(bsz, L, SSD_GROUPS, -1), norm_g.reshape(SSD_GROUPS, -1)).reshape(bsz, L, SSD_DI)
    return y.astype(h.dtype) @ w_out, jnp.stack([s_f, s_b], axis=1)


def rglru_mixer(h, s0, w_in, conv_w, conv_b, w_a, b_a, w_i, b_i, lam, w_out):
    bsz, L, _ = h.shape
    f32 = jnp.float32
    xr, gate = jnp.split(h @ w_in, 2, axis=-1)
    xr = dwconv1d(xr, conv_w, conv_b).astype(f32)
    xb = xr.reshape(bsz, L, LRU_BLOCKS, LRU_BW)
    r = jax.nn.sigmoid(jnp.einsum("blnk,enkm->blenm", xb, w_a).reshape(bsz, L, 2, LRU_WIDTH) + b_a)
    ig = jax.nn.sigmoid(jnp.einsum("blnk,enkm->blenm", xb, w_i).reshape(bsz, L, 2, LRU_WIDTH) + b_i)
    log_a = -LRU_C * r * jax.nn.softplus(-lam.astype(f32))
    a = jnp.exp(log_a)
    u = jnp.sqrt(-jnp.expm1(2.0 * log_a)) * ig * xr[:, :, None]
    h_f, s_f = linear_scan(a[:, :, 0], u[:, :, 0], s0[:, 0])
    h_b, s_b = linear_scan(_flip(a[:, :, 1]), _flip(u[:, :, 1]), s0[:, 1])
    y = (h_f + _flip(h_b)) * jax.nn.gelu(gate.astype(f32))
    return y.astype(h.dtype) @ w_out, jnp.stack([s_f, s_b], axis=1)


def conv_ffn(h, rows, w_up, conv_w, conv_b, w_down):
    bsz, L, _ = h.shape
    act, val = jnp.split(h @ w_up, 2, axis=-1)
    act = dwconv2d(act.reshape(bsz, rows, L // rows, FFN_DIM), conv_w, conv_b).reshape(bsz, L, FFN_DIM)
    return (jax.nn.silu(act) * val) @ w_down


def block(x, rows, mod, mixer_fn, s0, mixer_params, norm_mix_g, norm_ffn_g, ffn_w_up, ffn_conv_w, ffn_conv_b, ffn_w_down):
    shift_m, scale_m, gate_m, shift_f, scale_f, gate_f = jnp.split(mod[:, None, :].astype(x.dtype), MOD_CHUNKS, axis=-1)
    h = rmsnorm(x, norm_mix_g) * (1 + scale_m) + shift_m
    y, state = mixer_fn(h, s0, *mixer_params)
    x = x + gate_m * y
    h = rmsnorm(x, norm_ffn_g) * (1 + scale_f) + shift_f
    x = x + gate_f * conv_ffn(h, rows, ffn_w_up, ffn_conv_w, ffn_conv_b, ffn_w_down)
    return x, state


def setup_inputs(seed: int = 0) -> dict:
    key = jax.random.key(seed)
    keys = iter(jax.random.split(key, 48))

    def nrm(shape, scale):
        return scale * jax.random.normal(next(keys), shape, jnp.float32)

    def unif(shape, lo, hi):
        return jax.random.uniform(next(keys), shape, jnp.float32, lo, hi)

    D = D_MODEL
    qk = GLA_HEADS * GLA_DK
    vd = GLA_HEADS * GLA_DV
    gn = SSD_GROUPS * SSD_STATE
    dt0 = jnp.exp(unif((N_SSD_LAYERS, 2, SSD_HEADS), math.log(1e-3), math.log(1e-1)))
    a0 = unif((N_LRU_LAYERS, 2, LRU_WIDTH), 0.9, 0.999)
    sig0 = a0 ** (1.0 / LRU_C)
    return {
        "x_prompt": nrm((BATCH, SEQ, D), 1.0),
        "x_sample": nrm((DEC_BATCH, DEC_SEQ, D), 1.0),
        "state_gla": nrm((DEC_BATCH, N_GLA_LAYERS, 2, GLA_HEADS, GLA_DK, GLA_DV), 0.5),
        "state_ssd": nrm((DEC_BATCH, N_SSD_LAYERS, 2, SSD_HEADS, SSD_HEADDIM, SSD_STATE), 0.5),
        "state_lru": nrm((DEC_BATCH, N_LRU_LAYERS, 2, LRU_WIDTH), 0.5),
        "c": nrm((DEC_BATCH, D), 1.0),
        "c_ctx": nrm((D,), 1.0),
        "w_mod": nrm((DEPTH, D, MOD_CHUNKS * D), 0.5 * D ** -0.5),
        "b_mod": nrm((DEPTH, MOD_CHUNKS * D), 0.02),
        "norm_mix_g": 1.0 + nrm((DEPTH, D), 0.05),
        "norm_ffn_g": 1.0 + nrm((DEPTH, D), 0.05),
        "ffn_w_up": nrm((DEPTH, D, 2 * FFN_DIM), D ** -0.5),
        "ffn_conv_w": nrm((DEPTH, FFN_CONV, FFN_CONV, FFN_DIM), 1.0 / FFN_CONV),
        "ffn_conv_b": nrm((DEPTH, FFN_DIM), 0.02),
        "ffn_w_down": nrm((DEPTH, FFN_DIM, D), FFN_DIM ** -0.5),
        "final_norm_g": 1.0 + nrm((D,), 0.05),
        "gla_w_in": nrm((N_GLA_LAYERS, D, 2 * qk + 2 * vd), D ** -0.5),
        "gla_w_g1": nrm((N_GLA_LAYERS, 2, D, GLA_RANK), D ** -0.5),
        "gla_w_g2": nrm((N_GLA_LAYERS, 2, GLA_RANK, qk), GLA_RANK ** -0.5),
        "gla_b_g": nrm((N_GLA_LAYERS, 2, qk), 0.1),
        "gla_norm_g": 1.0 + nrm((N_GLA_LAYERS, GLA_DV), 0.05),
        "gla_w_out": nrm((N_GLA_LAYERS, vd, D), vd ** -0.5),
        "ssd_w_in": nrm((N_SSD_LAYERS, D, 2 * SSD_DI + 2 * gn + 2 * SSD_HEADS), D ** -0.5),
        "ssd_conv_w": nrm((N_SSD_LAYERS, SSD_CONV, SSD_DI + 2 * gn), SSD_CONV ** -0.5),
        "ssd_conv_b": nrm((N_SSD_LAYERS, SSD_DI + 2 * gn), 0.02),
        "ssd_a_log": jnp.log(unif((N_SSD_LAYERS, 2, SSD_HEADS), 1.0, 16.0)),
        "ssd_dt_bias": dt0 + jnp.log(-jnp.expm1(-dt0)),
        "ssd_d": 1.0 + nrm((N_SSD_LAYERS, SSD_HEADS), 0.1),
        "ssd_norm_g": 1.0 + nrm((N_SSD_LAYERS, SSD_DI), 0.05),
        "ssd_w_out": nrm((N_SSD_LAYERS, SSD_DI, D), SSD_DI ** -0.5),
        "lru_w_in": nrm((N_LRU_LAYERS, D, 2 * LRU_WIDTH), D ** -0.5),
        "lru_conv_w": nrm((N_LRU_LAYERS, LRU_CONV, LRU_WIDTH), LRU_CONV ** -0.5),
        "lru_conv_b": nrm((N_LRU_LAYERS, LRU_WIDTH), 0.02),
        "lru_w_a": nrm((N_LRU_LAYERS, 2, LRU_BLOCKS, LRU_BW, LRU_BW), LRU_BW ** -0.5),
        "lru_b_a": nrm((N_LRU_LAYERS, 2, LRU_WIDTH), 0.1),
        "lru_w_i": nrm((N_LRU_LAYERS, 2, LRU_BLOCKS, LRU_BW, LRU_BW), LRU_BW ** -0.5),
        "lru_b_i": nrm((N_LRU_LAYERS, 2, LRU_WIDTH), 0.1),
        "lru_lambda": jnp.log(sig0) - jnp.log1p(-sig0),
        "lru_w_out": nrm((N_LRU_LAYERS, LRU_WIDTH, D), LRU_WIDTH ** -0.5),
    }


def reference(x_prompt, x_sample, state_gla, state_ssd, state_lru, c, c_ctx, w_mod, b_mod,
              norm_mix_g, norm_ffn_g, ffn_w_up, ffn_conv_w, ffn_conv_b, ffn_w_down, final_norm_g,
              gla_w_in, gla_w_g1, gla_w_g2, gla_b_g, gla_norm_g, gla_w_out,
              ssd_w_in, ssd_conv_w, ssd_conv_b, ssd_a_log, ssd_dt_bias, ssd_d, ssd_norm_g, ssd_w_out,
              lru_w_in, lru_conv_w, lru_conv_b, lru_w_a, lru_b_a, lru_w_i, lru_b_i, lru_lambda, lru_w_out):
    bp = x_prompt.shape[0]
    rows = x_sample.shape[1] // GRID_W
    silu_ctx = jax.nn.silu(c_ctx)[None]
    silu_lat = jax.nn.silu(c)
    xp, xs = x_prompt, x_sample
    gla_new, ssd_new, lru_new = [], [], []
    for l in range(DEPTH):
        kind, j = l % N_MIXERS, l // N_MIXERS
        if kind == 0:
            fn = gla_mixer
            params = (gla_w_in[j], gla_w_g1[j], gla_w_g2[j], gla_b_g[j], gla_norm_g[j], gla_w_out[j])
            s_ctx0 = jnp.zeros((bp, 2, GLA_HEADS, GLA_DK, GLA_DV), jnp.float32)
            s_lat0 = state_gla[:, j]
            store = gla_new
        elif kind == 1:
            fn = ssd_mixer
            params = (ssd_w_in[j], ssd_conv_w[j], ssd_conv_b[j], ssd_a_log[j], ssd_dt_bias[j],
                      ssd_d[j], ssd_norm_g[j], ssd_w_out[j])
            s_ctx0 = jnp.zeros((bp, 2, SSD_HEADS, SSD_HEADDIM, SSD_STATE), jnp.float32)
            s_lat0 = state_ssd[:, j]
            store = ssd_new
        else:
            fn = rglru_mixer
            params = (lru_w_in[j], lru_conv_w[j], lru_conv_b[j], lru_w_a[j], lru_b_a[j],
                      lru_w_i[j], lru_b_i[j], lru_lambda[j], lru_w_out[j])
            s_ctx0 = jnp.zeros((bp, 2, LRU_WIDTH), jnp.float32)
            s_lat0 = state_lru[:, j]
            store = lru_new
        ffn = (norm_mix_g[l], norm_ffn_g[l], ffn_w_up[l], ffn_conv_w[l], ffn_conv_b[l], ffn_w_down[l])
        mod_ctx = silu_ctx @ w_mod[l] + b_mod[l]
        mod_lat = silu_lat @ w_mod[l] + b_mod[l]
        xp, s_new = block(xp, 1, mod_ctx, fn, s_ctx0, params, *ffn)
        store.append(s_new)
        xs, _ = block(xs, rows, mod_lat, fn, s_lat0, params, *ffn)
    y_prompt = rmsnorm(xp, final_norm_g)
    y_sample = rmsnorm(xs, final_norm_g)
    return (y_prompt, y_sample, jnp.stack(gla_new, axis=1), jnp.stack(ssd_new, axis=1), jnp.stack(lru_new, axis=1))
```

```python
import functools

import jax
import jax.numpy as jnp
from jax import lax
from jax.experimental import pallas as pl
from jax.experimental.pallas import tpu as pltpu

F32 = jnp.float32
BF16 = jnp.bfloat16

_V7X_VMEM_BYTES = 64 * 1024 * 1024
_VMEM_LIMIT = _V7X_VMEM_BYTES - 8 * 1024 * 1024
_SUBLANES = 8
_LANES = 128

NORM_EPS = 1e-6
MOD_CHUNKS = 6
MOD_ROWS = 8
GRID_W = 64

GLA_HEADS = 4
GLA_RANK = 16
GLA_GATE_NORM = 16.0
SSD_HEADDIM = 64
SSD_GROUPS = 8
SSD_STATE = 128
LRU_BLOCKS = 8
LRU_C = 8.0
CHUNK = 64


def _cparams(*sem):
    return pltpu.CompilerParams(dimension_semantics=sem, vmem_limit_bytes=_VMEM_LIMIT)


def _silu(x):
    return x * jax.nn.sigmoid(x)


def _softplus(x):
    return jnp.maximum(x, 0.0) + jnp.log1p(jnp.exp(-jnp.abs(x)))


def _log_sigmoid(x):
    return jnp.minimum(x, 0.0) - jnp.log1p(jnp.exp(-jnp.abs(x)))


def _gelu_tanh(x):
    return 0.5 * x * (1.0 + jnp.tanh(0.7978845608028654 * (x + 0.044715 * (x * x * x))))


def _neg_expm1(x):
    t = jnp.tanh(0.5 * x)
    return -2.0 * t / (1.0 - t)


def _split3(x):
    x1 = x.astype(BF16)
    r1 = x - x1.astype(F32)
    x2 = r1.astype(BF16)
    x3 = (r1 - x2.astype(F32)).astype(BF16)
    return x1, x2, x3


def _dot(a, b):
    return jnp.dot(a, b, preferred_element_type=F32)


def _dot_nt(a, b):
    return lax.dot_general(a, b, (((1,), (1,)), ((), ())), preferred_element_type=F32)


def _dot_tn(a, b):
    return lax.dot_general(a, b, (((0,), (0,)), ((), ())), preferred_element_type=F32)


def _exact_dot_left01(m01, x):
    x1, x2, x3 = _split3(x)
    return _dot(m01, x1) + _dot(m01, x2) + _dot(m01, x3)


def _exact_dot_right01(x, m01):
    x1, x2, x3 = _split3(x)
    return _dot(x1, m01) + _dot(x2, m01) + _dot(x3, m01)


def _chunk_masks():
    ii = lax.broadcasted_iota(jnp.int32, (CHUNK, CHUNK), 0)
    jj = lax.broadcasted_iota(jnp.int32, (CHUNK, CHUNK), 1)
    masks = (jj <= ii, jj >= ii)
    tris = tuple(jnp.where(m, 1.0, 0.0).astype(BF16) for m in masks)
    return masks, tris


def _shifted_rows(ref, r0, nrows, off, total):
    lo, hi = r0 + off, r0 + off + nrows
    clo, chi = max(lo, 0), min(hi, total)
    x = ref[clo:chi, :]
    parts = []
    if clo > lo:
        parts.append(jnp.zeros((clo - lo, x.shape[1]), x.dtype))
    parts.append(x)
    if hi > chi:
        parts.append(jnp.zeros((hi - chi, x.shape[1]), x.dtype))
    return x if len(parts) == 1 else jnp.concatenate(parts, axis=0)


def _conv4_block(ref, w_ref, b_ref, r0, nrows, total):
    acc = b_ref[...] + w_ref[0:1, :] * _shifted_rows(ref, r0, nrows, -1, total)
    acc = acc + w_ref[1:2, :] * ref[r0:r0 + nrows, :]
    acc = acc + w_ref[2:3, :] * _shifted_rows(ref, r0, nrows, 1, total)
    acc = acc + w_ref[3:4, :] * _shifted_rows(ref, r0, nrows, 2, total)
    return acc


def _mod_kernel(c_ref, w_ref, b_ref, o_ref):
    c = c_ref[...]
    s = _silu(c).astype(BF16)
    o_ref[0] = _dot(s, w_ref[0].astype(BF16)) + b_ref[0]


def _mod_table(c_rows, w_mod, b_mod):
    depth, d, n = w_mod.shape
    bn = 1024
    return pl.pallas_call(
        _mod_kernel,
        grid=(depth, n // bn),
        in_specs=[
            pl.BlockSpec((MOD_ROWS, d), lambda l, j: (0, 0)),
            pl.BlockSpec((1, d, bn), lambda l, j: (l, 0, j)),
            pl.BlockSpec((1, 1, bn), lambda l, j: (l, 0, j)),
        ],
        out_specs=pl.BlockSpec((1, MOD_ROWS, bn), lambda l, j: (l, 0, j)),
        out_shape=jax.ShapeDtypeStruct((depth, MOD_ROWS, n), F32),
        compiler_params=_cparams("arbitrary", "arbitrary"),
        name="adaln_table",
    )(c_rows, w_mod, b_mod.reshape(depth, 1, n))


class _Tokens:
    def __init__(self, n_prompt, l_prompt, n_latent, l_latent):
        self.bp, self.lp, self.bs, self.ls = n_prompt, l_prompt, n_latent, l_latent
        self.np_rows = n_prompt * l_prompt
        self.ns_rows = n_latent * l_latent
        self.rows = self.np_rows + self.ns_rows

    def mod_row(self, i, bm):
        r0 = i * bm
        return jnp.where(r0 < self.np_rows, 0, 1 + (r0 - self.np_rows) // self.ls)


def _norm_mod_kernel(x_ref, g_ref, mod_ref, h_ref, *, shift_idx, scale_idx):
    x = x_ref[...]
    ms = jnp.mean(x * x, axis=-1, keepdims=True)
    y = x * lax.rsqrt(ms + NORM_EPS) * g_ref[...]
    h = y * (1.0 + mod_ref[0, scale_idx:scale_idx + 1, :]) + mod_ref[0, shift_idx:shift_idx + 1, :]
    h_ref[...] = h.astype(h_ref.dtype)


def _norm_mod(x, g, mod, layer, tok, *, shift_idx, scale_idx):
    rows, d = x.shape
    bm = 512
    return pl.pallas_call(
        functools.partial(_norm_mod_kernel, shift_idx=shift_idx, scale_idx=scale_idx),
        grid=(rows // bm,),
        in_specs=[
            pl.BlockSpec((bm, d), lambda i: (i, 0)),
            pl.BlockSpec((1, d), lambda i: (0, 0)),
            pl.BlockSpec((1, MOD_CHUNKS, d), lambda i: (layer * MOD_ROWS + tok.mod_row(i, bm), 0, 0)),
        ],
        out_specs=pl.BlockSpec((bm, d), lambda i: (i, 0)),
        out_shape=jax.ShapeDtypeStruct((rows, d), BF16),
        compiler_params=_cparams("arbitrary"),
        name="norm_adaln",
    )(x, g.reshape(1, d), mod)


def _final_norm_kernel(x_ref, g_ref, o_ref):
    x = x_ref[...]
    ms = jnp.mean(x * x, axis=-1, keepdims=True)
    o_ref[...] = x * lax.rsqrt(ms + NORM_EPS) * g_ref[...]


def _final_norm(x, g):
    rows, d = x.shape
    bm = 512
    return pl.pallas_call(
        _final_norm_kernel,
        grid=(rows // bm,),
        in_specs=[pl.BlockSpec((bm, d), lambda i: (i, 0)), pl.BlockSpec((1, d), lambda i: (0, 0))],
        out_specs=pl.BlockSpec((bm, d), lambda i: (i, 0)),
        out_shape=jax.ShapeDtypeStruct((rows, d), F32),
        compiler_params=_cparams("arbitrary"),
        name="final_norm",
    )(x, g.reshape(1, d))


def _mm_kernel(x_ref, w_ref, o_ref, acc_ref, *, nk):
    k = pl.program_id(2)
    part = _dot(x_ref[...], w_ref[...])
    if nk == 1:
        o_ref[...] = part.astype(o_ref.dtype)
        return

    @pl.when(k == 0)
    def _():
        acc_ref[...] = part

    @pl.when(jnp.logical_and(k > 0, k < nk - 1))
    def _():
        acc_ref[...] += part

    @pl.when(k == nk - 1)
    def _():
        o_ref[...] = (acc_ref[...] + part).astype(o_ref.dtype)


def _mm_res_kernel(x_ref, w_ref, res_ref, mod_ref, o_ref, acc_ref, *, nk, gate_idx):
    k = pl.program_id(2)
    part = _dot(x_ref[...], w_ref[...])

    def finish(acc):
        o_ref[...] = res_ref[...] + mod_ref[0, gate_idx:gate_idx + 1, :] * acc

    if nk == 1:
        finish(part)
        return

    @pl.when(k == 0)
    def _():
        acc_ref[...] = part

    @pl.when(jnp.logical_and(k > 0, k < nk - 1))
    def _():
        acc_ref[...] += part

    @pl.when(k == nk - 1)
    def _():
        finish(acc_ref[...] + part)


def _matmul(x, w, *, col_start=0, n_cols=None, bm, bn, bk=None, out_dtype=F32, name="matmul"):
    m, kdim = x.shape
    n_cols = w.shape[1] - col_start if n_cols is None else n_cols
    bk = kdim if bk is None else bk
    nk = kdim // bk
    off = col_start // bn
    return pl.pallas_call(
        functools.partial(_mm_kernel, nk=nk),
        grid=(n_cols // bn, m // bm, nk),
        in_specs=[
            pl.BlockSpec((bm, bk), lambda j, i, k: (i, k)),
            pl.BlockSpec((bk, bn), lambda j, i, k: (k, j + off)),
        ],
        out_specs=pl.BlockSpec((bm, bn), lambda j, i, k: (i, j)),
        out_shape=jax.ShapeDtypeStruct((m, n_cols), out_dtype),
        scratch_shapes=[pltpu.VMEM((bm, bn) if nk > 1 else (_SUBLANES, _LANES), F32)],
        compiler_params=_cparams("arbitrary", "arbitrary", "arbitrary"),
        name=name,
    )(x, w)


def _matmul_residual(x, w, res, mod, layer, tok, *, gate_idx, bm, bn, bk=None, name="matmul_res"):
    m, kdim = x.shape
    n = w.shape[1]
    bk = kdim if bk is None else bk
    nk = kdim // bk
    return pl.pallas_call(
        functools.partial(_mm_res_kernel, nk=nk, gate_idx=gate_idx),
        grid=(n // bn, m // bm, nk),
        in_specs=[
            pl.BlockSpec((bm, bk), lambda j, i, k: (i, k)),
            pl.BlockSpec((bk, bn), lambda j, i, k: (k, j)),
            pl.BlockSpec((bm, bn), lambda j, i, k: (i, j)),
            pl.BlockSpec((1, MOD_CHUNKS, bn), lambda j, i, k: (layer * MOD_ROWS + tok.mod_row(i, bm), 0, j)),
        ],
        out_specs=pl.BlockSpec((bm, bn), lambda j, i, k: (i, j)),
        out_shape=jax.ShapeDtypeStruct((m, n), F32),
        scratch_shapes=[pltpu.VMEM((bm, bn) if nk > 1 else (_SUBLANES, _LANES), F32)],
        compiler_params=_cparams("arbitrary", "arbitrary", "arbitrary"),
        name=name,
    )(x, w, res, mod)


def _ffn_up_kernel(h_ref, wa_ref, wv_ref, cw_ref, cb_ref, g_ref, *, n_prompt_tiles, w_prompt, w_latent):
    i = pl.program_id(0)
    x = h_ref[...]
    act = _dot(x, wa_ref[...])
    val = _dot(x, wv_ref[...])
    t, tn = act.shape
    row = lax.broadcasted_iota(jnp.int32, (t, tn), 0)
    prev = pltpu.roll(act, 1, 0)
    nxt = pltpu.roll(act, t - 1, 0)

    def horizontal(width):
        col = row & (width - 1)
        return jnp.where(col > 0, prev, 0.0), jnp.where(col < width - 1, nxt, 0.0)

    def tap_row(kh, left, right):
        return (cw_ref[3 * kh:3 * kh + 1, :] * left + cw_ref[3 * kh + 1:3 * kh + 2, :] * act
                + cw_ref[3 * kh + 2:3 * kh + 3, :] * right)

    @pl.when(i < n_prompt_tiles)
    def _():
        left, right = horizontal(w_prompt)
        conv = tap_row(1, left, right) + cb_ref[...]
        g_ref[...] = (_silu(conv) * val).astype(g_ref.dtype)

    @pl.when(i >= n_prompt_tiles)
    def _():
        left, right = horizontal(w_latent)
        zeros = jnp.zeros((w_latent, tn), F32)
        above = jnp.concatenate([zeros, tap_row(0, left, right)[:t - w_latent, :]], axis=0)
        below = jnp.concatenate([tap_row(2, left, right)[w_latent:, :], zeros], axis=0)
        conv = above + tap_row(1, left, right) + below + cb_ref[...]
        g_ref[...] = (_silu(conv) * val).astype(g_ref.dtype)


def _ffn_up(h, w_up, conv_w, conv_b, tok):
    rows, d = h.shape
    f = w_up.shape[1] // 2
    tm = tok.ls
    tn = 256
    assert tok.np_rows % tm == 0 and tm % tok.lp == 0
    nj = f // tn
    return pl.pallas_call(
        functools.partial(_ffn_up_kernel, n_prompt_tiles=tok.np_rows // tm,
                          w_prompt=tok.lp, w_latent=GRID_W),
        grid=(rows // tm, nj),
        in_specs=[
            pl.BlockSpec((tm, d), lambda i, j: (i, 0)),
            pl.BlockSpec((d, tn), lambda i, j: (0, j)),
            pl.BlockSpec((d, tn), lambda i, j: (0, j + nj)),
            pl.BlockSpec((9, tn), lambda i, j: (0, j)),
            pl.BlockSpec((1, tn), lambda i, j: (0, j)),
        ],
        out_specs=pl.BlockSpec((tm, tn), lambda i, j: (i, j)),
        out_shape=jax.ShapeDtypeStruct((rows, f), BF16),
        compiler_params=_cparams("arbitrary", "arbitrary"),
        name="ffn_up_conv",
    )(h, w_up, w_up, conv_w.reshape(9, f), conv_b.reshape(1, f))


def _gla_kernel(*refs, seq_len, zero_init, emit_state, dk, dv):
    it = iter(refs)
    q_ref, k_ref, v_ref, r_ref, low_ref, w2_ref, bg_ref, ng_ref = (next(it) for _ in range(8))
    s0_ref = None if zero_init else next(it)
    y_ref = next(it)
    sout_ref = next(it) if emit_state else None
    g_scr, o_scr, s_scr = next(it), next(it), next(it)

    nc = seq_len // CHUNK
    masks, tris = _chunk_masks()
    mid_row = (CHUNK // 2, CHUNK // 2 - 1)
    end_row = (CHUNK - 1, 0)
    q_scale = dk ** -0.5

    low = low_ref[...].astype(BF16)
    for e in range(2):
        z = _dot(low, w2_ref[e]) + bg_ref[e]
        g_scr[e] = _log_sigmoid(z) * (1.0 / GLA_GATE_NORM)
        if zero_init:
            s_scr[e] = jnp.zeros((dv, dk), F32)
        else:
            s_scr[e] = s0_ref[0, e, 0].T

    def chunk_step(e, c):
        r0 = pl.multiple_of(c * CHUNK, CHUNK)
        g = g_scr[e, pl.ds(r0, CHUNK), :]
        gc = _exact_dot_left01(tris[e], g)
        g_mid = gc[mid_row[e]:mid_row[e] + 1, :]
        g_end = gc[end_row[e]:end_row[e] + 1, :]
        q = q_ref[pl.ds(r0, CHUNK), :] * q_scale
        k = k_ref[pl.ds(r0, CHUNK), :]
        v = v_ref[pl.ds(r0, CHUNK), :]
        s_t = s_scr[e]
        scores = _dot_nt((q * jnp.exp(gc - g_mid)).astype(BF16), (k * jnp.exp(g_mid - gc)).astype(BF16))
        o = _dot(jnp.where(masks[e], scores, 0.0).astype(BF16), v)
        o = o + _dot_nt((q * jnp.exp(gc)).astype(BF16), s_t.astype(BF16))
        u_t = _dot_tn(v, (k * jnp.exp(g_end - gc)).astype(BF16))
        s_scr[e] = s_t * jnp.exp(g_end) + u_t
        o_scr[e, pl.ds(r0, CHUNK), :] = o

    def body(i, carry):
        chunk_step(0, i)
        chunk_step(1, nc - 1 - i)
        return carry

    lax.fori_loop(0, nc, body, 0)

    blk = 256
    for r0 in range(0, seq_len, blk):
        o = o_scr[0, r0:r0 + blk, :] + o_scr[1, r0:r0 + blk, :]
        ms = jnp.mean(o * o, axis=-1, keepdims=True)
        y = o * lax.rsqrt(ms + NORM_EPS) * ng_ref[...]
        y_ref[r0:r0 + blk, :] = (y * _silu(r_ref[r0:r0 + blk, :])).astype(y_ref.dtype)

    if emit_state:
        for e in range(2):
            sout_ref[0, e, 0] = s_scr[e].T


def _gla_scan(qk, v, r, low, w2, bg, ng, s0, tok, *, latent):
    dk = qk.shape[1] // (2 * GLA_HEADS)
    dv = v.shape[1] // GLA_HEADS
    if latent:
        nseq, seq_len, rb0 = tok.bs, tok.ls, tok.np_rows // tok.ls
    else:
        nseq, seq_len, rb0 = tok.bp, tok.lp, 0
    in_specs = [
        pl.BlockSpec((seq_len, dk), lambda b, h: (rb0 + b, h)),
        pl.BlockSpec((seq_len, dk), lambda b, h: (rb0 + b, GLA_HEADS + h)),
        pl.BlockSpec((seq_len, dv), lambda b, h: (rb0 + b, h)),
        pl.BlockSpec((seq_len, dv), lambda b, h: (rb0 + b, h)),
        pl.BlockSpec((seq_len, _LANES), lambda b, h: (rb0 + b, 0)),
        pl.BlockSpec((2, _LANES, dk), lambda b, h: (0, 0, h)),
        pl.BlockSpec((2, 1, dk), lambda b, h: (0, 0, h)),
        pl.BlockSpec((1, dv), lambda b, h: (0, 0)),
    ]
    args = [qk, qk, v, r, low, w2, bg, ng]
    if latent:
        in_specs.append(pl.BlockSpec((1, 2, 1, dk, dv), lambda b, h: (b, 0, h, 0, 0)))
        args.append(s0)
    y_spec = pl.BlockSpec((seq_len, dv), lambda b, h: (b, h))
    y_shape = jax.ShapeDtypeStruct((nseq * seq_len, v.shape[1]), BF16)
    if latent:
        out_specs, out_shape = y_spec, y_shape
    else:
        out_specs = (y_spec, pl.BlockSpec((1, 2, 1, dk, dv), lambda b, h: (b, 0, h, 0, 0)))
        out_shape = (y_shape, jax.ShapeDtypeStruct((nseq, 2, GLA_HEADS, dk, dv), F32))
    return pl.pallas_call(
        functools.partial(_gla_kernel, seq_len=seq_len, zero_init=not latent, emit_state=not latent,
                          dk=dk, dv=dv),
        grid=(nseq, GLA_HEADS),
        in_specs=in_specs,
        out_specs=out_specs,
        out_shape=out_shape,
        scratch_shapes=[
            pltpu.VMEM((2, seq_len, dk), F32),
            pltpu.VMEM((2, seq_len, dv), F32),
            pltpu.VMEM((2, dv, dk), F32),
        ],
        compiler_params=_cparams("arbitrary", "arbitrary"),
        name="gla_scan_latent" if latent else "gla_scan_prompt",
    )(*args)


def _gla_mixer(h, s_lat0, w, tok):
    w_in, w_g1cat, w2, bg, ng = w["w_in"], w["w_g1cat"], w["w2"], w["bg"], w["ng"]
    qkw = 2 * GLA_HEADS * (w2.shape[2] // GLA_HEADS)
    vw = (w_in.shape[1] - qkw) // 2
    qk = _matmul(h, w_in, col_start=0, n_cols=qkw, bm=1024, bn=1024, name="gla_in_qk")
    v = _matmul(h, w_in, col_start=qkw, n_cols=vw, bm=1024, bn=1024, out_dtype=BF16, name="gla_in_v")
    r = _matmul(h, w_in, col_start=qkw + vw, n_cols=vw, bm=1024, bn=1024, name="gla_in_r")
    low = _matmul(h, w_g1cat, bm=1024, bn=_LANES, name="gla_in_gate")
    y_p, s_new = _gla_scan(qk, v, r, low, w2, bg, ng, None, tok, latent=False)
    y_s = _gla_scan(qk, v, r, low, w2, bg, ng, s_lat0, tok, latent=True)
    return y_p, y_s, s_new


def _ssd_kernel(*refs, seq_len, zero_init, emit_state):
    it = iter(refs)
    (z_ref, x_ref, b_ref, c_ref, dt_ref, cwx_ref, cbx_ref, cwb_ref, cbb_ref, cwc_ref, cbc_ref,
     dtb_ref, alog_ref, dskip_ref, ng_ref, e_ref) = (next(it) for _ in range(16))
    s0_ref = None if zero_init else next(it)
    y_ref = next(it)
    sout_ref = next(it) if emit_state else None
    xs_scr, bm_scr, cm_scr, dt_scr, y_scr, s_scr = (next(it) for _ in range(6))

    hpg = xs_scr.shape[1] // SSD_HEADDIM
    nc = seq_len // CHUNK
    masks, tris = _chunk_masks()
    end_row = (CHUNK - 1, 0)

    blk = 256
    for r0 in range(0, seq_len, blk):
        xs = _silu(_conv4_block(x_ref, cwx_ref, cbx_ref, r0, blk, seq_len))
        xs_scr[r0:r0 + blk, :] = xs
        y_scr[r0:r0 + blk, :] = dskip_ref[...] * xs
        bm_scr[r0:r0 + blk, :] = _silu(_conv4_block(b_ref, cwb_ref, cbb_ref, r0, blk, seq_len)).astype(BF16)
        cm_scr[r0:r0 + blk, :] = _silu(_conv4_block(c_ref, cwc_ref, cbc_ref, r0, blk, seq_len)).astype(BF16)
        dt_scr[r0:r0 + blk, :] = _softplus(dt_ref[r0:r0 + blk, :] + dtb_ref[0])
    a_row = -jnp.exp(alog_ref[0])

    for e in range(2):
        if zero_init:
            s_scr[e] = jnp.zeros(s_scr.shape[1:], F32)
        else:
            s_scr[e] = s0_ref[0, e, 0].T

    def chunk_step(e, c):
        r0 = pl.multiple_of(c * CHUNK, CHUNK)
        dt = dt_scr[pl.ds(r0, CHUNK), :]
        cum = _exact_dot_left01(tris[e], dt * a_row)
        cum_t = cum.T
        expand = e_ref[e]
        cum_x = _exact_dot_right01(cum, expand)
        dt_x = _exact_dot_right01(dt, expand)
        xd = xs_scr[pl.ds(r0, CHUNK), :] * dt_x
        xd_bf = xd.astype(BF16)
        bm = bm_scr[pl.ds(r0, CHUNK), :]
        cm = cm_scr[pl.ds(r0, CHUNK), :]
        cb = _dot_nt(cm, bm)
        parts = []
        for hh in range(hpg):
            lane = slice(hh * SSD_HEADDIM, (hh + 1) * SSD_HEADDIM)
            seg = cum_x[:, lane] - cum_t[e * hpg + hh:e * hpg + hh + 1, :]
            decay = jnp.where(masks[e], jnp.exp(jnp.where(masks[e], seg, 0.0)), 0.0)
            parts.append(_dot((cb * decay).astype(BF16), xd_bf[:, lane]))
        y = jnp.concatenate(parts, axis=1)
        s_t = s_scr[e]
        y = y + _dot(cm, s_t.astype(BF16)) * jnp.exp(cum_x)
        cum_end = cum_x[end_row[e]:end_row[e] + 1, :]
        st_t = _dot_tn(bm, (xd * jnp.exp(cum_end - cum_x)).astype(BF16))
        s_scr[e] = s_t * jnp.exp(cum_end) + st_t
        y_scr[pl.ds(r0, CHUNK), :] += y

    def body(i, carry):
        chunk_step(0, i)
        chunk_step(1, nc - 1 - i)
        return carry

    lax.fori_loop(0, nc, body, 0)

    for r0 in range(0, seq_len, blk):
        y = y_scr[r0:r0 + blk, :] * _silu(z_ref[r0:r0 + blk, :])
        ms = jnp.mean(y * y, axis=-1, keepdims=True)
        y_ref[r0:r0 + blk, :] = (y * lax.rsqrt(ms + NORM_EPS) * ng_ref[...]).astype(y_ref.dtype)

    if emit_state:
        for e in range(2):
            sout_ref[0, e, 0] = s_scr[e].T


def _ssd_scan(proj, dtp, w, s0, tok, *, latent):
    di = w["di"]
    gw = di // SSD_GROUPS
    if latent:
        nseq, seq_len, rb0 = tok.bs, tok.ls, tok.np_rows // tok.ls
    else:
        nseq, seq_len, rb0 = tok.bp, tok.lp, 0
    nx = di // gw
    nb = (2 * di) // SSD_STATE
    ncb = nb + SSD_GROUPS

    def rows(width, col):
        return pl.BlockSpec((seq_len, width), lambda b, g: (rb0 + b, col(g)))

    def vec(width, col, nrows=1):
        return pl.BlockSpec((nrows, width), lambda b, g: (0, col(g)))

    in_specs = [
        rows(gw, lambda g: g),
        rows(gw, lambda g: nx + g),
        rows(SSD_STATE, lambda g: nb + g),
        rows(SSD_STATE, lambda g: ncb + g),
        pl.BlockSpec((seq_len, _LANES), lambda b, g: (rb0 + b, g)),
        vec(gw, lambda g: g, 4), vec(gw, lambda g: g),
        vec(SSD_STATE, lambda g: di // SSD_STATE + g, 4), vec(SSD_STATE, lambda g: di // SSD_STATE + g),
        vec(SSD_STATE, lambda g: di // SSD_STATE + SSD_GROUPS + g, 4),
        vec(SSD_STATE, lambda g: di // SSD_STATE + SSD_GROUPS + g),
        pl.BlockSpec((1, 1, _LANES), lambda b, g: (g, 0, 0)),
        pl.BlockSpec((1, 1, _LANES), lambda b, g: (g, 0, 0)),
        vec(gw, lambda g: g),
        vec(gw, lambda g: g),
        pl.BlockSpec((2, _LANES, gw), lambda b, g: (0, 0, 0)),
    ]
    args = [proj, proj, proj, proj, dtp, w["conv_w"], w["conv_b"], w["conv_w"], w["conv_b"],
            w["conv_w"], w["conv_b"], w["dt_bias"], w["a_log"], w["d_skip"], w["ng"], w["expand"]]
    st_spec = pl.BlockSpec((1, 2, 1, gw, SSD_STATE), lambda b, g: (b, 0, g, 0, 0))
    if latent:
        in_specs.append(st_spec)
        args.append(s0)
    y_spec = pl.BlockSpec((seq_len, gw), lambda b, g: (b, g))
    y_shape = jax.ShapeDtypeStruct((nseq * seq_len, di), BF16)
    if latent:
        out_specs, out_shape = y_spec, y_shape
    else:
        out_specs = (y_spec, st_spec)
        out_shape = (y_shape, jax.ShapeDtypeStruct((nseq, 2, SSD_GROUPS, gw, SSD_STATE), F32))

    return pl.pallas_call(
        functools.partial(_ssd_kernel, seq_len=seq_len, zero_init=not latent, emit_state=not latent),
        grid=(nseq, SSD_GROUPS),
        in_specs=in_specs,
        out_specs=out_specs,
        out_shape=out_shape,
        scratch_shapes=[
            pltpu.VMEM((seq_len, gw), F32),
            pltpu.VMEM((seq_len, SSD_STATE), BF16),
            pltpu.VMEM((seq_len, SSD_STATE), BF16),
            pltpu.VMEM((seq_len, _LANES), F32),
            pltpu.VMEM((seq_len, gw), F32),
            pltpu.VMEM((2, SSD_STATE, gw), F32),
        ],
        compiler_params=_cparams("arbitrary", "arbitrary"),
        name="ssd_scan_latent" if latent else "ssd_scan_prompt",
    )(*args)


def _lru_kernel(*refs, seq_len, zero_init, emit_state):
    it = iter(refs)
    (x_ref, gate_ref, cw_ref, cb_ref, wa_ref, ba_ref, wi_ref, bi_ref, lam_ref) = (next(it) for _ in range(9))
    s0_ref = None if zero_init else next(it)
    y_ref = next(it)
    sout_ref = next(it) if emit_state else None
    a_scr, u_scr, h_scr = next(it), next(it), next(it)
    width = a_scr.shape[2]

    blk = 256
    for r0 in range(0, seq_len, blk):
        xc = _conv4_block(x_ref, cw_ref, cb_ref, r0, blk, seq_len)
        xc_bf = xc.astype(BF16)
        for e in range(2):
            rg = jax.nn.sigmoid(_dot(xc_bf, wa_ref[e, 0]) + ba_ref[e])
            ig = jax.nn.sigmoid(_dot(xc_bf, wi_ref[e, 0]) + bi_ref[e])
            log_a = -LRU_C * rg * _softplus(-lam_ref[e])
            a_scr[e, r0:r0 + blk, :] = jnp.exp(log_a)
            u_scr[e, r0:r0 + blk, :] = jnp.sqrt(_neg_expm1(2.0 * log_a)) * ig * xc

    nt = seq_len // _SUBLANES
    row = lax.broadcasted_iota(jnp.int32, (_SUBLANES, width), 0)

    def tile_scan(e, t, carry):
        r0 = pl.multiple_of(t * _SUBLANES, _SUBLANES)
        a = a_scr[e, pl.ds(r0, _SUBLANES), :]
        u = u_scr[e, pl.ds(r0, _SUBLANES), :]
        for s in (1, 2, 4):
            if e == 0:
                valid = row >= s
                a_sh, u_sh = pltpu.roll(a, s, 0), pltpu.roll(u, s, 0)
            else:
                valid = row < _SUBLANES - s
                a_sh, u_sh = pltpu.roll(a, _SUBLANES - s, 0), pltpu.roll(u, _SUBLANES - s, 0)
            u = u + a * jnp.where(valid, u_sh, 0.0)
            a = a * jnp.where(valid, a_sh, 1.0)
        hcur = u + a * carry
        last = _SUBLANES - 1 if e == 0 else 0
        return hcur, hcur[last:last + 1, :]

    def body(i, carry):
        cf, cbk = carry
        hf, cf = tile_scan(0, i, cf)
        r0 = pl.multiple_of(i * _SUBLANES, _SUBLANES)
        h_scr[0, pl.ds(r0, _SUBLANES), :] = hf
        tb = nt - 1 - i
        hb, cbk = tile_scan(1, tb, cbk)
        rb = pl.multiple_of(tb * _SUBLANES, _SUBLANES)
        h_scr[1, pl.ds(rb, _SUBLANES), :] = hb
        return cf, cbk

    if zero_init:
        init = (jnp.zeros((1, width), F32), jnp.zeros((1, width), F32))
    else:
        init = (s0_ref[0, 0], s0_ref[0, 1])
    cf, cbk = lax.fori_loop(0, nt, body, init)

    for r0 in range(0, seq_len, blk):
        hsum = h_scr[0, r0:r0 + blk, :] + h_scr[1, r0:r0 + blk, :]
        y_ref[r0:r0 + blk, :] = (hsum * _gelu_tanh(gate_ref[r0:r0 + blk, :])).astype(y_ref.dtype)

    if emit_state:
        sout_ref[0, 0] = cf
        sout_ref[0, 1] = cbk


def _lru_scan(proj, w, s0, tok, *, latent):
    width = proj.shape[1] // 2
    bw = width // LRU_BLOCKS
    if latent:
        nseq, seq_len, rb0 = tok.bs, tok.ls, tok.np_rows // tok.ls
    else:
        nseq, seq_len, rb0 = tok.bp, tok.lp, 0
    pair = pl.BlockSpec((2, 1, bw), lambda b, n: (0, 0, n))
    in_specs = [
        pl.BlockSpec((seq_len, bw), lambda b, n: (rb0 + b, n)),
        pl.BlockSpec((seq_len, bw), lambda b, n: (rb0 + b, LRU_BLOCKS + n)),
        pl.BlockSpec((4, bw), lambda b, n: (0, n)),
        pl.BlockSpec((1, bw), lambda b, n: (0, n)),
        pl.BlockSpec((2, 1, bw, bw), lambda b, n: (0, n, 0, 0)), pair,
        pl.BlockSpec((2, 1, bw, bw), lambda b, n: (0, n, 0, 0)), pair,
        pair,
    ]
    args = [proj, proj, w["conv_w"], w["conv_b"], w["w_a"], w["b_a"], w["w_i"], w["b_i"], w["lam"]]
    st_spec = pl.BlockSpec((1, 2, 1, bw), lambda b, n: (b, 0, 0, n))
    if latent:
        in_specs.append(st_spec)
        args.append(s0)
    y_spec = pl.BlockSpec((seq_len, bw), lambda b, n: (b, n))
    y_shape = jax.ShapeDtypeStruct((nseq * seq_len, width), BF16)
    if latent:
        out_specs, out_shape = y_spec, y_shape
    else:
        out_specs = (y_spec, st_spec)
        out_shape = (y_shape, jax.ShapeDtypeStruct((nseq, 2, 1, width), F32))
    return pl.pallas_call(
        functools.partial(_lru_kernel, seq_len=seq_len, zero_init=not latent, emit_state=not latent),
        grid=(nseq, LRU_BLOCKS),
        in_specs=in_specs,
        out_specs=out_specs,
        out_shape=out_shape,
        scratch_shapes=[
            pltpu.VMEM((2, seq_len, bw), F32),
            pltpu.VMEM((2, seq_len, bw), F32),
            pltpu.VMEM((2, seq_len, bw), F32),
        ],
        compiler_params=_cparams("arbitrary", "arbitrary"),
        name="lru_scan_latent" if latent else "lru_scan_prompt",
    )(*args)


def _merge_rows(y_p, y_s):
    return jnp.concatenate([y_p, y_s], axis=0)


def _prep_gla(w_in, w_g1, w_g2, b_g, norm_g):
    d = w_in.shape[0]
    qkw = w_g2.shape[2]
    g1 = jnp.concatenate([w_g1[0], w_g1[1]], axis=1)
    g1 = jnp.pad(g1, ((0, 0), (0, _LANES - 2 * GLA_RANK)))
    w2 = jnp.zeros((2, _LANES, qkw), F32)
    w2 = w2.at[0, 0:GLA_RANK].set(w_g2[0]).at[1, GLA_RANK:2 * GLA_RANK].set(w_g2[1])
    return {
        "w_in": w_in.astype(BF16), "w_g1cat": g1.astype(BF16), "w2": w2.astype(BF16),
        "bg": b_g.reshape(2, 1, qkw), "ng": norm_g.reshape(1, -1),
    }


def _prep_ssd(w_in, conv_w, conv_b, a_log, dt_bias, d_skip, norm_g):
    heads = a_log.shape[1]
    di = heads * SSD_HEADDIM
    hpg = heads // SSD_GROUPS
    gw = di // SSD_GROUPS
    main = 2 * di + 2 * SSD_GROUPS * SSD_STATE

    def by_group(t):
        t = t.reshape(2, SSD_GROUPS, hpg).transpose(1, 0, 2).reshape(SSD_GROUPS, 1, 2 * hpg)
        return jnp.pad(t, ((0, 0), (0, 0), (0, _LANES - 2 * hpg)))

    w_dt = w_in[:, main:].reshape(-1, 2, SSD_GROUPS, hpg).transpose(0, 2, 1, 3)
    w_dt = w_dt.reshape(-1, SSD_GROUPS, 2 * hpg)
    w_dt = jnp.pad(w_dt, ((0, 0), (0, 0), (0, _LANES - 2 * hpg))).reshape(-1, SSD_GROUPS * _LANES)
    lane = jnp.arange(_LANES)[:, None]
    chan = jnp.arange(gw)[None, :] // SSD_HEADDIM
    expand = jnp.stack([(lane == e * hpg + chan) for e in range(2)]).astype(BF16)
    return {
        "di": di, "w_main": w_in.astype(BF16), "w_dt": w_dt.astype(BF16), "n_main": main,
        "conv_w": conv_w, "conv_b": conv_b.reshape(1, -1),
        "dt_bias": by_group(dt_bias), "a_log": by_group(a_log),
        "d_skip": jnp.repeat(d_skip, SSD_HEADDIM).reshape(1, di), "ng": norm_g.reshape(1, di),
        "expand": expand,
    }


def _prep_lru(w_in, conv_w, conv_b, w_a, b_a, w_i, b_i, lam):
    width = conv_w.shape[1]
    return {
        "w_in": w_in.astype(BF16), "conv_w": conv_w, "conv_b": conv_b.reshape(1, width),
        "w_a": w_a.astype(BF16), "b_a": b_a.reshape(2, 1, width),
        "w_i": w_i.astype(BF16), "b_i": b_i.reshape(2, 1, width), "lam": lam.reshape(2, 1, width),
    }


def kernel(x_prompt, x_sample, state_gla, state_ssd, state_lru, c, c_ctx, w_mod, b_mod, norm_mix_g,
           norm_ffn_g, ffn_w_up, ffn_conv_w, ffn_conv_b, ffn_w_down, final_norm_g, gla_w_in, gla_w_g1,
           gla_w_g2, gla_b_g, gla_norm_g, gla_w_out, ssd_w_in, ssd_conv_w, ssd_conv_b, ssd_a_log,
           ssd_dt_bias, ssd_d, ssd_norm_g, ssd_w_out, lru_w_in, lru_conv_w, lru_conv_b, lru_w_a, lru_b_a,
           lru_w_i, lru_b_i, lru_lambda, lru_w_out):
    bp, lp, d = x_prompt.shape
    bs, ls, _ = x_sample.shape
    depth = w_mod.shape[0]
    tok = _Tokens(bp, lp, bs, ls)
    assert bs + 1 <= MOD_ROWS and ls // GRID_W * GRID_W == ls

    x = jnp.concatenate([x_prompt.reshape(bp * lp, d), x_sample.reshape(bs * ls, d)], axis=0)
    c_rows = jnp.concatenate([c_ctx[None], c, jnp.zeros((MOD_ROWS - 1 - bs, d), F32)], axis=0)
    mod = _mod_table(c_rows, w_mod, b_mod).reshape(depth * MOD_ROWS, MOD_CHUNKS, d)

    gla_new, ssd_new, lru_new = [], [], []
    for l in range(depth):
        kind, j = l % 3, l // 3
        h = _norm_mod(x, norm_mix_g[l], mod, l, tok, shift_idx=0, scale_idx=1)
        if kind == 0:
            w = _prep_gla(gla_w_in[j], gla_w_g1[j], gla_w_g2[j], gla_b_g[j], gla_norm_g[j])
            y_p, y_s, s_new = _gla_mixer(h, state_gla[:, j], w, tok)
            gla_new.append(s_new)
            w_out = gla_w_out[j]
        elif kind == 1:
            w = _prep_ssd(ssd_w_in[j], ssd_conv_w[j], ssd_conv_b[j], ssd_a_log[j], ssd_dt_bias[j],
                          ssd_d[j], ssd_norm_g[j])
            proj = _matmul(h, w["w_main"], col_start=0, n_cols=w["n_main"], bm=1024, bn=1024,
                           name="ssd_in_main")
            dtp = _matmul(h, w["w_dt"], bm=1024, bn=SSD_GROUPS * _LANES, name="ssd_in_dt")
            gw = w["di"] // SSD_GROUPS
            s_lat0 = state_ssd[:, j].reshape(bs, 2, SSD_GROUPS, gw, SSD_STATE)
            y_p, s_new = _ssd_scan(proj, dtp, w, None, tok, latent=False)
            y_s = _ssd_scan(proj, dtp, w, s_lat0, tok, latent=True)
            ssd_new.append(s_new.reshape(bp, 2, -1, SSD_HEADDIM, SSD_STATE))
            w_out = ssd_w_out[j]
        else:
            w = _prep_lru(lru_w_in[j], lru_conv_w[j], lru_conv_b[j], lru_w_a[j], lru_b_a[j],
                          lru_w_i[j], lru_b_i[j], lru_lambda[j])
            proj = _matmul(h, w["w_in"], bm=1024, bn=1024, name="lru_in")
            s_lat0 = state_lru[:, j].reshape(bs, 2, 1, -1)
            y_p, s_new = _lru_scan(proj, w, None, tok, latent=False)
            y_s = _lru_scan(proj, w, s_lat0, tok, latent=True)
            lru_new.append(s_new.reshape(bp, 2, -1))
            w_out = lru_w_out[j]
        y = _merge_rows(y_p, y_s)
        kout = w_out.shape[0]
        x = _matmul_residual(y, w_out.astype(BF16), x, mod, l, tok, gate_idx=2, bm=1024, bn=1024,
                             bk=min(kout, 2048), name="mixer_out")
        h = _norm_mod(x, norm_ffn_g[l], mod, l, tok, shift_idx=3, scale_idx=4)
        g = _ffn_up(h, ffn_w_up[l].astype(BF16), ffn_conv_w[l], ffn_conv_b[l], tok)
        x = _matmul_residual(g, ffn_w_down[l].astype(BF16), x, mod, l, tok, gate_idx=5, bm=1024, bn=1024,
                             bk=ffn_w_down.shape[1] // 4, name="ffn_down")
    y = _final_norm(x, final_norm_g)
    y_prompt = y[:tok.np_rows].reshape(bp, lp, d)
    y_sample = y[tok.np_rows:].reshape(bs, ls, d)
    return (y_prompt, y_sample, jnp.stack(gla_new, axis=1), jnp.stack(ssd_new, axis=1),
            jnp.stack(lru_new, axis=1))
```

```python
import functools

import jax
import jax.numpy as jnp
from jax import lax
from jax.experimental import pallas as pl
from jax.experimental.pallas import tpu as pltpu

F32 = jnp.float32
BF16 = jnp.bfloat16

_V7X_VMEM_BYTES = 64 * 1024 * 1024
_VMEM_LIMIT = _V7X_VMEM_BYTES - 8 * 1024 * 1024
_SUBLANES = 8
_LANES = 128

NORM_EPS = 1e-6
MOD_CHUNKS = 6
MOD_ROWS = 8
GRID_W = 64

GLA_HEADS = 4
GLA_RANK = 16
GLA_GATE_NORM = 16.0
SSD_HEADDIM = 64
SSD_GROUPS = 8
SSD_STATE = 128
LRU_BLOCKS = 8
LRU_C = 8.0
CHUNK = 64


def _cparams(*sem):
    return pltpu.CompilerParams(dimension_semantics=sem, vmem_limit_bytes=_VMEM_LIMIT)


def _silu(x):
    return x * jax.nn.sigmoid(x)


def _softplus(x):
    return jnp.maximum(x, 0.0) + jnp.log1p(jnp.exp(-jnp.abs(x)))


def _log_sigmoid(x):
    return jnp.minimum(x, 0.0) - jnp.log1p(jnp.exp(-jnp.abs(x)))


def _gelu_tanh(x):
    return 0.5 * x * (1.0 + jnp.tanh(0.7978845608028654 * (x + 0.044715 * (x * x * x))))


def _neg_expm1(x):
    t = jnp.tanh(0.5 * x)
    return -2.0 * t / (1.0 - t)


def _split3(x):
    x1 = x.astype(BF16)
    r1 = x - x1.astype(F32)
    x2 = r1.astype(BF16)
    x3 = (r1 - x2.astype(F32)).astype(BF16)
    return x1, x2, x3


def _dot(a, b):
    return jnp.dot(a, b, preferred_element_type=F32)


def _dot_nt(a, b):
    return lax.dot_general(a, b, (((1,), (1,)), ((), ())), preferred_element_type=F32)


def _dot_tn(a, b):
    return lax.dot_general(a, b, (((0,), (0,)), ((), ())), preferred_element_type=F32)


def _exact_dot_left01(m01, x):
    x1, x2, x3 = _split3(x)
    return _dot(m01, x1) + _dot(m01, x2) + _dot(m01, x3)


def _exact_select_copies(x, sel01, ncopy_lanes):
    t1 = x.astype(BF16).astype(F32)
    r1 = x - t1
    t2 = r1.astype(BF16).astype(F32)
    lane = lax.broadcasted_iota(jnp.int32, x.shape, 1)
    terms = jnp.where(lane < ncopy_lanes, t1, jnp.where(lane < 2 * ncopy_lanes, t2, r1 - t2))
    return _dot(terms.astype(BF16), sel01)


def _chunk_masks():
    ii = lax.broadcasted_iota(jnp.int32, (CHUNK, CHUNK), 0)
    jj = lax.broadcasted_iota(jnp.int32, (CHUNK, CHUNK), 1)
    masks = (jj <= ii, jj >= ii)
    tris = tuple(jnp.where(m, 1.0, 0.0).astype(BF16) for m in masks)
    return masks, tris


def _block_chunk_masks(rows):
    ii = lax.broadcasted_iota(jnp.int32, (rows, rows), 0)
    jj = lax.broadcasted_iota(jnp.int32, (rows, rows), 1)
    shift = CHUNK.bit_length() - 1
    same = jnp.where(jnp.right_shift(ii, shift) == jnp.right_shift(jj, shift), 1, 0)
    masks = (same * jnp.where(jj <= ii, 1, 0) > 0, same * jnp.where(jj >= ii, 1, 0) > 0)
    tris = tuple(jnp.where(m, 1.0, 0.0).astype(BF16) for m in masks)
    return masks, tris


def _shifted_rows(ref, r0, nrows, off, total):
    lo, hi = r0 + off, r0 + off + nrows
    clo, chi = max(lo, 0), min(hi, total)
    x = ref[clo:chi, :]
    parts = []
    if clo > lo:
        parts.append(jnp.zeros((clo - lo, x.shape[1]), x.dtype))
    parts.append(x)
    if hi > chi:
        parts.append(jnp.zeros((hi - chi, x.shape[1]), x.dtype))
    return x if len(parts) == 1 else jnp.concatenate(parts, axis=0)


def _conv4_block(ref, w_ref, b_ref, r0, nrows, total):
    acc = b_ref[...] + w_ref[0:1, :] * _shifted_rows(ref, r0, nrows, -1, total)
    acc = acc + w_ref[1:2, :] * ref[r0:r0 + nrows, :]
    acc = acc + w_ref[2:3, :] * _shifted_rows(ref, r0, nrows, 1, total)
    acc = acc + w_ref[3:4, :] * _shifted_rows(ref, r0, nrows, 2, total)
    return acc


def _mod_kernel(c_ref, w_ref, b_ref, o_ref):
    c = c_ref[...]
    s = _silu(c).astype(BF16)
    o_ref[0] = _dot(s, w_ref[0].astype(BF16)) + b_ref[0]


def _mod_table(c_rows, w_mod, b_mod):
    depth, d, n = w_mod.shape
    bn = 1024
    return pl.pallas_call(
        _mod_kernel,
        grid=(depth, n // bn),
        in_specs=[
            pl.BlockSpec((MOD_ROWS, d), lambda l, j: (0, 0)),
            pl.BlockSpec((1, d, bn), lambda l, j: (l, 0, j)),
            pl.BlockSpec((1, 1, bn), lambda l, j: (l, 0, j)),
        ],
        out_specs=pl.BlockSpec((1, MOD_ROWS, bn), lambda l, j: (l, 0, j)),
        out_shape=jax.ShapeDtypeStruct((depth, MOD_ROWS, n), F32),
        compiler_params=_cparams("arbitrary", "arbitrary"),
        name="adaln_table",
    )(c_rows, w_mod, b_mod.reshape(depth, 1, n))


class _Tokens:
    def __init__(self, n_prompt, l_prompt, n_latent, l_latent):
        self.bp, self.lp, self.bs, self.ls = n_prompt, l_prompt, n_latent, l_latent
        self.np_rows = n_prompt * l_prompt
        self.ns_rows = n_latent * l_latent
        self.rows = self.np_rows + self.ns_rows

    def mod_row(self, i, bm):
        r0 = i * bm
        return jnp.where(r0 < self.np_rows, 0, 1 + (r0 - self.np_rows) // self.ls)


def _norm_mod_kernel(x_ref, g_ref, mod_ref, h_ref, *, shift_idx, scale_idx):
    x = x_ref[...]
    ms = jnp.mean(x * x, axis=-1, keepdims=True)
    y = x * lax.rsqrt(ms + NORM_EPS) * g_ref[...]
    h = y * (1.0 + mod_ref[0, scale_idx:scale_idx + 1, :]) + mod_ref[0, shift_idx:shift_idx + 1, :]
    h_ref[...] = h.astype(h_ref.dtype)


def _norm_mod(x, g, mod, layer, tok, *, shift_idx, scale_idx):
    rows, d = x.shape
    bm = 512
    return pl.pallas_call(
        functools.partial(_norm_mod_kernel, shift_idx=shift_idx, scale_idx=scale_idx),
        grid=(rows // bm,),
        in_specs=[
            pl.BlockSpec((bm, d), lambda i: (i, 0)),
            pl.BlockSpec((1, d), lambda i: (0, 0)),
            pl.BlockSpec((1, MOD_CHUNKS, d), lambda i: (layer * MOD_ROWS + tok.mod_row(i, bm), 0, 0)),
        ],
        out_specs=pl.BlockSpec((bm, d), lambda i: (i, 0)),
        out_shape=jax.ShapeDtypeStruct((rows, d), BF16),
        compiler_params=_cparams("arbitrary"),
        name="norm_adaln",
    )(x, g.reshape(1, d), mod)


def _final_norm_kernel(x_ref, g_ref, o_ref):
    x = x_ref[...]
    ms = jnp.mean(x * x, axis=-1, keepdims=True)
    o_ref[...] = x * lax.rsqrt(ms + NORM_EPS) * g_ref[...]


def _final_norm(x, g):
    rows, d = x.shape
    bm = 512
    return pl.pallas_call(
        _final_norm_kernel,
        grid=(rows // bm,),
        in_specs=[pl.BlockSpec((bm, d), lambda i: (i, 0)), pl.BlockSpec((1, d), lambda i: (0, 0))],
        out_specs=pl.BlockSpec((bm, d), lambda i: (i, 0)),
        out_shape=jax.ShapeDtypeStruct((rows, d), F32),
        compiler_params=_cparams("arbitrary"),
        name="final_norm",
    )(x, g.reshape(1, d))


def _mm_kernel(*refs, nk, gate_idx):
    if gate_idx is None:
        x_ref, w_ref, o_ref, acc_ref, wb_ref = refs
        res_ref = mod_ref = None
    else:
        x_ref, w_ref, res_ref, mod_ref, o_ref, acc_ref, wb_ref = refs
    i, k = pl.program_id(1), pl.program_id(2)
    if w_ref.dtype == BF16:
        w = w_ref[0]
    elif nk == 1:
        @pl.when(i == 0)
        def _():
            wb_ref[...] = w_ref[0].astype(BF16)

        w = wb_ref[...]
    else:
        w = w_ref[0].astype(BF16)
    part = _dot(x_ref[...], w)

    def finish(acc):
        if gate_idx is None:
            o_ref[...] = acc.astype(o_ref.dtype)
        else:
            o_ref[...] = res_ref[...] + mod_ref[0, gate_idx:gate_idx + 1, :] * acc

    if nk == 1:
        finish(part)
        return

    @pl.when(k == 0)
    def _():
        acc_ref[...] = part

    @pl.when(jnp.logical_and(k > 0, k < nk - 1))
    def _():
        acc_ref[...] += part

    @pl.when(k == nk - 1)
    def _():
        finish(acc_ref[...] + part)


def _matmul(x, w, layer, *, col_start=0, n_cols=None, bm, bn, bk=None, out_dtype=F32, residual=None,
            name="matmul"):
    m, kdim = x.shape
    n_cols = w.shape[2] - col_start if n_cols is None else n_cols
    bk = kdim if bk is None else bk
    nk = kdim // bk
    off = col_start // bn
    in_specs = [
        pl.BlockSpec((bm, bk), lambda j, i, k: (i, k)),
        pl.BlockSpec((1, bk, bn), lambda j, i, k: (layer, k, j + off)),
    ]
    args = [x, w]
    gate_idx = None
    if residual is not None:
        res, mod, mod_layer, tok, gate_idx = residual
        in_specs += [
            pl.BlockSpec((bm, bn), lambda j, i, k: (i, j)),
            pl.BlockSpec((1, MOD_CHUNKS, bn),
                         lambda j, i, k: (mod_layer * MOD_ROWS + tok.mod_row(i, bm), 0, j)),
        ]
        args += [res, mod]
    cast_once = w.dtype != BF16 and nk == 1
    tiny = (2 * _SUBLANES, _LANES)
    return pl.pallas_call(
        functools.partial(_mm_kernel, nk=nk, gate_idx=gate_idx),
        grid=(n_cols // bn, m // bm, nk),
        in_specs=in_specs,
        out_specs=pl.BlockSpec((bm, bn), lambda j, i, k: (i, j)),
        out_shape=jax.ShapeDtypeStruct((m, n_cols), out_dtype),
        scratch_shapes=[pltpu.VMEM((bm, bn) if nk > 1 else tiny, F32),
                        pltpu.VMEM((bk, bn) if cast_once else tiny, BF16)],
        compiler_params=_cparams("arbitrary", "arbitrary", "arbitrary"),
        name=name,
    )(*args)


def _ffn_up_kernel(h_ref, wa_ref, wv_ref, cw_ref, cb_ref, g_ref, *, n_prompt_tiles, w_prompt, w_latent):
    i = pl.program_id(0)
    x = h_ref[...]
    act = _dot(x, wa_ref[0].astype(BF16))
    val = _dot(x, wv_ref[0].astype(BF16))
    t, tn = act.shape
    row = lax.broadcasted_iota(jnp.int32, (t, tn), 0)
    prev = pltpu.roll(act, 1, 0)
    nxt = pltpu.roll(act, t - 1, 0)

    def horizontal(width):
        col = row & (width - 1)
        return jnp.where(col > 0, prev, 0.0), jnp.where(col < width - 1, nxt, 0.0)

    def tap_row(kh, left, right):
        return (cw_ref[3 * kh:3 * kh + 1, :] * left + cw_ref[3 * kh + 1:3 * kh + 2, :] * act
                + cw_ref[3 * kh + 2:3 * kh + 3, :] * right)

    @pl.when(i < n_prompt_tiles)
    def _():
        left, right = horizontal(w_prompt)
        conv = tap_row(1, left, right) + cb_ref[...]
        g_ref[...] = (_silu(conv) * val).astype(g_ref.dtype)

    @pl.when(i >= n_prompt_tiles)
    def _():
        left, right = horizontal(w_latent)
        zeros = jnp.zeros((w_latent, tn), F32)
        above = jnp.concatenate([zeros, tap_row(0, left, right)[:t - w_latent, :]], axis=0)
        below = jnp.concatenate([tap_row(2, left, right)[w_latent:, :], zeros], axis=0)
        conv = above + tap_row(1, left, right) + below + cb_ref[...]
        g_ref[...] = (_silu(conv) * val).astype(g_ref.dtype)


def _ffn_up(h, w_up, conv_w, conv_b, layer, tok):
    rows, d = h.shape
    f = w_up.shape[2] // 2
    tm = tok.ls
    tn = 256
    assert tok.np_rows % tm == 0 and tm % tok.lp == 0
    nj = f // tn
    return pl.pallas_call(
        functools.partial(_ffn_up_kernel, n_prompt_tiles=tok.np_rows // tm,
                          w_prompt=tok.lp, w_latent=GRID_W),
        grid=(rows // tm, nj),
        in_specs=[
            pl.BlockSpec((tm, d), lambda i, j: (i, 0)),
            pl.BlockSpec((1, d, tn), lambda i, j: (layer, 0, j)),
            pl.BlockSpec((1, d, tn), lambda i, j: (layer, 0, j + nj)),
            pl.BlockSpec((9, tn), lambda i, j: (0, j)),
            pl.BlockSpec((1, tn), lambda i, j: (0, j)),
        ],
        out_specs=pl.BlockSpec((tm, tn), lambda i, j: (i, j)),
        out_shape=jax.ShapeDtypeStruct((rows, f), BF16),
        compiler_params=_cparams("arbitrary", "arbitrary"),
        name="ffn_up_conv",
    )(h, w_up, w_up, conv_w[layer].reshape(9, f), conv_b[layer].reshape(1, f))


def _gla_kernel(*refs, seq_len, zero_init, emit_state, dk, dv):
    it = iter(refs)
    q_ref, k_ref, v_ref, r_ref, low_ref, w2_ref, bg_ref, ng_ref = (next(it) for _ in range(8))
    s0_ref = None if zero_init else next(it)
    y_ref = next(it)
    sout_ref = next(it) if emit_state else None
    g_scr, o_scr, s_scr = next(it), next(it), next(it)

    nc = seq_len // CHUNK
    blk = 256
    cpb = blk // CHUNK
    nb = seq_len // blk
    masks, tris = _block_chunk_masks(blk)
    mid_row = (CHUNK // 2, CHUNK // 2 - 1)
    end_row = (CHUNK - 1, 0)
    q_scale = dk ** -0.5

    low = low_ref[...].astype(BF16)
    for e in range(2):
        z = _dot(low, w2_ref[e]) + bg_ref[e]
        g_scr[e] = _log_sigmoid(z) * (1.0 / GLA_GATE_NORM)
        if zero_init:
            s_scr[e] = jnp.zeros((dv, dk), F32)
        else:
            s_scr[e] = s0_ref[0, e, 0].T

    def block_step(e, b):
        r0 = pl.multiple_of(b * blk, blk)
        gc = _exact_dot_left01(tris[e], g_scr[e, pl.ds(r0, blk), :])
        g_scr[e, pl.ds(r0, blk), :] = gc
        g_mid = jnp.concatenate(
            [jnp.broadcast_to(gc[c * CHUNK + mid_row[e]:c * CHUNK + mid_row[e] + 1, :], (CHUNK, dk))
             for c in range(cpb)], axis=0)
        q = q_ref[pl.ds(r0, blk), :] * q_scale
        k = k_ref[pl.ds(r0, blk), :]
        scores = _dot_nt((q * jnp.exp(gc - g_mid)).astype(BF16), (k * jnp.exp(g_mid - gc)).astype(BF16))
        o_scr[e, pl.ds(r0, blk), :] = _dot(jnp.where(masks[e], scores, 0.0).astype(BF16),
                                            v_ref[pl.ds(r0, blk), :])

    def block_body(b, carry):
        block_step(0, b)
        block_step(1, b)
        return carry

    lax.fori_loop(0, nb, block_body, 0)

    def chunk_step(e, c):
        r0 = pl.multiple_of(c * CHUNK, CHUNK)
        gc = g_scr[e, pl.ds(r0, CHUNK), :]
        g_end = gc[end_row[e]:end_row[e] + 1, :]
        q = q_ref[pl.ds(r0, CHUNK), :] * q_scale
        k = k_ref[pl.ds(r0, CHUNK), :]
        v = v_ref[pl.ds(r0, CHUNK), :]
        s_t = s_scr[e]
        o_scr[e, pl.ds(r0, CHUNK), :] += _dot_nt((q * jnp.exp(gc)).astype(BF16), s_t.astype(BF16))
        u_t = _dot_tn(v, (k * jnp.exp(g_end - gc)).astype(BF16))
        s_scr[e] = s_t * jnp.exp(g_end) + u_t

    def body(i, carry):
        chunk_step(0, i)
        chunk_step(1, nc - 1 - i)
        return carry

    lax.fori_loop(0, nc, body, 0, unroll=2)

    for r0 in range(0, seq_len, blk):
        o = o_scr[0, r0:r0 + blk, :] + o_scr[1, r0:r0 + blk, :]
        ms = jnp.mean(o * o, axis=-1, keepdims=True)
        y = o * lax.rsqrt(ms + NORM_EPS) * ng_ref[...]
        y_ref[r0:r0 + blk, :] = (y * _silu(r_ref[r0:r0 + blk, :])).astype(y_ref.dtype)

    if emit_state:
        for e in range(2):
            sout_ref[0, e, 0] = s_scr[e].T


def _gla_scan(qk, v, r, low, w2, bg, ng, s0, tok, *, latent):
    dk = qk.shape[1] // (2 * GLA_HEADS)
    dv = v.shape[1] // GLA_HEADS
    if latent:
        nseq, seq_len, rb0 = tok.bs, tok.ls, tok.np_rows // tok.ls
    else:
        nseq, seq_len, rb0 = tok.bp, tok.lp, 0
    in_specs = [
        pl.BlockSpec((seq_len, dk), lambda b, h: (rb0 + b, h)),
        pl.BlockSpec((seq_len, dk), lambda b, h: (rb0 + b, GLA_HEADS + h)),
        pl.BlockSpec((seq_len, dv), lambda b, h: (rb0 + b, h)),
        pl.BlockSpec((seq_len, dv), lambda b, h: (rb0 + b, h)),
        pl.BlockSpec((seq_len, _LANES), lambda b, h: (rb0 + b, 0)),
        pl.BlockSpec((2, _LANES, dk), lambda b, h: (0, 0, h)),
        pl.BlockSpec((2, 1, dk), lambda b, h: (0, 0, h)),
        pl.BlockSpec((1, dv), lambda b, h: (0, 0)),
    ]
    args = [qk, qk, v, r, low, w2, bg, ng]
    if latent:
        in_specs.append(pl.BlockSpec((1, 2, 1, dk, dv), lambda b, h: (b, 0, h, 0, 0)))
        args.append(s0)
    y_spec = pl.BlockSpec((seq_len, dv), lambda b, h: (b, h))
    y_shape = jax.ShapeDtypeStruct((nseq * seq_len, v.shape[1]), BF16)
    if latent:
        out_specs, out_shape = y_spec, y_shape
    else:
        out_specs = (y_spec, pl.BlockSpec((1, 2, 1, dk, dv), lambda b, h: (b, 0, h, 0, 0)))
        out_shape = (y_shape, jax.ShapeDtypeStruct((nseq, 2, GLA_HEADS, dk, dv), F32))
    return pl.pallas_call(
        functools.partial(_gla_kernel, seq_len=seq_len, zero_init=not latent, emit_state=not latent,
                          dk=dk, dv=dv),
        grid=(nseq, GLA_HEADS),
        in_specs=in_specs,
        out_specs=out_specs,
        out_shape=out_shape,
        scratch_shapes=[
            pltpu.VMEM((2, seq_len, dk), F32),
            pltpu.VMEM((2, seq_len, dv), F32),
            pltpu.VMEM((2, dv, dk), F32),
        ],
        compiler_params=_cparams("arbitrary", "arbitrary"),
        name="gla_scan_latent" if latent else "gla_scan_prompt",
    )(*args)


def _gla_mixer(h, s_lat0, w_in, layer, w, tok):
    w_g1cat, w2, bg, ng = w["w_g1cat"], w["w2"], w["bg"], w["ng"]
    qkw = 2 * w2.shape[2]
    vw = (w_in.shape[2] - qkw) // 2
    qk = _matmul(h, w_in, layer, col_start=0, n_cols=qkw, bm=1024, bn=1024, name="gla_in_qk")
    v = _matmul(h, w_in, layer, col_start=qkw, n_cols=vw, bm=1024, bn=1024, out_dtype=BF16,
                name="gla_in_v")
    r = _matmul(h, w_in, layer, col_start=qkw + vw, n_cols=vw, bm=1024, bn=1024, name="gla_in_r")
    low = _matmul(h, w_g1cat[None], 0, bm=1024, bn=_LANES, name="gla_in_gate")
    y_p, s_new = _gla_scan(qk, v, r, low, w2, bg, ng, None, tok, latent=False)
    y_s = _gla_scan(qk, v, r, low, w2, bg, ng, s_lat0, tok, latent=True)
    return y_p, y_s, s_new


def _ssd_kernel(*refs, seq_len, zero_init, emit_state):
    it = iter(refs)
    (z_ref, x_ref, b_ref, c_ref, dt_ref, cwx_ref, cbx_ref, cwb_ref, cbb_ref, cwc_ref, cbc_ref,
     dtb_ref, alog_ref, dskip_ref, ng_ref, e_ref) = (next(it) for _ in range(16))
    s0_ref = None if zero_init else next(it)
    y_ref = next(it)
    sout_ref = next(it) if emit_state else None
    xd_scr, cumx_scr, cumt_scr, bm_scr, cm_scr, y_scr, s_scr = (next(it) for _ in range(7))

    hpg = y_scr.shape[1] // SSD_HEADDIM
    nc = seq_len // CHUNK
    blk = 256
    cpb = blk // CHUNK
    masks, _ = _chunk_masks()
    _, blk_tris = _block_chunk_masks(blk)
    end_row = (CHUNK - 1, 0)
    a_row = -jnp.exp(alog_ref[0])

    for r0 in range(0, seq_len, blk):
        xs = _silu(_conv4_block(x_ref, cwx_ref, cbx_ref, r0, blk, seq_len))
        y_scr[r0:r0 + blk, :] = dskip_ref[...] * xs
        bm_scr[r0:r0 + blk, :] = _silu(_conv4_block(b_ref, cwb_ref, cbb_ref, r0, blk, seq_len)).astype(BF16)
        cm_scr[r0:r0 + blk, :] = _silu(_conv4_block(c_ref, cwc_ref, cbc_ref, r0, blk, seq_len)).astype(BF16)
        dt = _softplus(dt_ref[r0:r0 + blk, :] + dtb_ref[0])
        for e in range(2):
            expand = e_ref[e]
            cum = _exact_dot_left01(blk_tris[e], dt * a_row)
            cum_t = cum.T
            for cc in range(cpb):
                cumt_scr[e, r0 // CHUNK + cc] = cum_t[0:2 * hpg, cc * CHUNK:(cc + 1) * CHUNK]
            cumx_scr[e, r0:r0 + blk, :] = _exact_select_copies(cum, expand, 2 * hpg)
            xd_scr[e, r0:r0 + blk, :] = xs * _exact_select_copies(dt, expand, 2 * hpg)

    for e in range(2):
        if zero_init:
            s_scr[e] = jnp.zeros(s_scr.shape[1:], F32)
        else:
            s_scr[e] = s0_ref[0, e, 0].T

    def chunk_step(e, c):
        r0 = pl.multiple_of(c * CHUNK, CHUNK)
        cum_x = cumx_scr[e, pl.ds(r0, CHUNK), :]
        cum_t = cumt_scr[e, c]
        xd = xd_scr[e, pl.ds(r0, CHUNK), :]
        xd_bf = xd.astype(BF16)
        bm = bm_scr[pl.ds(r0, CHUNK), :]
        cm = cm_scr[pl.ds(r0, CHUNK), :]
        cb = _dot_nt(cm, bm)
        parts = []
        for hh in range(hpg):
            lane = slice(hh * SSD_HEADDIM, (hh + 1) * SSD_HEADDIM)
            seg = cum_x[:, lane] - cum_t[e * hpg + hh:e * hpg + hh + 1, :]
            decay = jnp.where(masks[e], jnp.exp(jnp.where(masks[e], seg, 0.0)), 0.0)
            parts.append(_dot((cb * decay).astype(BF16), xd_bf[:, lane]))
        y = jnp.concatenate(parts, axis=1)
        s_t = s_scr[e]
        y = y + _dot(cm, s_t.astype(BF16)) * jnp.exp(cum_x)
        cum_end = cum_x[end_row[e]:end_row[e] + 1, :]
        st_t = _dot_tn(bm, (xd * jnp.exp(cum_end - cum_x)).astype(BF16))
        s_scr[e] = s_t * jnp.exp(cum_end) + st_t
        y_scr[pl.ds(r0, CHUNK), :] += y

    def body(i, carry):
        chunk_step(0, i)
        chunk_step(1, nc - 1 - i)
        return carry

    lax.fori_loop(0, nc, body, 0, unroll=2)

    for r0 in range(0, seq_len, blk):
        y = y_scr[r0:r0 + blk, :] * _silu(z_ref[r0:r0 + blk, :])
        ms = jnp.mean(y * y, axis=-1, keepdims=True)
        y_ref[r0:r0 + blk, :] = (y * lax.rsqrt(ms + NORM_EPS) * ng_ref[...]).astype(y_ref.dtype)

    if emit_state:
        for e in range(2):
            sout_ref[0, e, 0] = s_scr[e].T


def _ssd_scan(proj, dtp, w, s0, tok, *, latent):
    di = w["di"]
    gw = di // SSD_GROUPS
    if latent:
        nseq, seq_len, rb0 = tok.bs, tok.ls, tok.np_rows // tok.ls
    else:
        nseq, seq_len, rb0 = tok.bp, tok.lp, 0
    nx = di // gw
    nb = (2 * di) // SSD_STATE
    ncb = nb + SSD_GROUPS

    def rows(width, col, single=False):
        mode = {"pipeline_mode": pl.Buffered(1)} if single and latent else {}
        return pl.BlockSpec((seq_len, width), lambda b, g: (rb0 + b, col(g)), **mode)

    def vec(width, col, nrows=1):
        return pl.BlockSpec((nrows, width), lambda b, g: (0, col(g)))

    in_specs = [
        rows(gw, lambda g: g, single=True),
        rows(gw, lambda g: nx + g, single=True),
        rows(SSD_STATE, lambda g: nb + g),
        rows(SSD_STATE, lambda g: ncb + g),
        pl.BlockSpec((seq_len, _LANES), lambda b, g: (rb0 + b, g)),
        vec(gw, lambda g: g, 4), vec(gw, lambda g: g),
        vec(SSD_STATE, lambda g: di // SSD_STATE + g, 4), vec(SSD_STATE, lambda g: di // SSD_STATE + g),
        vec(SSD_STATE, lambda g: di // SSD_STATE + SSD_GROUPS + g, 4),
        vec(SSD_STATE, lambda g: di // SSD_STATE + SSD_GROUPS + g),
        pl.BlockSpec((1, 1, _LANES), lambda b, g: (g, 0, 0)),
        pl.BlockSpec((1, 1, _LANES), lambda b, g: (g, 0, 0)),
        vec(gw, lambda g: g),
        vec(gw, lambda g: g),
        pl.BlockSpec((2, _LANES, gw), lambda b, g: (0, 0, 0)),
    ]
    args = [proj, proj, proj, proj, dtp, w["conv_w"], w["conv_b"], w["conv_w"], w["conv_b"],
            w["conv_w"], w["conv_b"], w["dt_bias"], w["a_log"], w["d_skip"], w["ng"], w["expand"]]
    st_spec = pl.BlockSpec((1, 2, 1, gw, SSD_STATE), lambda b, g: (b, 0, g, 0, 0))
    if latent:
        in_specs.append(st_spec)
        args.append(s0)
    y_spec = pl.BlockSpec((seq_len, gw), lambda b, g: (b, g))
    y_shape = jax.ShapeDtypeStruct((nseq * seq_len, di), BF16)
    if latent:
        out_specs, out_shape = y_spec, y_shape
    else:
        out_specs = (y_spec, st_spec)
        out_shape = (y_shape, jax.ShapeDtypeStruct((nseq, 2, SSD_GROUPS, gw, SSD_STATE), F32))

    return pl.pallas_call(
        functools.partial(_ssd_kernel, seq_len=seq_len, zero_init=not latent, emit_state=not latent),
        grid=(nseq, SSD_GROUPS),
        in_specs=in_specs,
        out_specs=out_specs,
        out_shape=out_shape,
        scratch_shapes=[
            pltpu.VMEM((2, seq_len, gw), F32),
            pltpu.VMEM((2, seq_len, gw), F32),
            pltpu.VMEM((2, seq_len // CHUNK, 2 * gw // SSD_HEADDIM, CHUNK), F32),
            pltpu.VMEM((seq_len, SSD_STATE), BF16),
            pltpu.VMEM((seq_len, SSD_STATE), BF16),
            pltpu.VMEM((seq_len, gw), F32),
            pltpu.VMEM((2, SSD_STATE, gw), F32),
        ],
        compiler_params=_cparams("arbitrary", "arbitrary"),
        name="ssd_scan_latent" if latent else "ssd_scan_prompt",
    )(*args)


def _lru_kernel(*refs, seq_len, zero_init, emit_state):
    it = iter(refs)
    (x_ref, gate_ref, cw_ref, cb_ref, wa_ref, ba_ref, wi_ref, bi_ref, lam_ref) = (next(it) for _ in range(9))
    s0_ref = None if zero_init else next(it)
    y_ref = next(it)
    sout_ref = next(it) if emit_state else None
    a_scr, u_scr, h_scr = next(it), next(it), next(it)
    width = a_scr.shape[2]

    blk = 256
    for r0 in range(0, seq_len, blk):
        xc = _conv4_block(x_ref, cw_ref, cb_ref, r0, blk, seq_len)
        xc_bf = xc.astype(BF16)
        for e in range(2):
            rg = jax.nn.sigmoid(_dot(xc_bf, wa_ref[e, 0]) + ba_ref[e])
            ig = jax.nn.sigmoid(_dot(xc_bf, wi_ref[e, 0]) + bi_ref[e])
            log_a = -LRU_C * rg * _softplus(-lam_ref[e])
            a_scr[e, r0:r0 + blk, :] = jnp.exp(log_a)
            u_scr[e, r0:r0 + blk, :] = jnp.sqrt(_neg_expm1(2.0 * log_a)) * ig * xc

    nt = seq_len // _SUBLANES
    row = lax.broadcasted_iota(jnp.int32, (_SUBLANES, width), 0)

    def tile_scan(e, t, carry):
        r0 = pl.multiple_of(t * _SUBLANES, _SUBLANES)
        a = a_scr[e, pl.ds(r0, _SUBLANES), :]
        u = u_scr[e, pl.ds(r0, _SUBLANES), :]
        for s in (1, 2, 4):
            if e == 0:
                valid = row >= s
                a_sh, u_sh = pltpu.roll(a, s, 0), pltpu.roll(u, s, 0)
            else:
                valid = row < _SUBLANES - s
                a_sh, u_sh = pltpu.roll(a, _SUBLANES - s, 0), pltpu.roll(u, _SUBLANES - s, 0)
            u = u + a * jnp.where(valid, u_sh, 0.0)
            a = a * jnp.where(valid, a_sh, 1.0)
        hcur = u + a * carry
        last = _SUBLANES - 1 if e == 0 else 0
        return hcur, hcur[last:last + 1, :]

    def body(i, carry):
        cf, cbk = carry
        hf, cf = tile_scan(0, i, cf)
        r0 = pl.multiple_of(i * _SUBLANES, _SUBLANES)
        h_scr[0, pl.ds(r0, _SUBLANES), :] = hf
        tb = nt - 1 - i
        hb, cbk = tile_scan(1, tb, cbk)
        rb = pl.multiple_of(tb * _SUBLANES, _SUBLANES)
        h_scr[1, pl.ds(rb, _SUBLANES), :] = hb
        return cf, cbk

    if zero_init:
        init = (jnp.zeros((1, width), F32), jnp.zeros((1, width), F32))
    else:
        init = (s0_ref[0, 0], s0_ref[0, 1])
    cf, cbk = lax.fori_loop(0, nt, body, init, unroll=4)

    for r0 in range(0, seq_len, blk):
        hsum = h_scr[0, r0:r0 + blk, :] + h_scr[1, r0:r0 + blk, :]
        y_ref[r0:r0 + blk, :] = (hsum * _gelu_tanh(gate_ref[r0:r0 + blk, :])).astype(y_ref.dtype)

    if emit_state:
        sout_ref[0, 0] = cf
        sout_ref[0, 1] = cbk


def _lru_scan(proj, w, s0, tok, *, latent):
    width = proj.shape[1] // 2
    bw = width // LRU_BLOCKS
    if latent:
        nseq, seq_len, rb0 = tok.bs, tok.ls, tok.np_rows // tok.ls
    else:
        nseq, seq_len, rb0 = tok.bp, tok.lp, 0
    pair = pl.BlockSpec((2, 1, bw), lambda b, n: (0, 0, n))
    in_specs = [
        pl.BlockSpec((seq_len, bw), lambda b, n: (rb0 + b, n)),
        pl.BlockSpec((seq_len, bw), lambda b, n: (rb0 + b, LRU_BLOCKS + n)),
        pl.BlockSpec((4, bw), lambda b, n: (0, n)),
        pl.BlockSpec((1, bw), lambda b, n: (0, n)),
        pl.BlockSpec((2, 1, bw, bw), lambda b, n: (0, n, 0, 0)), pair,
        pl.BlockSpec((2, 1, bw, bw), lambda b, n: (0, n, 0, 0)), pair,
        pair,
    ]
    args = [proj, proj, w["conv_w"], w["conv_b"], w["w_a"], w["b_a"], w["w_i"], w["b_i"], w["lam"]]
    st_spec = pl.BlockSpec((1, 2, 1, bw), lambda b, n: (b, 0, 0, n))
    if latent:
        in_specs.append(st_spec)
        args.append(s0)
    y_spec = pl.BlockSpec((seq_len, bw), lambda b, n: (b, n))
    y_shape = jax.ShapeDtypeStruct((nseq * seq_len, width), BF16)
    if latent:
        out_specs, out_shape = y_spec, y_shape
    else:
        out_specs = (y_spec, st_spec)
        out_shape = (y_shape, jax.ShapeDtypeStruct((nseq, 2, 1, width), F32))
    return pl.pallas_call(
        functools.partial(_lru_kernel, seq_len=seq_len, zero_init=not latent, emit_state=not latent),
        grid=(nseq, LRU_BLOCKS),
        in_specs=in_specs,
        out_specs=out_specs,
        out_shape=out_shape,
        scratch_shapes=[
            pltpu.VMEM((2, seq_len, bw), F32),
            pltpu.VMEM((2, seq_len, bw), F32),
            pltpu.VMEM((2, seq_len, bw), F32),
        ],
        compiler_params=_cparams("arbitrary", "arbitrary"),
        name="lru_scan_latent" if latent else "lru_scan_prompt",
    )(*args)


def _merge_rows(y_p, y_s):
    return jnp.concatenate([y_p, y_s], axis=0)


def _prep_gla(w_g1, w_g2, b_g, norm_g):
    qkw = w_g2.shape[2]
    g1 = jnp.concatenate([w_g1[0], w_g1[1]], axis=1)
    g1 = jnp.pad(g1, ((0, 0), (0, _LANES - 2 * GLA_RANK)))
    w2 = jnp.zeros((2, _LANES, qkw), F32)
    w2 = w2.at[0, 0:GLA_RANK].set(w_g2[0]).at[1, GLA_RANK:2 * GLA_RANK].set(w_g2[1])
    return {
        "w_g1cat": g1.astype(BF16), "w2": w2.astype(BF16),
        "bg": b_g.reshape(2, 1, qkw), "ng": norm_g.reshape(1, -1),
    }


def _prep_ssd(w_in_dt, conv_w, conv_b, a_log, dt_bias, d_skip, norm_g):
    heads = a_log.shape[1]
    di = heads * SSD_HEADDIM
    hpg = heads // SSD_GROUPS
    gw = di // SSD_GROUPS
    main = 2 * di + 2 * SSD_GROUPS * SSD_STATE

    ncopy = 3
    pad = _LANES - ncopy * 2 * hpg

    def by_group(t):
        t = t.reshape(2, SSD_GROUPS, hpg).transpose(1, 0, 2).reshape(SSD_GROUPS, 1, 2 * hpg)
        return jnp.pad(jnp.tile(t, (1, 1, ncopy)), ((0, 0), (0, 0), (0, pad)))

    w_dt = w_in_dt.reshape(-1, 2, SSD_GROUPS, hpg).transpose(0, 2, 1, 3)
    w_dt = jnp.tile(w_dt.reshape(-1, SSD_GROUPS, 2 * hpg), (1, 1, ncopy))
    w_dt = jnp.pad(w_dt, ((0, 0), (0, 0), (0, pad))).reshape(-1, SSD_GROUPS * _LANES)
    lane = jnp.arange(_LANES)[:, None]
    chan = jnp.arange(gw)[None, :] // SSD_HEADDIM
    expand = jnp.stack([(lane % (2 * hpg) == e * hpg + chan) & (lane < ncopy * 2 * hpg)
                        for e in range(2)]).astype(BF16)
    return {
        "di": di, "w_dt": w_dt.astype(BF16), "n_main": main,
        "conv_w": conv_w, "conv_b": conv_b.reshape(1, -1),
        "dt_bias": by_group(dt_bias), "a_log": by_group(a_log),
        "d_skip": jnp.repeat(d_skip, SSD_HEADDIM).reshape(1, di), "ng": norm_g.reshape(1, di),
        "expand": expand,
    }


def _prep_lru(conv_w, conv_b, w_a, b_a, w_i, b_i, lam):
    width = conv_w.shape[1]
    return {
        "conv_w": conv_w, "conv_b": conv_b.reshape(1, width),
        "w_a": w_a.astype(BF16), "b_a": b_a.reshape(2, 1, width),
        "w_i": w_i.astype(BF16), "b_i": b_i.reshape(2, 1, width), "lam": lam.reshape(2, 1, width),
    }


def kernel(x_prompt, x_sample, state_gla, state_ssd, state_lru, c, c_ctx, w_mod, b_mod, norm_mix_g,
           norm_ffn_g, ffn_w_up, ffn_conv_w, ffn_conv_b, ffn_w_down, final_norm_g, gla_w_in, gla_w_g1,
           gla_w_g2, gla_b_g, gla_norm_g, gla_w_out, ssd_w_in, ssd_conv_w, ssd_conv_b, ssd_a_log,
           ssd_dt_bias, ssd_d, ssd_norm_g, ssd_w_out, lru_w_in, lru_conv_w, lru_conv_b, lru_w_a, lru_b_a,
           lru_w_i, lru_b_i, lru_lambda, lru_w_out):
    bp, lp, d = x_prompt.shape
    bs, ls, _ = x_sample.shape
    depth = w_mod.shape[0]
    tok = _Tokens(bp, lp, bs, ls)
    assert bs + 1 <= MOD_ROWS and ls // GRID_W * GRID_W == ls

    x = jnp.concatenate([x_prompt.reshape(bp * lp, d), x_sample.reshape(bs * ls, d)], axis=0)
    c_rows = jnp.concatenate([c_ctx[None], c, jnp.zeros((MOD_ROWS - 1 - bs, d), F32)], axis=0)
    mod = _mod_table(c_rows, w_mod, b_mod).reshape(depth * MOD_ROWS, MOD_CHUNKS, d)

    ffn_w_down_bf = ffn_w_down.astype(BF16)
    ssd_w_out_bf = ssd_w_out.astype(BF16)
    gla_new, ssd_new, lru_new = [], [], []
    for l in range(depth):
        kind, j = l % 3, l // 3
        h = _norm_mod(x, norm_mix_g[l], mod, l, tok, shift_idx=0, scale_idx=1)
        if kind == 0:
            w = _prep_gla(gla_w_g1[j], gla_w_g2[j], gla_b_g[j], gla_norm_g[j])
            y_p, y_s, s_new = _gla_mixer(h, state_gla[:, j], gla_w_in, j, w, tok)
            gla_new.append(s_new)
            w_out = gla_w_out
        elif kind == 1:
            n_main = ssd_conv_w.shape[2] + ssd_d.shape[1] * SSD_HEADDIM
            w = _prep_ssd(ssd_w_in[j, :, n_main:], ssd_conv_w[j], ssd_conv_b[j], ssd_a_log[j],
                          ssd_dt_bias[j], ssd_d[j], ssd_norm_g[j])
            proj = _matmul(h, ssd_w_in, j, col_start=0, n_cols=n_main, bm=1024, bn=1024,
                           name="ssd_in_main")
            dtp = _matmul(h, w["w_dt"][None], 0, bm=1024, bn=SSD_GROUPS * _LANES, name="ssd_in_dt")
            gw = w["di"] // SSD_GROUPS
            s_lat0 = state_ssd[:, j].reshape(bs, 2, SSD_GROUPS, gw, SSD_STATE)
            y_p, s_new = _ssd_scan(proj, dtp, w, None, tok, latent=False)
            y_s = _ssd_scan(proj, dtp, w, s_lat0, tok, latent=True)
            ssd_new.append(s_new.reshape(bp, 2, -1, SSD_HEADDIM, SSD_STATE))
            w_out = ssd_w_out_bf
        else:
            w = _prep_lru(lru_conv_w[j], lru_conv_b[j], lru_w_a[j], lru_b_a[j],
                          lru_w_i[j], lru_b_i[j], lru_lambda[j])
            proj = _matmul(h, lru_w_in, j, bm=1024, bn=1024, name="lru_in")
            s_lat0 = state_lru[:, j].reshape(bs, 2, 1, -1)
            y_p, s_new = _lru_scan(proj, w, None, tok, latent=False)
            y_s = _lru_scan(proj, w, s_lat0, tok, latent=True)
            lru_new.append(s_new.reshape(bp, 2, -1))
            w_out = lru_w_out
        y = _merge_rows(y_p, y_s)
        x = _matmul(y, w_out, j, bm=1024, bn=1024, bk=min(w_out.shape[1], 2048),
                    residual=(x, mod, l, tok, 2), name="mixer_out")
        h = _norm_mod(x, norm_ffn_g[l], mod, l, tok, shift_idx=3, scale_idx=4)
        g = _ffn_up(h, ffn_w_up, ffn_conv_w, ffn_conv_b, l, tok)
        x = _matmul(g, ffn_w_down_bf, l, bm=1024, bn=1024, bk=ffn_w_down.shape[1] // 2,
                    residual=(x, mod, l, tok, 5), name="ffn_down")
    y = _final_norm(x, final_norm_g)
    y_prompt = y[:tok.np_rows].reshape(bp, lp, d)
    y_sample = y[tok.np_rows:].reshape(bs, ls, d)
    return (y_prompt, y_sample, jnp.stack(gla_new, axis=1), jnp.stack(ssd_new, axis=1),
            jnp.stack(lru_new, axis=1))
```

```python
import functools

import jax
import jax.numpy as jnp
from jax import lax
from jax.experimental import pallas as pl
from jax.experimental.pallas import tpu as pltpu

F32 = jnp.float32
BF16 = jnp.bfloat16

_V7X_VMEM_BYTES = 64 * 1024 * 1024
_VMEM_LIMIT = _V7X_VMEM_BYTES - 8 * 1024 * 1024
_SUBLANES = 8
_LANES = 128

NORM_EPS = 1e-6
MOD_CHUNKS = 6
MOD_ROWS = 8
GRID_W = 64

GLA_HEADS = 4
GLA_RANK = 16
GLA_GATE_NORM = 16.0
SSD_HEADDIM = 64
SSD_GROUPS = 8
SSD_STATE = 128
LRU_BLOCKS = 8
LRU_C = 8.0
CHUNK = 64


def _cparams(*sem):
    return pltpu.CompilerParams(dimension_semantics=sem, vmem_limit_bytes=_VMEM_LIMIT)


def _silu(x):
    return x * jax.nn.sigmoid(x)


def _softplus(x):
    return jnp.maximum(x, 0.0) + jnp.log1p(jnp.exp(-jnp.abs(x)))


def _log_sigmoid(x):
    return jnp.minimum(x, 0.0) - jnp.log1p(jnp.exp(-jnp.abs(x)))


def _gelu_tanh(x):
    return 0.5 * x * (1.0 + jnp.tanh(0.7978845608028654 * (x + 0.044715 * (x * x * x))))


def _neg_expm1(x):
    t = jnp.tanh(0.5 * x)
    return -2.0 * t / (1.0 - t)


def _split3(x):
    x1 = x.astype(BF16)
    r1 = x - x1.astype(F32)
    x2 = r1.astype(BF16)
    x3 = (r1 - x2.astype(F32)).astype(BF16)
    return x1, x2, x3


def _dot(a, b):
    return jnp.dot(a, b, preferred_element_type=F32)


def _dot_nt(a, b):
    return lax.dot_general(a, b, (((1,), (1,)), ((), ())), preferred_element_type=F32)


def _dot_tn(a, b):
    return lax.dot_general(a, b, (((0,), (0,)), ((), ())), preferred_element_type=F32)


def _exact_dot_left01(m01, x):
    x1, x2, x3 = _split3(x)
    return _dot(m01, x1) + _dot(m01, x2) + _dot(m01, x3)


def _exact_select_copies(x, sel01, ncopy_lanes):
    t1 = x.astype(BF16).astype(F32)
    r1 = x - t1
    t2 = r1.astype(BF16).astype(F32)
    lane = lax.broadcasted_iota(jnp.int32, x.shape, 1)
    terms = jnp.where(lane < ncopy_lanes, t1, jnp.where(lane < 2 * ncopy_lanes, t2, r1 - t2))
    return _dot(terms.astype(BF16), sel01)


def _chunk_masks():
    ii = lax.broadcasted_iota(jnp.int32, (CHUNK, CHUNK), 0)
    jj = lax.broadcasted_iota(jnp.int32, (CHUNK, CHUNK), 1)
    masks = (jj <= ii, jj >= ii)
    tris = tuple(jnp.where(m, 1.0, 0.0).astype(BF16) for m in masks)
    return masks, tris


def _block_chunk_masks(rows):
    ii = lax.broadcasted_iota(jnp.int32, (rows, rows), 0)
    jj = lax.broadcasted_iota(jnp.int32, (rows, rows), 1)
    shift = CHUNK.bit_length() - 1
    same = jnp.where(jnp.right_shift(ii, shift) == jnp.right_shift(jj, shift), 1, 0)
    masks = (same * jnp.where(jj <= ii, 1, 0) > 0, same * jnp.where(jj >= ii, 1, 0) > 0)
    tris = tuple(jnp.where(m, 1.0, 0.0).astype(BF16) for m in masks)
    return masks, tris


def _shifted_rows(ref, r0, nrows, off, total):
    lo, hi = r0 + off, r0 + off + nrows
    clo, chi = max(lo, 0), min(hi, total)
    x = ref[clo:chi, :]
    parts = []
    if clo > lo:
        parts.append(jnp.zeros((clo - lo, x.shape[1]), x.dtype))
    parts.append(x)
    if hi > chi:
        parts.append(jnp.zeros((hi - chi, x.shape[1]), x.dtype))
    return x if len(parts) == 1 else jnp.concatenate(parts, axis=0)


def _conv4_block(ref, w_ref, b_ref, r0, nrows, total):
    acc = b_ref[...] + w_ref[0:1, :] * _shifted_rows(ref, r0, nrows, -1, total)
    acc = acc + w_ref[1:2, :] * ref[r0:r0 + nrows, :]
    acc = acc + w_ref[2:3, :] * _shifted_rows(ref, r0, nrows, 1, total)
    acc = acc + w_ref[3:4, :] * _shifted_rows(ref, r0, nrows, 2, total)
    return acc


def _mod_kernel(c_ref, w_ref, b_ref, o_ref):
    c = c_ref[...]
    s = _silu(c).astype(BF16)
    o_ref[0] = _dot(s, w_ref[0].astype(BF16)) + b_ref[0]


def _mod_table(c_rows, w_mod, b_mod):
    depth, d, n = w_mod.shape
    bn = 1024
    return pl.pallas_call(
        _mod_kernel,
        grid=(depth, n // bn),
        in_specs=[
            pl.BlockSpec((MOD_ROWS, d), lambda l, j: (0, 0)),
            pl.BlockSpec((1, d, bn), lambda l, j: (l, 0, j)),
            pl.BlockSpec((1, 1, bn), lambda l, j: (l, 0, j)),
        ],
        out_specs=pl.BlockSpec((1, MOD_ROWS, bn), lambda l, j: (l, 0, j)),
        out_shape=jax.ShapeDtypeStruct((depth, MOD_ROWS, n), F32),
        compiler_params=_cparams("arbitrary", "arbitrary"),
        name="adaln_table",
    )(c_rows, w_mod, b_mod.reshape(depth, 1, n))


class _Tokens:
    def __init__(self, n_prompt, l_prompt, n_latent, l_latent):
        self.bp, self.lp, self.bs, self.ls = n_prompt, l_prompt, n_latent, l_latent
        self.np_rows = n_prompt * l_prompt
        self.ns_rows = n_latent * l_latent
        self.rows = self.np_rows + self.ns_rows

    def mod_row(self, i, bm):
        r0 = i * bm
        return jnp.where(r0 < self.np_rows, 0, 1 + (r0 - self.np_rows) // self.ls)


def _norm_mod_kernel(x_ref, g_ref, mod_ref, h_ref, *, shift_idx, scale_idx):
    x = x_ref[...]
    ms = jnp.mean(x * x, axis=-1, keepdims=True)
    y = x * lax.rsqrt(ms + NORM_EPS) * g_ref[...]
    h = y * (1.0 + mod_ref[0, scale_idx:scale_idx + 1, :]) + mod_ref[0, shift_idx:shift_idx + 1, :]
    h_ref[...] = h.astype(h_ref.dtype)


def _norm_mod(x, g, mod, layer, tok, *, shift_idx, scale_idx):
    rows, d = x.shape
    bm = 512
    return pl.pallas_call(
        functools.partial(_norm_mod_kernel, shift_idx=shift_idx, scale_idx=scale_idx),
        grid=(rows // bm,),
        in_specs=[
            pl.BlockSpec((bm, d), lambda i: (i, 0)),
            pl.BlockSpec((1, d), lambda i: (0, 0)),
            pl.BlockSpec((1, MOD_CHUNKS, d), lambda i: (layer * MOD_ROWS + tok.mod_row(i, bm), 0, 0)),
        ],
        out_specs=pl.BlockSpec((bm, d), lambda i: (i, 0)),
        out_shape=jax.ShapeDtypeStruct((rows, d), BF16),
        compiler_params=_cparams("arbitrary"),
        name="norm_adaln",
    )(x, g.reshape(1, d), mod)


def _final_norm_kernel(x_ref, g_ref, op_ref, os_ref, *, n_prompt_tiles):
    x = x_ref[...]
    ms = jnp.mean(x * x, axis=-1, keepdims=True)
    y = x * lax.rsqrt(ms + NORM_EPS) * g_ref[...]

    @pl.when(pl.program_id(0) < n_prompt_tiles)
    def _():
        op_ref[...] = y

    @pl.when(pl.program_id(0) >= n_prompt_tiles)
    def _():
        os_ref[...] = y


def _final_norm(x, g, tok):
    rows, d = x.shape
    bm = 512
    npt = tok.np_rows // bm
    return pl.pallas_call(
        functools.partial(_final_norm_kernel, n_prompt_tiles=npt),
        grid=(rows // bm,),
        in_specs=[pl.BlockSpec((bm, d), lambda i: (i, 0)), pl.BlockSpec((1, d), lambda i: (0, 0))],
        out_specs=[pl.BlockSpec((bm, d), lambda i: (jnp.minimum(i, npt - 1), 0)),
                   pl.BlockSpec((bm, d), lambda i: (jnp.maximum(i - npt, 0), 0))],
        out_shape=[jax.ShapeDtypeStruct((tok.np_rows, d), F32),
                   jax.ShapeDtypeStruct((tok.ns_rows, d), F32)],
        compiler_params=_cparams("arbitrary"),
        name="final_norm",
    )(x, g.reshape(1, d))


def _mm_kernel(*refs, nk, gate_idx):
    if gate_idx is None:
        x_ref, w_ref, o_ref, acc_ref, wb_ref = refs
        res_ref = mod_ref = None
    else:
        x_ref, w_ref, res_ref, mod_ref, o_ref, acc_ref, wb_ref = refs
    i, k = pl.program_id(1), pl.program_id(2)
    if w_ref.dtype == BF16:
        w = w_ref[0]
    elif nk == 1:
        @pl.when(i == 0)
        def _():
            wb_ref[...] = w_ref[0].astype(BF16)

        w = wb_ref[...]
    else:
        w = w_ref[0].astype(BF16)
    part = _dot(x_ref[...], w)

    def finish(acc):
        if gate_idx is None:
            o_ref[...] = acc.astype(o_ref.dtype)
        else:
            o_ref[...] = res_ref[...] + mod_ref[0, gate_idx:gate_idx + 1, :] * acc

    if nk == 1:
        finish(part)
        return

    @pl.when(k == 0)
    def _():
        acc_ref[...] = part

    @pl.when(jnp.logical_and(k > 0, k < nk - 1))
    def _():
        acc_ref[...] += part

    @pl.when(k == nk - 1)
    def _():
        finish(acc_ref[...] + part)


def _matmul(x, w, layer, *, col_start=0, n_cols=None, bm, bn, bk=None, out_dtype=F32, residual=None,
            name="matmul"):
    m, kdim = x.shape
    n_cols = w.shape[2] - col_start if n_cols is None else n_cols
    bk = kdim if bk is None else bk
    nk = kdim // bk
    off = col_start // bn
    in_specs = [
        pl.BlockSpec((bm, bk), lambda j, i, k: (i, k)),
        pl.BlockSpec((1, bk, bn), lambda j, i, k: (layer, k, j + off)),
    ]
    args = [x, w]
    gate_idx = None
    if residual is not None:
        res, mod, mod_layer, tok, gate_idx = residual
        in_specs += [
            pl.BlockSpec((bm, bn), lambda j, i, k: (i, j)),
            pl.BlockSpec((1, MOD_CHUNKS, bn),
                         lambda j, i, k: (mod_layer * MOD_ROWS + tok.mod_row(i, bm), 0, j)),
        ]
        args += [res, mod]
    cast_once = w.dtype != BF16 and nk == 1
    tiny = (2 * _SUBLANES, _LANES)
    return pl.pallas_call(
        functools.partial(_mm_kernel, nk=nk, gate_idx=gate_idx),
        grid=(n_cols // bn, m // bm, nk),
        in_specs=in_specs,
        out_specs=pl.BlockSpec((bm, bn), lambda j, i, k: (i, j)),
        out_shape=jax.ShapeDtypeStruct((m, n_cols), out_dtype),
        scratch_shapes=[pltpu.VMEM((bm, bn) if nk > 1 else tiny, F32),
                        pltpu.VMEM((bk, bn) if cast_once else tiny, BF16)],
        compiler_params=_cparams("arbitrary", "arbitrary", "arbitrary"),
        name=name,
    )(*args)


def _ffn_up_kernel(h_ref, wa_ref, wv_ref, cw_ref, cb_ref, g_ref, *, n_prompt_tiles, w_prompt, w_latent):
    i = pl.program_id(0)
    x = h_ref[...]
    act = _dot(x, wa_ref[0].astype(BF16))
    val = _dot(x, wv_ref[0].astype(BF16))
    t, tn = act.shape
    row = lax.broadcasted_iota(jnp.int32, (t, tn), 0)
    prev = pltpu.roll(act, 1, 0)
    nxt = pltpu.roll(act, t - 1, 0)

    def horizontal(width):
        col = row & (width - 1)
        return jnp.where(col > 0, prev, 0.0), jnp.where(col < width - 1, nxt, 0.0)

    def tap_row(kh, left, right):
        return (cw_ref[3 * kh:3 * kh + 1, :] * left + cw_ref[3 * kh + 1:3 * kh + 2, :] * act
                + cw_ref[3 * kh + 2:3 * kh + 3, :] * right)

    @pl.when(i < n_prompt_tiles)
    def _():
        left, right = horizontal(w_prompt)
        conv = tap_row(1, left, right) + cb_ref[...]
        g_ref[...] = (_silu(conv) * val).astype(g_ref.dtype)

    @pl.when(i >= n_prompt_tiles)
    def _():
        left, right = horizontal(w_latent)
        zeros = jnp.zeros((w_latent, tn), F32)
        above = jnp.concatenate([zeros, tap_row(0, left, right)[:t - w_latent, :]], axis=0)
        below = jnp.concatenate([tap_row(2, left, right)[w_latent:, :], zeros], axis=0)
        conv = above + tap_row(1, left, right) + below + cb_ref[...]
        g_ref[...] = (_silu(conv) * val).astype(g_ref.dtype)


def _ffn_up(h, w_up, conv_w, conv_b, layer, tok):
    rows, d = h.shape
    f = w_up.shape[2] // 2
    tm = tok.ls
    tn = 256
    assert tok.np_rows % tm == 0 and tm % tok.lp == 0
    nj = f // tn
    return pl.pallas_call(
        functools.partial(_ffn_up_kernel, n_prompt_tiles=tok.np_rows // tm,
                          w_prompt=tok.lp, w_latent=GRID_W),
        grid=(rows // tm, nj),
        in_specs=[
            pl.BlockSpec((tm, d), lambda i, j: (i, 0)),
            pl.BlockSpec((1, d, tn), lambda i, j: (layer, 0, j)),
            pl.BlockSpec((1, d, tn), lambda i, j: (layer, 0, j + nj)),
            pl.BlockSpec((9, tn), lambda i, j: (0, j)),
            pl.BlockSpec((1, tn), lambda i, j: (0, j)),
        ],
        out_specs=pl.BlockSpec((tm, tn), lambda i, j: (i, j)),
        out_shape=jax.ShapeDtypeStruct((rows, f), BF16),
        compiler_params=_cparams("arbitrary", "arbitrary"),
        name="ffn_up_conv",
    )(h, w_up, w_up, conv_w[layer].reshape(9, f), conv_b[layer].reshape(1, f))


def _gla_kernel(*refs, seq_len, zero_init, emit_state, n_alias, dk, dv):
    it = iter(refs)
    q_ref, k_ref, v_ref, r_ref, low_ref, w2_ref, bg_ref, ng_ref = (next(it) for _ in range(8))
    s0_ref = None if zero_init else next(it)
    for _ in range(n_alias):
        next(it)
    y_ref = next(it)
    sout_ref = next(it).at[0, 0] if emit_state else None
    g_scr, o_scr, s_scr = next(it), next(it), next(it)

    nc = seq_len // CHUNK
    blk = 256
    cpb = blk // CHUNK
    nb = seq_len // blk
    masks, tris = _block_chunk_masks(blk)
    mid_row = (CHUNK // 2, CHUNK // 2 - 1)
    end_row = (CHUNK - 1, 0)
    q_scale = dk ** -0.5

    low = low_ref[...].astype(BF16)
    for e in range(2):
        z = _dot(low, w2_ref[e]) + bg_ref[e]
        g_scr[e] = _log_sigmoid(z) * (1.0 / GLA_GATE_NORM)
        if zero_init:
            s_scr[e] = jnp.zeros((dv, dk), F32)
        else:
            s_scr[e] = s0_ref[0, e, 0].T

    def block_step(e, b):
        r0 = pl.multiple_of(b * blk, blk)
        gc = _exact_dot_left01(tris[e], g_scr[e, pl.ds(r0, blk), :])
        g_scr[e, pl.ds(r0, blk), :] = gc
        g_mid = jnp.concatenate(
            [jnp.broadcast_to(gc[c * CHUNK + mid_row[e]:c * CHUNK + mid_row[e] + 1, :], (CHUNK, dk))
             for c in range(cpb)], axis=0)
        q = q_ref[pl.ds(r0, blk), :] * q_scale
        k = k_ref[pl.ds(r0, blk), :]
        scores = _dot_nt((q * jnp.exp(gc - g_mid)).astype(BF16), (k * jnp.exp(g_mid - gc)).astype(BF16))
        o_scr[e, pl.ds(r0, blk), :] = _dot(jnp.where(masks[e], scores, 0.0).astype(BF16),
                                            v_ref[pl.ds(r0, blk), :])

    def block_body(b, carry):
        block_step(0, b)
        block_step(1, b)
        return carry

    lax.fori_loop(0, nb, block_body, 0, unroll=min(2, nb))

    def chunk_step(e, c):
        r0 = pl.multiple_of(c * CHUNK, CHUNK)
        gc = g_scr[e, pl.ds(r0, CHUNK), :]
        g_end = gc[end_row[e]:end_row[e] + 1, :]
        q = q_ref[pl.ds(r0, CHUNK), :] * q_scale
        k = k_ref[pl.ds(r0, CHUNK), :]
        v = v_ref[pl.ds(r0, CHUNK), :]
        s_t = s_scr[e]
        o_scr[e, pl.ds(r0, CHUNK), :] += _dot_nt((q * jnp.exp(gc)).astype(BF16), s_t.astype(BF16))
        u_t = _dot_tn(v, (k * jnp.exp(g_end - gc)).astype(BF16))
        s_scr[e] = s_t * jnp.exp(g_end) + u_t

    def body(i, carry):
        chunk_step(0, i)
        chunk_step(1, nc - 1 - i)
        return carry

    lax.fori_loop(0, nc, body, 0, unroll=4)

    for r0 in range(0, seq_len, blk):
        o = o_scr[0, r0:r0 + blk, :] + o_scr[1, r0:r0 + blk, :]
        ms = jnp.mean(o * o, axis=-1, keepdims=True)
        y = o * lax.rsqrt(ms + NORM_EPS) * ng_ref[...]
        y_ref[r0:r0 + blk, :] = (y * _silu(r_ref[r0:r0 + blk, :])).astype(y_ref.dtype)

    if emit_state:
        for e in range(2):
            sout_ref[e, 0] = s_scr[e].T


def _in_place(buffers):
    return [pl.BlockSpec(memory_space=pl.ANY) for _ in buffers], list(buffers)


def _gla_scan(qk, v, r, low, w2, bg, ng, s0, tok, *, latent, y_buf=None, state_buf=None, layer=0,
              n_layers=1):
    dk = qk.shape[1] // (2 * GLA_HEADS)
    dv = v.shape[1] // GLA_HEADS
    if latent:
        nseq, seq_len, rb0 = tok.bs, tok.ls, tok.np_rows // tok.ls
    else:
        nseq, seq_len, rb0 = tok.bp, tok.lp, 0
    in_specs = [
        pl.BlockSpec((seq_len, dk), lambda b, h: (rb0 + b, h)),
        pl.BlockSpec((seq_len, dk), lambda b, h: (rb0 + b, GLA_HEADS + h)),
        pl.BlockSpec((seq_len, dv), lambda b, h: (rb0 + b, h)),
        pl.BlockSpec((seq_len, dv), lambda b, h: (rb0 + b, h)),
        pl.BlockSpec((seq_len, _LANES), lambda b, h: (rb0 + b, 0)),
        pl.BlockSpec((2, _LANES, dk), lambda b, h: (0, 0, h)),
        pl.BlockSpec((2, 1, dk), lambda b, h: (0, 0, h)),
        pl.BlockSpec((1, dv), lambda b, h: (0, 0)),
    ]
    args = [qk, qk, v, r, low, w2, bg, ng]
    if latent:
        in_specs.append(pl.BlockSpec((1, 2, 1, dk, dv), lambda b, h: (b, 0, h, 0, 0)))
        args.append(s0)
    y_spec = pl.BlockSpec((seq_len, dv), lambda b, h: (rb0 + b, h))
    y_shape = jax.ShapeDtypeStruct((tok.rows, v.shape[1]), BF16)
    if latent:
        out_specs, out_shape, bufs = [y_spec], [y_shape], [y_buf]
    else:
        out_specs = [y_spec, pl.BlockSpec((1, 1, 2, 1, dk, dv), lambda b, h: (b, layer, 0, h, 0, 0))]
        out_shape = [y_shape, jax.ShapeDtypeStruct((nseq, n_layers, 2, GLA_HEADS, dk, dv), F32)]
        bufs = [y_buf, state_buf]
    aliases = {}
    for o, buf in enumerate(bufs):
        if buf is not None:
            aliases[len(args)] = o
            spec, arg = _in_place([buf])
            in_specs += spec
            args += arg
    return pl.pallas_call(
        functools.partial(_gla_kernel, seq_len=seq_len, zero_init=not latent, emit_state=not latent,
                          n_alias=len(aliases), dk=dk, dv=dv),
        grid=(nseq, GLA_HEADS),
        in_specs=in_specs,
        out_specs=out_specs,
        out_shape=out_shape,
        input_output_aliases=aliases,
        scratch_shapes=[
            pltpu.VMEM((2, seq_len, dk), F32),
            pltpu.VMEM((2, seq_len, dv), F32),
            pltpu.VMEM((2, dv, dk), F32),
        ],
        compiler_params=_cparams("arbitrary", "arbitrary"),
        name="gla_scan_latent" if latent else "gla_scan_prompt",
    )(*args)


def _gla_mixer(h, s_lat0, w_in, layer, w, tok, state_buf):
    w_g1cat, w2, bg, ng = w["w_g1cat"], w["w2"], w["bg"], w["ng"]
    qkw = 2 * w2.shape[2]
    vw = (w_in.shape[2] - qkw) // 2
    qk = _matmul(h, w_in, layer, col_start=0, n_cols=qkw, bm=1024, bn=1024, name="gla_in_qk")
    v = _matmul(h, w_in, layer, col_start=qkw, n_cols=vw, bm=1024, bn=1024, out_dtype=BF16,
                name="gla_in_v")
    r = _matmul(h, w_in, layer, col_start=qkw + vw, n_cols=vw, bm=1024, bn=1024, name="gla_in_r")
    low = _matmul(h, w_g1cat[None], 0, bm=1024, bn=_LANES, name="gla_in_gate")
    y, states = _gla_scan(qk, v, r, low, w2, bg, ng, None, tok, latent=False, state_buf=state_buf,
                          layer=layer, n_layers=w_in.shape[0])
    (y,) = _gla_scan(qk, v, r, low, w2, bg, ng, s_lat0, tok, latent=True, y_buf=y)
    return y, states


def _ssd_kernel(*refs, seq_len, zero_init, emit_state, n_alias):
    it = iter(refs)
    (z_ref, x_ref, b_ref, c_ref, dt_ref, cwx_ref, cbx_ref, cwb_ref, cbb_ref, cwc_ref, cbc_ref,
     dtb_ref, alog_ref, dskip_ref, ng_ref, e_ref) = (next(it) for _ in range(16))
    s0_ref = None if zero_init else next(it)
    for _ in range(n_alias):
        next(it)
    y_ref = next(it)
    sout_ref = next(it) if emit_state else None
    xd_scr, cumx_scr, cumt_scr, bm_scr, cm_scr, y_scr, s_scr = (next(it) for _ in range(7))

    hpg = y_scr.shape[1] // SSD_HEADDIM
    nc = seq_len // CHUNK
    blk = 256
    cpb = blk // CHUNK
    masks, _ = _chunk_masks()
    _, blk_tris = _block_chunk_masks(blk)
    end_row = (CHUNK - 1, 0)
    a_row = -jnp.exp(alog_ref[0])

    for r0 in range(0, seq_len, blk):
        xs = _silu(_conv4_block(x_ref, cwx_ref, cbx_ref, r0, blk, seq_len))
        y_scr[r0:r0 + blk, :] = dskip_ref[...] * xs
        bm_scr[r0:r0 + blk, :] = _silu(_conv4_block(b_ref, cwb_ref, cbb_ref, r0, blk, seq_len)).astype(BF16)
        cm_scr[r0:r0 + blk, :] = _silu(_conv4_block(c_ref, cwc_ref, cbc_ref, r0, blk, seq_len)).astype(BF16)
        dt = _softplus(dt_ref[r0:r0 + blk, :] + dtb_ref[0])
        for e in range(2):
            expand = e_ref[e]
            cum = _exact_dot_left01(blk_tris[e], dt * a_row)
            cum_t = cum.T
            for cc in range(cpb):
                cumt_scr[e, r0 // CHUNK + cc] = cum_t[0:2 * hpg, cc * CHUNK:(cc + 1) * CHUNK]
            cumx_scr[e, r0:r0 + blk, :] = _exact_select_copies(cum, expand, 2 * hpg)
            xd_scr[e, r0:r0 + blk, :] = xs * _exact_select_copies(dt, expand, 2 * hpg)

    for e in range(2):
        if zero_init:
            s_scr[e] = jnp.zeros(s_scr.shape[1:], F32)
        else:
            s_scr[e] = s0_ref[0, e, 0].T

    def chunk_step(e, c):
        r0 = pl.multiple_of(c * CHUNK, CHUNK)
        cum_x = cumx_scr[e, pl.ds(r0, CHUNK), :]
        cum_t = cumt_scr[e, c]
        xd = xd_scr[e, pl.ds(r0, CHUNK), :]
        xd_bf = xd.astype(BF16)
        bm = bm_scr[pl.ds(r0, CHUNK), :]
        cm = cm_scr[pl.ds(r0, CHUNK), :]
        cb = _dot_nt(cm, bm)
        parts = []
        for hh in range(hpg):
            lane = slice(hh * SSD_HEADDIM, (hh + 1) * SSD_HEADDIM)
            seg = cum_x[:, lane] - cum_t[e * hpg + hh:e * hpg + hh + 1, :]
            decay = jnp.where(masks[e], jnp.exp(jnp.where(masks[e], seg, 0.0)), 0.0)
            parts.append(_dot((cb * decay).astype(BF16), xd_bf[:, lane]))
        y = jnp.concatenate(parts, axis=1)
        s_t = s_scr[e]
        y = y + _dot(cm, s_t.astype(BF16)) * jnp.exp(cum_x)
        cum_end = cum_x[end_row[e]:end_row[e] + 1, :]
        st_t = _dot_tn(bm, (xd * jnp.exp(cum_end - cum_x)).astype(BF16))
        s_scr[e] = s_t * jnp.exp(cum_end) + st_t
        y_scr[pl.ds(r0, CHUNK), :] += y

    def body(i, carry):
        chunk_step(0, i)
        chunk_step(1, nc - 1 - i)
        return carry

    lax.fori_loop(0, nc, body, 0, unroll=4)

    for r0 in range(0, seq_len, blk):
        y = y_scr[r0:r0 + blk, :] * _silu(z_ref[r0:r0 + blk, :])
        ms = jnp.mean(y * y, axis=-1, keepdims=True)
        y_ref[r0:r0 + blk, :] = (y * lax.rsqrt(ms + NORM_EPS) * ng_ref[...]).astype(y_ref.dtype)

    if emit_state:
        for e in range(2):
            sout_ref[0, e, 0] = s_scr[e].T


def _ssd_scan(proj, dtp, w, s0, tok, *, latent, y_buf=None):
    di = w["di"]
    gw = di // SSD_GROUPS
    if latent:
        nseq, seq_len, rb0 = tok.bs, tok.ls, tok.np_rows // tok.ls
    else:
        nseq, seq_len, rb0 = tok.bp, tok.lp, 0
    nx = di // gw
    nb = (2 * di) // SSD_STATE
    ncb = nb + SSD_GROUPS

    def rows(width, col, single=False):
        mode = {"pipeline_mode": pl.Buffered(1)} if single and latent else {}
        return pl.BlockSpec((seq_len, width), lambda b, g: (rb0 + b, col(g)), **mode)

    def vec(width, col, nrows=1):
        return pl.BlockSpec((nrows, width), lambda b, g: (0, col(g)))

    in_specs = [
        rows(gw, lambda g: g, single=True),
        rows(gw, lambda g: nx + g, single=True),
        rows(SSD_STATE, lambda g: nb + g),
        rows(SSD_STATE, lambda g: ncb + g),
        pl.BlockSpec((seq_len, _LANES), lambda b, g: (rb0 + b, g)),
        vec(gw, lambda g: g, 4), vec(gw, lambda g: g),
        vec(SSD_STATE, lambda g: di // SSD_STATE + g, 4), vec(SSD_STATE, lambda g: di // SSD_STATE + g),
        vec(SSD_STATE, lambda g: di // SSD_STATE + SSD_GROUPS + g, 4),
        vec(SSD_STATE, lambda g: di // SSD_STATE + SSD_GROUPS + g),
        pl.BlockSpec((1, 1, _LANES), lambda b, g: (g, 0, 0)),
        pl.BlockSpec((1, 1, _LANES), lambda b, g: (g, 0, 0)),
        vec(gw, lambda g: g),
        vec(gw, lambda g: g),
        pl.BlockSpec((2, _LANES, gw), lambda b, g: (0, 0, 0)),
    ]
    args = [proj, proj, proj, proj, dtp, w["conv_w"], w["conv_b"], w["conv_w"], w["conv_b"],
            w["conv_w"], w["conv_b"], w["dt_bias"], w["a_log"], w["d_skip"], w["ng"], w["expand"]]
    st_spec = pl.BlockSpec((1, 2, 1, gw, SSD_STATE), lambda b, g: (b, 0, g, 0, 0))
    if latent:
        in_specs.append(st_spec)
        args.append(s0)
    y_spec = pl.BlockSpec((seq_len, gw), lambda b, g: (rb0 + b, g))
    y_shape = jax.ShapeDtypeStruct((tok.rows, di), BF16)
    aliases = {}
    if latent:
        out_specs, out_shape = [y_spec], [y_shape]
        aliases[len(args)] = 0
        spec, arg = _in_place([y_buf])
        in_specs += spec
        args += arg
    else:
        out_specs = [y_spec, st_spec]
        out_shape = [y_shape, jax.ShapeDtypeStruct((nseq, 2, SSD_GROUPS, gw, SSD_STATE), F32)]

    return pl.pallas_call(
        functools.partial(_ssd_kernel, seq_len=seq_len, zero_init=not latent, emit_state=not latent,
                          n_alias=len(aliases)),
        grid=(nseq, SSD_GROUPS),
        in_specs=in_specs,
        out_specs=out_specs,
        out_shape=out_shape,
        input_output_aliases=aliases,
        scratch_shapes=[
            pltpu.VMEM((2, seq_len, gw), F32),
            pltpu.VMEM((2, seq_len, gw), F32),
            pltpu.VMEM((2, seq_len // CHUNK, 2 * gw // SSD_HEADDIM, CHUNK), F32),
            pltpu.VMEM((seq_len, SSD_STATE), BF16),
            pltpu.VMEM((seq_len, SSD_STATE), BF16),
            pltpu.VMEM((seq_len, gw), F32),
            pltpu.VMEM((2, SSD_STATE, gw), F32),
        ],
        compiler_params=_cparams("arbitrary", "arbitrary"),
        name="ssd_scan_latent" if latent else "ssd_scan_prompt",
    )(*args)


def _lru_kernel(*refs, seq_len, zero_init, emit_state, n_alias):
    it = iter(refs)
    (x_ref, gate_ref, cw_ref, cb_ref, wa_ref, ba_ref, wi_ref, bi_ref, lam_ref) = (next(it) for _ in range(9))
    s0_ref = None if zero_init else next(it)
    for _ in range(n_alias):
        next(it)
    y_ref = next(it)
    sout_ref = next(it) if emit_state else None
    a_scr, u_scr, h_scr = next(it), next(it), next(it)
    width = a_scr.shape[2]

    blk = 256
    for r0 in range(0, seq_len, blk):
        xc = _conv4_block(x_ref, cw_ref, cb_ref, r0, blk, seq_len)
        xc_bf = xc.astype(BF16)
        for e in range(2):
            rg = jax.nn.sigmoid(_dot(xc_bf, wa_ref[e, 0]) + ba_ref[e])
            ig = jax.nn.sigmoid(_dot(xc_bf, wi_ref[e, 0]) + bi_ref[e])
            log_a = -LRU_C * rg * _softplus(-lam_ref[e])
            a_scr[e, r0:r0 + blk, :] = jnp.exp(log_a)
            u_scr[e, r0:r0 + blk, :] = jnp.sqrt(_neg_expm1(2.0 * log_a)) * ig * xc

    nt = seq_len // _SUBLANES
    row = lax.broadcasted_iota(jnp.int32, (_SUBLANES, width), 0)

    def tile_scan(e, t, carry):
        r0 = pl.multiple_of(t * _SUBLANES, _SUBLANES)
        a = a_scr[e, pl.ds(r0, _SUBLANES), :]
        u = u_scr[e, pl.ds(r0, _SUBLANES), :]
        for s in (1, 2, 4):
            if e == 0:
                valid = row >= s
                a_sh, u_sh = pltpu.roll(a, s, 0), pltpu.roll(u, s, 0)
            else:
                valid = row < _SUBLANES - s
                a_sh, u_sh = pltpu.roll(a, _SUBLANES - s, 0), pltpu.roll(u, _SUBLANES - s, 0)
            u = u + a * jnp.where(valid, u_sh, 0.0)
            a = a * jnp.where(valid, a_sh, 1.0)
        hcur = u + a * carry
        last = _SUBLANES - 1 if e == 0 else 0
        return hcur, hcur[last:last + 1, :]

    def body(i, carry):
        cf, cbk = carry
        hf, cf = tile_scan(0, i, cf)
        r0 = pl.multiple_of(i * _SUBLANES, _SUBLANES)
        h_scr[0, pl.ds(r0, _SUBLANES), :] = hf
        tb = nt - 1 - i
        hb, cbk = tile_scan(1, tb, cbk)
        rb = pl.multiple_of(tb * _SUBLANES, _SUBLANES)
        h_scr[1, pl.ds(rb, _SUBLANES), :] = hb
        return cf, cbk

    if zero_init:
        init = (jnp.zeros((1, width), F32), jnp.zeros((1, width), F32))
    else:
        init = (s0_ref[0, 0], s0_ref[0, 1])
    cf, cbk = lax.fori_loop(0, nt, body, init, unroll=4)

    for r0 in range(0, seq_len, blk):
        hsum = h_scr[0, r0:r0 + blk, :] + h_scr[1, r0:r0 + blk, :]
        y_ref[r0:r0 + blk, :] = (hsum * _gelu_tanh(gate_ref[r0:r0 + blk, :])).astype(y_ref.dtype)

    if emit_state:
        sout_ref[0, 0] = cf
        sout_ref[0, 1] = cbk


def _lru_scan(proj, w, s0, tok, *, latent, y_buf=None):
    width = proj.shape[1] // 2
    bw = width // LRU_BLOCKS
    if latent:
        nseq, seq_len, rb0 = tok.bs, tok.ls, tok.np_rows // tok.ls
    else:
        nseq, seq_len, rb0 = tok.bp, tok.lp, 0
    pair = pl.BlockSpec((2, 1, bw), lambda b, n: (0, 0, n))
    in_specs = [
        pl.BlockSpec((seq_len, bw), lambda b, n: (rb0 + b, n)),
        pl.BlockSpec((seq_len, bw), lambda b, n: (rb0 + b, LRU_BLOCKS + n)),
        pl.BlockSpec((4, bw), lambda b, n: (0, n)),
        pl.BlockSpec((1, bw), lambda b, n: (0, n)),
        pl.BlockSpec((2, 1, bw, bw), lambda b, n: (0, n, 0, 0)), pair,
        pl.BlockSpec((2, 1, bw, bw), lambda b, n: (0, n, 0, 0)), pair,
        pair,
    ]
    args = [proj, proj, w["conv_w"], w["conv_b"], w["w_a"], w["b_a"], w["w_i"], w["b_i"], w["lam"]]
    st_spec = pl.BlockSpec((1, 2, 1, bw), lambda b, n: (b, 0, 0, n))
    if latent:
        in_specs.append(st_spec)
        args.append(s0)
    y_spec = pl.BlockSpec((seq_len, bw), lambda b, n: (rb0 + b, n))
    y_shape = jax.ShapeDtypeStruct((tok.rows, width), BF16)
    aliases = {}
    if latent:
        out_specs, out_shape = [y_spec], [y_shape]
        aliases[len(args)] = 0
        spec, arg = _in_place([y_buf])
        in_specs += spec
        args += arg
    else:
        out_specs = [y_spec, st_spec]
        out_shape = [y_shape, jax.ShapeDtypeStruct((nseq, 2, 1, width), F32)]
    return pl.pallas_call(
        functools.partial(_lru_kernel, seq_len=seq_len, zero_init=not latent, emit_state=not latent,
                          n_alias=len(aliases)),
        grid=(nseq, LRU_BLOCKS),
        in_specs=in_specs,
        out_specs=out_specs,
        out_shape=out_shape,
        input_output_aliases=aliases,
        scratch_shapes=[
            pltpu.VMEM((2, seq_len, bw), F32),
            pltpu.VMEM((2, seq_len, bw), F32),
            pltpu.VMEM((2, seq_len, bw), F32),
        ],
        compiler_params=_cparams("arbitrary", "arbitrary"),
        name="lru_scan_latent" if latent else "lru_scan_prompt",
    )(*args)


def _prep_gla(w_g1, w_g2, b_g, norm_g):
    qkw = w_g2.shape[2]
    g1 = jnp.concatenate([w_g1[0], w_g1[1]], axis=1)
    g1 = jnp.pad(g1, ((0, 0), (0, _LANES - 2 * GLA_RANK)))
    w2 = jnp.zeros((2, _LANES, qkw), F32)
    w2 = w2.at[0, 0:GLA_RANK].set(w_g2[0]).at[1, GLA_RANK:2 * GLA_RANK].set(w_g2[1])
    return {
        "w_g1cat": g1.astype(BF16), "w2": w2.astype(BF16),
        "bg": b_g.reshape(2, 1, qkw), "ng": norm_g.reshape(1, -1),
    }


def _prep_ssd(w_in_dt, conv_w, conv_b, a_log, dt_bias, d_skip, norm_g):
    heads = a_log.shape[1]
    di = heads * SSD_HEADDIM
    hpg = heads // SSD_GROUPS
    gw = di // SSD_GROUPS
    main = 2 * di + 2 * SSD_GROUPS * SSD_STATE

    ncopy = 3
    pad = _LANES - ncopy * 2 * hpg

    def by_group(t):
        t = t.reshape(2, SSD_GROUPS, hpg).transpose(1, 0, 2).reshape(SSD_GROUPS, 1, 2 * hpg)
        return jnp.pad(jnp.tile(t, (1, 1, ncopy)), ((0, 0), (0, 0), (0, pad)))

    w_dt = w_in_dt.reshape(-1, 2, SSD_GROUPS, hpg).transpose(0, 2, 1, 3)
    w_dt = jnp.tile(w_dt.reshape(-1, SSD_GROUPS, 2 * hpg), (1, 1, ncopy))
    w_dt = jnp.pad(w_dt, ((0, 0), (0, 0), (0, pad))).reshape(-1, SSD_GROUPS * _LANES)
    lane = jnp.arange(_LANES)[:, None]
    chan = jnp.arange(gw)[None, :] // SSD_HEADDIM
    expand = jnp.stack([(lane % (2 * hpg) == e * hpg + chan) & (lane < ncopy * 2 * hpg)
                        for e in range(2)]).astype(BF16)
    return {
        "di": di, "w_dt": w_dt.astype(BF16), "n_main": main,
        "conv_w": conv_w, "conv_b": conv_b.reshape(1, -1),
        "dt_bias": by_group(dt_bias), "a_log": by_group(a_log),
        "d_skip": jnp.repeat(d_skip, SSD_HEADDIM).reshape(1, di), "ng": norm_g.reshape(1, di),
        "expand": expand,
    }


def _prep_lru(conv_w, conv_b, w_a, b_a, w_i, b_i, lam):
    width = conv_w.shape[1]
    return {
        "conv_w": conv_w, "conv_b": conv_b.reshape(1, width),
        "w_a": w_a.astype(BF16), "b_a": b_a.reshape(2, 1, width),
        "w_i": w_i.astype(BF16), "b_i": b_i.reshape(2, 1, width), "lam": lam.reshape(2, 1, width),
    }


def kernel(x_prompt, x_sample, state_gla, state_ssd, state_lru, c, c_ctx, w_mod, b_mod, norm_mix_g,
           norm_ffn_g, ffn_w_up, ffn_conv_w, ffn_conv_b, ffn_w_down, final_norm_g, gla_w_in, gla_w_g1,
           gla_w_g2, gla_b_g, gla_norm_g, gla_w_out, ssd_w_in, ssd_conv_w, ssd_conv_b, ssd_a_log,
           ssd_dt_bias, ssd_d, ssd_norm_g, ssd_w_out, lru_w_in, lru_conv_w, lru_conv_b, lru_w_a, lru_b_a,
           lru_w_i, lru_b_i, lru_lambda, lru_w_out):
    bp, lp, d = x_prompt.shape
    bs, ls, _ = x_sample.shape
    depth = w_mod.shape[0]
    tok = _Tokens(bp, lp, bs, ls)
    assert bs + 1 <= MOD_ROWS and ls // GRID_W * GRID_W == ls

    x = jnp.concatenate([x_prompt.reshape(bp * lp, d), x_sample.reshape(bs * ls, d)], axis=0)
    c_rows = jnp.concatenate([c_ctx[None], c, jnp.zeros((MOD_ROWS - 1 - bs, d), F32)], axis=0)
    mod = _mod_table(c_rows, w_mod, b_mod).reshape(depth * MOD_ROWS, MOD_CHUNKS, d)

    ffn_w_down_bf = ffn_w_down.astype(BF16)
    ssd_w_out_bf = ssd_w_out.astype(BF16)
    gla_states, ssd_new, lru_new = None, [], []
    for l in range(depth):
        kind, j = l % 3, l // 3
        h = _norm_mod(x, norm_mix_g[l], mod, l, tok, shift_idx=0, scale_idx=1)
        if kind == 0:
            w = _prep_gla(gla_w_g1[j], gla_w_g2[j], gla_b_g[j], gla_norm_g[j])
            y, gla_states = _gla_mixer(h, state_gla[:, j], gla_w_in, j, w, tok, gla_states)
            w_out = gla_w_out
        elif kind == 1:
            n_main = ssd_conv_w.shape[2] + ssd_d.shape[1] * SSD_HEADDIM
            w = _prep_ssd(ssd_w_in[j, :, n_main:], ssd_conv_w[j], ssd_conv_b[j], ssd_a_log[j],
                          ssd_dt_bias[j], ssd_d[j], ssd_norm_g[j])
            proj = _matmul(h, ssd_w_in, j, col_start=0, n_cols=n_main, bm=1024, bn=1024,
                           name="ssd_in_main")
            dtp = _matmul(h, w["w_dt"][None], 0, bm=1024, bn=SSD_GROUPS * _LANES, name="ssd_in_dt")
            gw = w["di"] // SSD_GROUPS
            s_lat0 = state_ssd[:, j].reshape(bs, 2, SSD_GROUPS, gw, SSD_STATE)
            y, s_new = _ssd_scan(proj, dtp, w, None, tok, latent=False)
            (y,) = _ssd_scan(proj, dtp, w, s_lat0, tok, latent=True, y_buf=y)
            ssd_new.append(s_new.reshape(bp, 2, -1, SSD_HEADDIM, SSD_STATE))
            w_out = ssd_w_out_bf
        else:
            w = _prep_lru(lru_conv_w[j], lru_conv_b[j], lru_w_a[j], lru_b_a[j],
                          lru_w_i[j], lru_b_i[j], lru_lambda[j])
            proj = _matmul(h, lru_w_in, j, bm=1024, bn=1024, name="lru_in")
            s_lat0 = state_lru[:, j].reshape(bs, 2, 1, -1)
            y, s_new = _lru_scan(proj, w, None, tok, latent=False)
            (y,) = _lru_scan(proj, w, s_lat0, tok, latent=True, y_buf=y)
            lru_new.append(s_new.reshape(bp, 2, -1))
            w_out = lru_w_out
        x = _matmul(y, w_out, j, bm=1024, bn=1024, bk=min(w_out.shape[1], 2048),
                    residual=(x, mod, l, tok, 2), name="mixer_out")
        h = _norm_mod(x, norm_ffn_g[l], mod, l, tok, shift_idx=3, scale_idx=4)
        g = _ffn_up(h, ffn_w_up, ffn_conv_w, ffn_conv_b, l, tok)
        x = _matmul(g, ffn_w_down_bf, l, bm=1024, bn=1024, bk=ffn_w_down.shape[1] // 2,
                    residual=(x, mod, l, tok, 5), name="ffn_down")
    y_prompt, y_sample = _final_norm(x, final_norm_g, tok)
    return (y_prompt.reshape(bp, lp, d), y_sample.reshape(bs, ls, d), gla_states,
            jnp.stack(ssd_new, axis=1), jnp.stack(lru_new, axis=1))
```

```python
import functools

import jax
import jax.numpy as jnp
from jax import lax
from jax.experimental import pallas as pl
from jax.experimental.pallas import tpu as pltpu

F32 = jnp.float32
BF16 = jnp.bfloat16

_V7X_VMEM_BYTES = 64 * 1024 * 1024
_VMEM_LIMIT = _V7X_VMEM_BYTES - 8 * 1024 * 1024
_SUBLANES = 8
_LANES = 128

NORM_EPS = 1e-6
MOD_CHUNKS = 6
MOD_ROWS = 8
GRID_W = 64

GLA_HEADS = 4
GLA_RANK = 16
GLA_GATE_NORM = 16.0
SSD_HEADDIM = 64
SSD_GROUPS = 8
SSD_STATE = 128
LRU_BLOCKS = 8
LRU_C = 8.0
CHUNK = 64


def _cparams(*sem):
    return pltpu.CompilerParams(dimension_semantics=sem, vmem_limit_bytes=_VMEM_LIMIT)


def _sigmoid(x):
    return 0.5 + 0.5 * jnp.tanh(0.5 * x)


def _silu(x):
    h = 0.5 * x
    return h + h * jnp.tanh(h)


def _softplus(x):
    return jnp.maximum(x, 0.0) + jnp.log1p(jnp.exp(-jnp.abs(x)))


def _log_sigmoid(x):
    return jnp.minimum(x, 0.0) - jnp.log1p(jnp.exp(-jnp.abs(x)))


def _gelu_tanh(x):
    return 0.5 * x * (1.0 + jnp.tanh(0.7978845608028654 * (x + 0.044715 * (x * x * x))))


def _neg_expm1_2x(x):
    t = jnp.tanh(x)
    return -2.0 * t / (1.0 - t)


def _split3(x):
    x1 = x.astype(BF16)
    r1 = x - x1.astype(F32)
    x2 = r1.astype(BF16)
    x3 = (r1 - x2.astype(F32)).astype(BF16)
    return x1, x2, x3


def _dot(a, b):
    return jnp.dot(a, b, preferred_element_type=F32)


def _dot_nt(a, b):
    return lax.dot_general(a, b, (((1,), (1,)), ((), ())), preferred_element_type=F32)


def _dot_tn(a, b):
    return lax.dot_general(a, b, (((0,), (0,)), ((), ())), preferred_element_type=F32)


def _exact_dot_left01(m01, x):
    x1, x2, x3 = _split3(x)
    return _dot(m01, x1) + _dot(m01, x2) + _dot(m01, x3)


def _exact_select_copies(x, sel01, ncopy_lanes):
    t1 = x.astype(BF16).astype(F32)
    r1 = x - t1
    t2 = r1.astype(BF16).astype(F32)
    lane = lax.broadcasted_iota(jnp.int32, x.shape, 1)
    terms = jnp.where(lane < ncopy_lanes, t1, jnp.where(lane < 2 * ncopy_lanes, t2, r1 - t2))
    return _dot(terms.astype(BF16), sel01)


def _chunk_masks():
    ii = lax.broadcasted_iota(jnp.int32, (CHUNK, CHUNK), 0)
    jj = lax.broadcasted_iota(jnp.int32, (CHUNK, CHUNK), 1)
    masks = (jj <= ii, jj >= ii)
    tris = tuple(jnp.where(m, 1.0, 0.0).astype(BF16) for m in masks)
    return masks, tris


def _block_chunk_masks(rows):
    ii = lax.broadcasted_iota(jnp.int32, (rows, rows), 0)
    jj = lax.broadcasted_iota(jnp.int32, (rows, rows), 1)
    shift = CHUNK.bit_length() - 1
    same = jnp.where(jnp.right_shift(ii, shift) == jnp.right_shift(jj, shift), 1, 0)
    masks = (same * jnp.where(jj <= ii, 1, 0) > 0, same * jnp.where(jj >= ii, 1, 0) > 0)
    tris = tuple(jnp.where(m, 1.0, 0.0).astype(BF16) for m in masks)
    return masks, tris


def _shifted_rows(ref, r0, nrows, off, total):
    lo, hi = r0 + off, r0 + off + nrows
    clo, chi = max(lo, 0), min(hi, total)
    x = ref[clo:chi, :]
    parts = []
    if clo > lo:
        parts.append(jnp.zeros((clo - lo, x.shape[1]), x.dtype))
    parts.append(x)
    if hi > chi:
        parts.append(jnp.zeros((hi - chi, x.shape[1]), x.dtype))
    return x if len(parts) == 1 else jnp.concatenate(parts, axis=0)


def _conv4_block(ref, w_ref, b_ref, r0, nrows, total):
    acc = b_ref[...] + w_ref[0:1, :] * _shifted_rows(ref, r0, nrows, -1, total)
    acc = acc + w_ref[1:2, :] * ref[r0:r0 + nrows, :]
    acc = acc + w_ref[2:3, :] * _shifted_rows(ref, r0, nrows, 1, total)
    acc = acc + w_ref[3:4, :] * _shifted_rows(ref, r0, nrows, 2, total)
    return acc


def _mod_kernel(c_ref, w_ref, b_ref, o_ref):
    c = c_ref[...]
    s = _silu(c).astype(BF16)
    o_ref[0] = _dot(s, w_ref[0].astype(BF16)) + b_ref[0]


def _mod_table(c_rows, w_mod, b_mod):
    depth, d, n = w_mod.shape
    bn = 1024
    return pl.pallas_call(
        _mod_kernel,
        grid=(depth, n // bn),
        in_specs=[
            pl.BlockSpec((MOD_ROWS, d), lambda l, j: (0, 0)),
            pl.BlockSpec((1, d, bn), lambda l, j: (l, 0, j)),
            pl.BlockSpec((1, 1, bn), lambda l, j: (l, 0, j)),
        ],
        out_specs=pl.BlockSpec((1, MOD_ROWS, bn), lambda l, j: (l, 0, j)),
        out_shape=jax.ShapeDtypeStruct((depth, MOD_ROWS, n), F32),
        compiler_params=_cparams("arbitrary", "arbitrary"),
        name="adaln_table",
    )(c_rows, w_mod, b_mod.reshape(depth, 1, n))


class _Tokens:
    def __init__(self, n_prompt, l_prompt, n_latent, l_latent):
        self.bp, self.lp, self.bs, self.ls = n_prompt, l_prompt, n_latent, l_latent
        self.np_rows = n_prompt * l_prompt
        self.ns_rows = n_latent * l_latent
        self.rows = self.np_rows + self.ns_rows

    def mod_row(self, i, bm):
        r0 = i * bm
        return jnp.where(r0 < self.np_rows, 0, 1 + (r0 - self.np_rows) // self.ls)


def _norm_mod_kernel(x_ref, g_ref, mod_ref, h_ref, *, shift_idx, scale_idx):
    x = x_ref[...]
    ms = jnp.mean(x * x, axis=-1, keepdims=True)
    y = x * lax.rsqrt(ms + NORM_EPS) * g_ref[...]
    h = y * (1.0 + mod_ref[0, scale_idx:scale_idx + 1, :]) + mod_ref[0, shift_idx:shift_idx + 1, :]
    h_ref[...] = h.astype(h_ref.dtype)


def _norm_mod(x, g, mod, layer, tok, *, shift_idx, scale_idx):
    rows, d = x.shape
    bm = 512
    return pl.pallas_call(
        functools.partial(_norm_mod_kernel, shift_idx=shift_idx, scale_idx=scale_idx),
        grid=(rows // bm,),
        in_specs=[
            pl.BlockSpec((bm, d), lambda i: (i, 0)),
            pl.BlockSpec((1, d), lambda i: (0, 0)),
            pl.BlockSpec((1, MOD_CHUNKS, d), lambda i: (layer * MOD_ROWS + tok.mod_row(i, bm), 0, 0)),
        ],
        out_specs=pl.BlockSpec((bm, d), lambda i: (i, 0)),
        out_shape=jax.ShapeDtypeStruct((rows, d), BF16),
        compiler_params=_cparams("arbitrary"),
        name="norm_adaln",
    )(x, g.reshape(1, d), mod)


def _final_norm_kernel(x_ref, g_ref, op_ref, os_ref, *, n_prompt_tiles):
    x = x_ref[...]
    ms = jnp.mean(x * x, axis=-1, keepdims=True)
    y = x * lax.rsqrt(ms + NORM_EPS) * g_ref[...]

    @pl.when(pl.program_id(0) < n_prompt_tiles)
    def _():
        op_ref[...] = y

    @pl.when(pl.program_id(0) >= n_prompt_tiles)
    def _():
        os_ref[...] = y


def _final_norm(x, g, tok):
    rows, d = x.shape
    bm = 512
    npt = tok.np_rows // bm
    return pl.pallas_call(
        functools.partial(_final_norm_kernel, n_prompt_tiles=npt),
        grid=(rows // bm,),
        in_specs=[pl.BlockSpec((bm, d), lambda i: (i, 0)), pl.BlockSpec((1, d), lambda i: (0, 0))],
        out_specs=[pl.BlockSpec((bm, d), lambda i: (jnp.minimum(i, npt - 1), 0)),
                   pl.BlockSpec((bm, d), lambda i: (jnp.maximum(i - npt, 0), 0))],
        out_shape=[jax.ShapeDtypeStruct((tok.np_rows, d), F32),
                   jax.ShapeDtypeStruct((tok.ns_rows, d), F32)],
        compiler_params=_cparams("arbitrary"),
        name="final_norm",
    )(x, g.reshape(1, d))


def _mm_kernel(*refs, nk, gate_idx):
    if gate_idx is None:
        x_ref, w_ref, o_ref, acc_ref, wb_ref = refs
        res_ref = mod_ref = None
    else:
        x_ref, w_ref, res_ref, mod_ref, o_ref, acc_ref, wb_ref = refs
    i, k = pl.program_id(1), pl.program_id(2)
    if w_ref.dtype == BF16:
        w = w_ref[0]
    elif nk == 1:
        @pl.when(i == 0)
        def _():
            wb_ref[...] = w_ref[0].astype(BF16)

        w = wb_ref[...]
    else:
        w = w_ref[0].astype(BF16)
    part = _dot(x_ref[...], w)

    def finish(acc):
        if gate_idx is None:
            o_ref[...] = acc.astype(o_ref.dtype)
        else:
            o_ref[...] = res_ref[...] + mod_ref[0, gate_idx:gate_idx + 1, :] * acc

    if nk == 1:
        finish(part)
        return

    @pl.when(k == 0)
    def _():
        acc_ref[...] = part

    @pl.when(jnp.logical_and(k > 0, k < nk - 1))
    def _():
        acc_ref[...] += part

    @pl.when(k == nk - 1)
    def _():
        finish(acc_ref[...] + part)


def _matmul(x, w, layer, *, col_start=0, n_cols=None, bm, bn, bk=None, out_dtype=F32, residual=None,
            name="matmul"):
    m, kdim = x.shape
    n_cols = w.shape[2] - col_start if n_cols is None else n_cols
    bk = kdim if bk is None else bk
    nk = kdim // bk
    off = col_start // bn
    in_specs = [
        pl.BlockSpec((bm, bk), lambda j, i, k: (i, k)),
        pl.BlockSpec((1, bk, bn), lambda j, i, k: (layer, k, j + off)),
    ]
    args = [x, w]
    gate_idx = None
    if residual is not None:
        res, mod, mod_layer, tok, gate_idx = residual
        in_specs += [
            pl.BlockSpec((bm, bn), lambda j, i, k: (i, j)),
            pl.BlockSpec((1, MOD_CHUNKS, bn),
                         lambda j, i, k: (mod_layer * MOD_ROWS + tok.mod_row(i, bm), 0, j)),
        ]
        args += [res, mod]
    cast_once = w.dtype != BF16 and nk == 1
    tiny = (2 * _SUBLANES, _LANES)
    return pl.pallas_call(
        functools.partial(_mm_kernel, nk=nk, gate_idx=gate_idx),
        grid=(n_cols // bn, m // bm, nk),
        in_specs=in_specs,
        out_specs=pl.BlockSpec((bm, bn), lambda j, i, k: (i, j)),
        out_shape=jax.ShapeDtypeStruct((m, n_cols), out_dtype),
        scratch_shapes=[pltpu.VMEM((bm, bn) if nk > 1 else tiny, F32),
                        pltpu.VMEM((bk, bn) if cast_once else tiny, BF16)],
        compiler_params=_cparams("arbitrary", "arbitrary", "arbitrary"),
        name=name,
    )(*args)


def _ffn_up_kernel(h_ref, wa_ref, wv_ref, cw_ref, cb_ref, g_ref, *, n_prompt_tiles, w_prompt, w_latent):
    i = pl.program_id(0)
    x = h_ref[...]
    act = _dot(x, wa_ref[0].astype(BF16))
    val = _dot(x, wv_ref[0].astype(BF16))
    t, tn = act.shape
    row = lax.broadcasted_iota(jnp.int32, (t, tn), 0)
    prev = pltpu.roll(act, 1, 0)
    nxt = pltpu.roll(act, t - 1, 0)

    def horizontal(width):
        col = row & (width - 1)
        return jnp.where(col > 0, prev, 0.0), jnp.where(col < width - 1, nxt, 0.0)

    def tap_row(kh, left, right):
        return (cw_ref[3 * kh:3 * kh + 1, :] * left + cw_ref[3 * kh + 1:3 * kh + 2, :] * act
                + cw_ref[3 * kh + 2:3 * kh + 3, :] * right)

    @pl.when(i < n_prompt_tiles)
    def _():
        left, right = horizontal(w_prompt)
        conv = tap_row(1, left, right) + cb_ref[...]
        g_ref[...] = (_silu(conv) * val).astype(g_ref.dtype)

    @pl.when(i >= n_prompt_tiles)
    def _():
        left, right = horizontal(w_latent)
        zeros = jnp.zeros((w_latent, tn), F32)
        above = jnp.concatenate([zeros, tap_row(0, left, right)[:t - w_latent, :]], axis=0)
        below = jnp.concatenate([tap_row(2, left, right)[w_latent:, :], zeros], axis=0)
        conv = above + tap_row(1, left, right) + below + cb_ref[...]
        g_ref[...] = (_silu(conv) * val).astype(g_ref.dtype)


def _ffn_up(h, w_up, conv_w, conv_b, layer, tok):
    rows, d = h.shape
    f = w_up.shape[2] // 2
    tm = tok.ls
    tn = 256
    assert tok.np_rows % tm == 0 and tm % tok.lp == 0
    nj = f // tn
    return pl.pallas_call(
        functools.partial(_ffn_up_kernel, n_prompt_tiles=tok.np_rows // tm,
                          w_prompt=tok.lp, w_latent=GRID_W),
        grid=(rows // tm, nj),
        in_specs=[
            pl.BlockSpec((tm, d), lambda i, j: (i, 0)),
            pl.BlockSpec((1, d, tn), lambda i, j: (layer, 0, j)),
            pl.BlockSpec((1, d, tn), lambda i, j: (layer, 0, j + nj)),
            pl.BlockSpec((9, tn), lambda i, j: (0, j)),
            pl.BlockSpec((1, tn), lambda i, j: (0, j)),
        ],
        out_specs=pl.BlockSpec((tm, tn), lambda i, j: (i, j)),
        out_shape=jax.ShapeDtypeStruct((rows, f), BF16),
        compiler_params=_cparams("arbitrary", "arbitrary"),
        name="ffn_up_conv",
    )(h, w_up, w_up, conv_w[layer].reshape(9, f), conv_b[layer].reshape(1, f))


def _gla_kernel(*refs, seq_len, zero_init, emit_state, n_alias, dk, dv):
    it = iter(refs)
    q_ref, k_ref, v_ref, r_ref, low_ref, w2_ref, bg_ref, ng_ref = (next(it) for _ in range(8))
    s0_ref = None if zero_init else next(it)
    for _ in range(n_alias):
        next(it)
    y_ref = next(it)
    sout_ref = next(it).at[0, 0] if emit_state else None
    g_scr, o_scr, s_scr = next(it), next(it), next(it)

    nc = seq_len // CHUNK
    blk = 256
    cpb = blk // CHUNK
    nb = seq_len // blk
    masks, tris = _block_chunk_masks(blk)
    mid_row = (CHUNK // 2, CHUNK // 2 - 1)
    end_row = (CHUNK - 1, 0)
    q_scale = dk ** -0.5

    low = low_ref[...].astype(BF16)
    for e in range(2):
        z = _dot(low, w2_ref[e]) + bg_ref[e]
        g_scr[e] = _log_sigmoid(z) * (1.0 / GLA_GATE_NORM)
        if zero_init:
            s_scr[e] = jnp.zeros((dv, dk), F32)
        else:
            s_scr[e] = s0_ref[0, e, 0].T

    def block_step(e, b):
        r0 = pl.multiple_of(b * blk, blk)
        gc = _exact_dot_left01(tris[e], g_scr[e, pl.ds(r0, blk), :])
        g_scr[e, pl.ds(r0, blk), :] = gc
        g_mid = jnp.concatenate(
            [jnp.broadcast_to(gc[c * CHUNK + mid_row[e]:c * CHUNK + mid_row[e] + 1, :], (CHUNK, dk))
             for c in range(cpb)], axis=0)
        q = q_ref[pl.ds(r0, blk), :] * q_scale
        k = k_ref[pl.ds(r0, blk), :]
        scores = _dot_nt((q * jnp.exp(gc - g_mid)).astype(BF16), (k * jnp.exp(g_mid - gc)).astype(BF16))
        o_scr[e, pl.ds(r0, blk), :] = _dot(jnp.where(masks[e], scores, 0.0).astype(BF16),
                                            v_ref[pl.ds(r0, blk), :])

    def block_body(b, carry):
        block_step(0, b)
        block_step(1, b)
        return carry

    lax.fori_loop(0, nb, block_body, 0, unroll=min(2, nb))

    def chunk_step(e, c):
        r0 = pl.multiple_of(c * CHUNK, CHUNK)
        gc = g_scr[e, pl.ds(r0, CHUNK), :]
        g_end = gc[end_row[e]:end_row[e] + 1, :]
        q = q_ref[pl.ds(r0, CHUNK), :] * q_scale
        k = k_ref[pl.ds(r0, CHUNK), :]
        v = v_ref[pl.ds(r0, CHUNK), :]
        s_t = s_scr[e]
        o_scr[e, pl.ds(r0, CHUNK), :] += _dot_nt((q * jnp.exp(gc)).astype(BF16), s_t.astype(BF16))
        u_t = _dot_tn(v, (k * jnp.exp(g_end - gc)).astype(BF16))
        s_scr[e] = s_t * jnp.exp(g_end) + u_t

    def body(i, carry):
        chunk_step(0, i)
        chunk_step(1, nc - 1 - i)
        return carry

    lax.fori_loop(0, nc, body, 0, unroll=4)

    for r0 in range(0, seq_len, blk):
        o = o_scr[0, r0:r0 + blk, :] + o_scr[1, r0:r0 + blk, :]
        ms = jnp.mean(o * o, axis=-1, keepdims=True)
        y = o * lax.rsqrt(ms + NORM_EPS) * ng_ref[...]
        y_ref[r0:r0 + blk, :] = (y * _silu(r_ref[r0:r0 + blk, :])).astype(y_ref.dtype)

    if emit_state:
        for e in range(2):
            sout_ref[e, 0] = s_scr[e].T


def _in_place(buffers):
    return [pl.BlockSpec(memory_space=pl.ANY) for _ in buffers], list(buffers)


def _gla_scan(qk, v, r, low, w2, bg, ng, s0, tok, *, latent, y_buf=None, state_buf=None, layer=0,
              n_layers=1):
    dk = qk.shape[1] // (2 * GLA_HEADS)
    dv = v.shape[1] // GLA_HEADS
    if latent:
        nseq, seq_len, rb0 = tok.bs, tok.ls, tok.np_rows // tok.ls
    else:
        nseq, seq_len, rb0 = tok.bp, tok.lp, 0
    in_specs = [
        pl.BlockSpec((seq_len, dk), lambda b, h: (rb0 + b, h)),
        pl.BlockSpec((seq_len, dk), lambda b, h: (rb0 + b, GLA_HEADS + h)),
        pl.BlockSpec((seq_len, dv), lambda b, h: (rb0 + b, h)),
        pl.BlockSpec((seq_len, dv), lambda b, h: (rb0 + b, h)),
        pl.BlockSpec((seq_len, _LANES), lambda b, h: (rb0 + b, 0)),
        pl.BlockSpec((2, _LANES, dk), lambda b, h: (0, 0, h)),
        pl.BlockSpec((2, 1, dk), lambda b, h: (0, 0, h)),
        pl.BlockSpec((1, dv), lambda b, h: (0, 0)),
    ]
    args = [qk, qk, v, r, low, w2, bg, ng]
    if latent:
        in_specs.append(pl.BlockSpec((1, 2, 1, dk, dv), lambda b, h: (b, 0, h, 0, 0)))
        args.append(s0)
    y_spec = pl.BlockSpec((seq_len, dv), lambda b, h: (rb0 + b, h))
    y_shape = jax.ShapeDtypeStruct((tok.rows, v.shape[1]), BF16)
    if latent:
        out_specs, out_shape, bufs = [y_spec], [y_shape], [y_buf]
    else:
        out_specs = [y_spec, pl.BlockSpec((1, 1, 2, 1, dk, dv), lambda b, h: (b, layer, 0, h, 0, 0))]
        out_shape = [y_shape, jax.ShapeDtypeStruct((nseq, n_layers, 2, GLA_HEADS, dk, dv), F32)]
        bufs = [y_buf, state_buf]
    aliases = {}
    for o, buf in enumerate(bufs):
        if buf is not None:
            aliases[len(args)] = o
            spec, arg = _in_place([buf])
            in_specs += spec
            args += arg
    return pl.pallas_call(
        functools.partial(_gla_kernel, seq_len=seq_len, zero_init=not latent, emit_state=not latent,
                          n_alias=len(aliases), dk=dk, dv=dv),
        grid=(nseq, GLA_HEADS),
        in_specs=in_specs,
        out_specs=out_specs,
        out_shape=out_shape,
        input_output_aliases=aliases,
        scratch_shapes=[
            pltpu.VMEM((2, seq_len, dk), F32),
            pltpu.VMEM((2, seq_len, dv), F32),
            pltpu.VMEM((2, dv, dk), F32),
        ],
        compiler_params=_cparams("arbitrary", "arbitrary"),
        name="gla_scan_latent" if latent else "gla_scan_prompt",
    )(*args)


def _gla_mixer(h, s_lat0, w_in, layer, w, tok, state_buf):
    w_g1cat, w2, bg, ng = w["w_g1cat"], w["w2"], w["bg"], w["ng"]
    qkw = 2 * w2.shape[2]
    vw = (w_in.shape[2] - qkw) // 2
    qk = _matmul(h, w_in, layer, col_start=0, n_cols=qkw, bm=1024, bn=1024, name="gla_in_qk")
    v = _matmul(h, w_in, layer, col_start=qkw, n_cols=vw, bm=1024, bn=1024, out_dtype=BF16,
                name="gla_in_v")
    r = _matmul(h, w_in, layer, col_start=qkw + vw, n_cols=vw, bm=1024, bn=1024, name="gla_in_r")
    low = _matmul(h, w_g1cat[None], 0, bm=1024, bn=_LANES, name="gla_in_gate")
    y, states = _gla_scan(qk, v, r, low, w2, bg, ng, None, tok, latent=False, state_buf=state_buf,
                          layer=layer, n_layers=w_in.shape[0])
    (y,) = _gla_scan(qk, v, r, low, w2, bg, ng, s_lat0, tok, latent=True, y_buf=y)
    return y, states


def _ssd_kernel(*refs, seq_len, zero_init, emit_state, n_alias):
    it = iter(refs)
    (z_ref, x_ref, b_ref, c_ref, dt_ref, cwx_ref, cbx_ref, cwb_ref, cbb_ref, cwc_ref, cbc_ref,
     dtb_ref, alog_ref, dskip_ref, ng_ref, e_ref) = (next(it) for _ in range(16))
    s0_ref = None if zero_init else next(it)
    for _ in range(n_alias):
        next(it)
    y_ref = next(it)
    sout_ref = next(it) if emit_state else None
    xd_scr, cumx_scr, cumt_scr, bm_scr, cm_scr, y_scr, s_scr = (next(it) for _ in range(7))

    hpg = y_scr.shape[1] // SSD_HEADDIM
    nc = seq_len // CHUNK
    blk = 256
    cpb = blk // CHUNK
    masks, _ = _chunk_masks()
    _, blk_tris = _block_chunk_masks(blk)
    end_row = (CHUNK - 1, 0)
    a_row = -jnp.exp(alog_ref[0])

    for r0 in range(0, seq_len, blk):
        xs = _silu(_conv4_block(x_ref, cwx_ref, cbx_ref, r0, blk, seq_len))
        y_scr[r0:r0 + blk, :] = dskip_ref[...] * xs
        bm_scr[r0:r0 + blk, :] = _silu(_conv4_block(b_ref, cwb_ref, cbb_ref, r0, blk, seq_len)).astype(BF16)
        cm_scr[r0:r0 + blk, :] = _silu(_conv4_block(c_ref, cwc_ref, cbc_ref, r0, blk, seq_len)).astype(BF16)
        dt = _softplus(dt_ref[r0:r0 + blk, :] + dtb_ref[0])
        for e in range(2):
            expand = e_ref[e]
            cum = _exact_dot_left01(blk_tris[e], dt * a_row)
            cum_t = cum.T
            for cc in range(cpb):
                cumt_scr[e, r0 // CHUNK + cc] = cum_t[0:2 * hpg, cc * CHUNK:(cc + 1) * CHUNK]
            cumx_scr[e, r0:r0 + blk, :] = _exact_select_copies(cum, expand, 2 * hpg)
            xd_scr[e, r0:r0 + blk, :] = xs * _exact_select_copies(dt, expand, 2 * hpg)

    for e in range(2):
        if zero_init:
            s_scr[e] = jnp.zeros(s_scr.shape[1:], F32)
        else:
            s_scr[e] = s0_ref[0, e, 0].T

    def chunk_step(e, c):
        r0 = pl.multiple_of(c * CHUNK, CHUNK)
        cum_x = cumx_scr[e, pl.ds(r0, CHUNK), :]
        cum_t = cumt_scr[e, c]
        xd = xd_scr[e, pl.ds(r0, CHUNK), :]
        xd_bf = xd.astype(BF16)
        bm = bm_scr[pl.ds(r0, CHUNK), :]
        cm = cm_scr[pl.ds(r0, CHUNK), :]
        cb = _dot_nt(cm, bm)
        parts = []
        for hh in range(hpg):
            lane = slice(hh * SSD_HEADDIM, (hh + 1) * SSD_HEADDIM)
            seg = cum_x[:, lane] - cum_t[e * hpg + hh:e * hpg + hh + 1, :]
            decay = jnp.where(masks[e], jnp.exp(jnp.where(masks[e], seg, 0.0)), 0.0)
            parts.append(_dot((cb * decay).astype(BF16), xd_bf[:, lane]))
        y = jnp.concatenate(parts, axis=1)
        s_t = s_scr[e]
        y = y + _dot(cm, s_t.astype(BF16)) * jnp.exp(cum_x)
        cum_end = cum_x[end_row[e]:end_row[e] + 1, :]
        st_t = _dot_tn(bm, (xd * jnp.exp(cum_end - cum_x)).astype(BF16))
        s_scr[e] = s_t * jnp.exp(cum_end) + st_t
        y_scr[pl.ds(r0, CHUNK), :] += y

    def body(i, carry):
        chunk_step(0, i)
        chunk_step(1, nc - 1 - i)
        return carry

    lax.fori_loop(0, nc, body, 0, unroll=4)

    for r0 in range(0, seq_len, blk):
        y = y_scr[r0:r0 + blk, :] * _silu(z_ref[r0:r0 + blk, :])
        ms = jnp.mean(y * y, axis=-1, keepdims=True)
        y_ref[r0:r0 + blk, :] = (y * lax.rsqrt(ms + NORM_EPS) * ng_ref[...]).astype(y_ref.dtype)

    if emit_state:
        for e in range(2):
            sout_ref[0, e, 0] = s_scr[e].T


def _ssd_scan(proj, dtp, w, s0, tok, *, latent, y_buf=None):
    di = w["di"]
    gw = di // SSD_GROUPS
    if latent:
        nseq, seq_len, rb0 = tok.bs, tok.ls, tok.np_rows // tok.ls
    else:
        nseq, seq_len, rb0 = tok.bp, tok.lp, 0
    nx = di // gw
    nb = (2 * di) // SSD_STATE
    ncb = nb + SSD_GROUPS

    def rows(width, col, single=False):
        mode = {"pipeline_mode": pl.Buffered(1)} if single and latent else {}
        return pl.BlockSpec((seq_len, width), lambda b, g: (rb0 + b, col(g)), **mode)

    def vec(width, col, nrows=1):
        return pl.BlockSpec((nrows, width), lambda b, g: (0, col(g)))

    in_specs = [
        rows(gw, lambda g: g, single=True),
        rows(gw, lambda g: nx + g, single=True),
        rows(SSD_STATE, lambda g: nb + g),
        rows(SSD_STATE, lambda g: ncb + g),
        pl.BlockSpec((seq_len, _LANES), lambda b, g: (rb0 + b, g)),
        vec(gw, lambda g: g, 4), vec(gw, lambda g: g),
        vec(SSD_STATE, lambda g: di // SSD_STATE + g, 4), vec(SSD_STATE, lambda g: di // SSD_STATE + g),
        vec(SSD_STATE, lambda g: di // SSD_STATE + SSD_GROUPS + g, 4),
        vec(SSD_STATE, lambda g: di // SSD_STATE + SSD_GROUPS + g),
        pl.BlockSpec((1, 1, _LANES), lambda b, g: (g, 0, 0)),
        pl.BlockSpec((1, 1, _LANES), lambda b, g: (g, 0, 0)),
        vec(gw, lambda g: g),
        vec(gw, lambda g: g),
        pl.BlockSpec((2, _LANES, gw), lambda b, g: (0, 0, 0)),
    ]
    args = [proj, proj, proj, proj, dtp, w["conv_w"], w["conv_b"], w["conv_w"], w["conv_b"],
            w["conv_w"], w["conv_b"], w["dt_bias"], w["a_log"], w["d_skip"], w["ng"], w["expand"]]
    st_spec = pl.BlockSpec((1, 2, 1, gw, SSD_STATE), lambda b, g: (b, 0, g, 0, 0))
    if latent:
        in_specs.append(st_spec)
        args.append(s0)
    y_spec = pl.BlockSpec((seq_len, gw), lambda b, g: (rb0 + b, g))
    y_shape = jax.ShapeDtypeStruct((tok.rows, di), BF16)
    aliases = {}
    if latent:
        out_specs, out_shape = [y_spec], [y_shape]
        aliases[len(args)] = 0
        spec, arg = _in_place([y_buf])
        in_specs += spec
        args += arg
    else:
        out_specs = [y_spec, st_spec]
        out_shape = [y_shape, jax.ShapeDtypeStruct((nseq, 2, SSD_GROUPS, gw, SSD_STATE), F32)]

    return pl.pallas_call(
        functools.partial(_ssd_kernel, seq_len=seq_len, zero_init=not latent, emit_state=not latent,
                          n_alias=len(aliases)),
        grid=(nseq, SSD_GROUPS),
        in_specs=in_specs,
        out_specs=out_specs,
        out_shape=out_shape,
        input_output_aliases=aliases,
        scratch_shapes=[
            pltpu.VMEM((2, seq_len, gw), F32),
            pltpu.VMEM((2, seq_len, gw), F32),
            pltpu.VMEM((2, seq_len // CHUNK, 2 * gw // SSD_HEADDIM, CHUNK), F32),
            pltpu.VMEM((seq_len, SSD_STATE), BF16),
            pltpu.VMEM((seq_len, SSD_STATE), BF16),
            pltpu.VMEM((seq_len, gw), F32),
            pltpu.VMEM((2, SSD_STATE, gw), F32),
        ],
        compiler_params=_cparams("arbitrary", "arbitrary"),
        name="ssd_scan_latent" if latent else "ssd_scan_prompt",
    )(*args)


def _lru_kernel(*refs, seq_len, zero_init, emit_state, n_alias):
    it = iter(refs)
    (x_ref, gate_ref, cw_ref, cb_ref, wa_ref, ba_ref, wi_ref, bi_ref, lam_ref) = (next(it) for _ in range(9))
    s0_ref = None if zero_init else next(it)
    for _ in range(n_alias):
        next(it)
    y_ref = next(it)
    sout_ref = next(it) if emit_state else None
    a_scr, u_scr, h_scr = next(it), next(it), next(it)
    width = a_scr.shape[2]

    blk = 256
    for r0 in range(0, seq_len, blk):
        xc = _conv4_block(x_ref, cw_ref, cb_ref, r0, blk, seq_len)
        xc_bf = xc.astype(BF16)
        for e in range(2):
            rg = _sigmoid(_dot(xc_bf, wa_ref[e, 0]) + ba_ref[e])
            ig = _sigmoid(_dot(xc_bf, wi_ref[e, 0]) + bi_ref[e])
            log_a = -LRU_C * rg * _softplus(-lam_ref[e])
            a_scr[e, r0:r0 + blk, :] = jnp.exp(log_a)
            u_scr[e, r0:r0 + blk, :] = jnp.sqrt(_neg_expm1_2x(log_a)) * ig * xc

    nt = seq_len // _SUBLANES
    row = lax.broadcasted_iota(jnp.int32, (_SUBLANES, width), 0)

    def tile_scan(e, t, carry):
        r0 = pl.multiple_of(t * _SUBLANES, _SUBLANES)
        a = a_scr[e, pl.ds(r0, _SUBLANES), :]
        u = u_scr[e, pl.ds(r0, _SUBLANES), :]
        for s in (1, 2, 4):
            if e == 0:
                valid = row >= s
                a_sh, u_sh = pltpu.roll(a, s, 0), pltpu.roll(u, s, 0)
            else:
                valid = row < _SUBLANES - s
                a_sh, u_sh = pltpu.roll(a, _SUBLANES - s, 0), pltpu.roll(u, _SUBLANES - s, 0)
            u = u + a * jnp.where(valid, u_sh, 0.0)
            a = a * jnp.where(valid, a_sh, 1.0)
        hcur = u + a * carry
        last = _SUBLANES - 1 if e == 0 else 0
        return hcur, hcur[last:last + 1, :]

    def body(i, carry):
        cf, cbk = carry
        hf, cf = tile_scan(0, i, cf)
        r0 = pl.multiple_of(i * _SUBLANES, _SUBLANES)
        h_scr[0, pl.ds(r0, _SUBLANES), :] = hf
        tb = nt - 1 - i
        hb, cbk = tile_scan(1, tb, cbk)
        rb = pl.multiple_of(tb * _SUBLANES, _SUBLANES)
        h_scr[1, pl.ds(rb, _SUBLANES), :] = hb
        return cf, cbk

    if zero_init:
        init = (jnp.zeros((1, width), F32), jnp.zeros((1, width), F32))
    else:
        init = (s0_ref[0, 0], s0_ref[0, 1])
    cf, cbk = lax.fori_loop(0, nt, body, init, unroll=4)

    for r0 in range(0, seq_len, blk):
        hsum = h_scr[0, r0:r0 + blk, :] + h_scr[1, r0:r0 + blk, :]
        y_ref[r0:r0 + blk, :] = (hsum * _gelu_tanh(gate_ref[r0:r0 + blk, :])).astype(y_ref.dtype)

    if emit_state:
        sout_ref[0, 0] = cf
        sout_ref[0, 1] = cbk


def _lru_scan(proj, w, s0, tok, *, latent, y_buf=None):
    width = proj.shape[1] // 2
    bw = width // LRU_BLOCKS
    if latent:
        nseq, seq_len, rb0 = tok.bs, tok.ls, tok.np_rows // tok.ls
    else:
        nseq, seq_len, rb0 = tok.bp, tok.lp, 0
    pair = pl.BlockSpec((2, 1, bw), lambda b, n: (0, 0, n))
    in_specs = [
        pl.BlockSpec((seq_len, bw), lambda b, n: (rb0 + b, n)),
        pl.BlockSpec((seq_len, bw), lambda b, n: (rb0 + b, LRU_BLOCKS + n)),
        pl.BlockSpec((4, bw), lambda b, n: (0, n)),
        pl.BlockSpec((1, bw), lambda b, n: (0, n)),
        pl.BlockSpec((2, 1, bw, bw), lambda b, n: (0, n, 0, 0)), pair,
        pl.BlockSpec((2, 1, bw, bw), lambda b, n: (0, n, 0, 0)), pair,
        pair,
    ]
    args = [proj, proj, w["conv_w"], w["conv_b"], w["w_a"], w["b_a"], w["w_i"], w["b_i"], w["lam"]]
    st_spec = pl.BlockSpec((1, 2, 1, bw), lambda b, n: (b, 0, 0, n))
    if latent:
        in_specs.append(st_spec)
        args.append(s0)
    y_spec = pl.BlockSpec((seq_len, bw), lambda b, n: (rb0 + b, n))
    y_shape = jax.ShapeDtypeStruct((tok.rows, width), BF16)
    aliases = {}
    if latent:
        out_specs, out_shape = [y_spec], [y_shape]
        aliases[len(args)] = 0
        spec, arg = _in_place([y_buf])
        in_specs += spec
        args += arg
    else:
        out_specs = [y_spec, st_spec]
        out_shape = [y_shape, jax.ShapeDtypeStruct((nseq, 2, 1, width), F32)]
    return pl.pallas_call(
        functools.partial(_lru_kernel, seq_len=seq_len, zero_init=not latent, emit_state=not latent,
                          n_alias=len(aliases)),
        grid=(nseq, LRU_BLOCKS),
        in_specs=in_specs,
        out_specs=out_specs,
        out_shape=out_shape,
        input_output_aliases=aliases,
        scratch_shapes=[
            pltpu.VMEM((2, seq_len, bw), F32),
            pltpu.VMEM((2, seq_len, bw), F32),
            pltpu.VMEM((2, seq_len, bw), F32),
        ],
        compiler_params=_cparams("arbitrary", "arbitrary"),
        name="lru_scan_latent" if latent else "lru_scan_prompt",
    )(*args)


def _prep_gla(w_g1, w_g2, b_g, norm_g):
    qkw = w_g2.shape[2]
    g1 = jnp.concatenate([w_g1[0], w_g1[1]], axis=1)
    g1 = jnp.pad(g1, ((0, 0), (0, _LANES - 2 * GLA_RANK)))
    w2 = jnp.zeros((2, _LANES, qkw), F32)
    w2 = w2.at[0, 0:GLA_RANK].set(w_g2[0]).at[1, GLA_RANK:2 * GLA_RANK].set(w_g2[1])
    return {
        "w_g1cat": g1.astype(BF16), "w2": w2.astype(BF16),
        "bg": b_g.reshape(2, 1, qkw), "ng": norm_g.reshape(1, -1),
    }


def _prep_ssd(w_in_dt, conv_w, conv_b, a_log, dt_bias, d_skip, norm_g):
    heads = a_log.shape[1]
    di = heads * SSD_HEADDIM
    hpg = heads // SSD_GROUPS
    gw = di // SSD_GROUPS
    main = 2 * di + 2 * SSD_GROUPS * SSD_STATE

    ncopy = 3
    pad = _LANES - ncopy * 2 * hpg

    def by_group(t):
        t = t.reshape(2, SSD_GROUPS, hpg).transpose(1, 0, 2).reshape(SSD_GROUPS, 1, 2 * hpg)
        return jnp.pad(jnp.tile(t, (1, 1, ncopy)), ((0, 0), (0, 0), (0, pad)))

    w_dt = w_in_dt.reshape(-1, 2, SSD_GROUPS, hpg).transpose(0, 2, 1, 3)
    w_dt = jnp.tile(w_dt.reshape(-1, SSD_GROUPS, 2 * hpg), (1, 1, ncopy))
    w_dt = jnp.pad(w_dt, ((0, 0), (0, 0), (0, pad))).reshape(-1, SSD_GROUPS * _LANES)
    lane = jnp.arange(_LANES)[:, None]
    chan = jnp.arange(gw)[None, :] // SSD_HEADDIM
    expand = jnp.stack([(lane % (2 * hpg) == e * hpg + chan) & (lane < ncopy * 2 * hpg)
                        for e in range(2)]).astype(BF16)
    return {
        "di": di, "w_dt": w_dt.astype(BF16), "n_main": main,
        "conv_w": conv_w, "conv_b": conv_b.reshape(1, -1),
        "dt_bias": by_group(dt_bias), "a_log": by_group(a_log),
        "d_skip": jnp.repeat(d_skip, SSD_HEADDIM).reshape(1, di), "ng": norm_g.reshape(1, di),
        "expand": expand,
    }


def _prep_lru(conv_w, conv_b, w_a, b_a, w_i, b_i, lam):
    width = conv_w.shape[1]
    return {
        "conv_w": conv_w, "conv_b": conv_b.reshape(1, width),
        "w_a": w_a.astype(BF16), "b_a": b_a.reshape(2, 1, width),
        "w_i": w_i.astype(BF16), "b_i": b_i.reshape(2, 1, width), "lam": lam.reshape(2, 1, width),
    }


def kernel(x_prompt, x_sample, state_gla, state_ssd, state_lru, c, c_ctx, w_mod, b_mod, norm_mix_g,
           norm_ffn_g, ffn_w_up, ffn_conv_w, ffn_conv_b, ffn_w_down, final_norm_g, gla_w_in, gla_w_g1,
           gla_w_g2, gla_b_g, gla_norm_g, gla_w_out, ssd_w_in, ssd_conv_w, ssd_conv_b, ssd_a_log,
           ssd_dt_bias, ssd_d, ssd_norm_g, ssd_w_out, lru_w_in, lru_conv_w, lru_conv_b, lru_w_a, lru_b_a,
           lru_w_i, lru_b_i, lru_lambda, lru_w_out):
    bp, lp, d = x_prompt.shape
    bs, ls, _ = x_sample.shape
    depth = w_mod.shape[0]
    tok = _Tokens(bp, lp, bs, ls)
    assert bs + 1 <= MOD_ROWS and ls // GRID_W * GRID_W == ls

    x = jnp.concatenate([x_prompt.reshape(bp * lp, d), x_sample.reshape(bs * ls, d)], axis=0)
    c_rows = jnp.concatenate([c_ctx[None], c, jnp.zeros((MOD_ROWS - 1 - bs, d), F32)], axis=0)
    mod = _mod_table(c_rows, w_mod, b_mod).reshape(depth * MOD_ROWS, MOD_CHUNKS, d)

    ffn_w_down_bf = ffn_w_down.astype(BF16)
    ssd_w_out_bf = ssd_w_out.astype(BF16)
    gla_states, ssd_new, lru_new = None, [], []
    for l in range(depth):
        kind, j = l % 3, l // 3
        h = _norm_mod(x, norm_mix_g[l], mod, l, tok, shift_idx=0, scale_idx=1)
        if kind == 0:
            w = _prep_gla(gla_w_g1[j], gla_w_g2[j], gla_b_g[j], gla_norm_g[j])
            y, gla_states = _gla_mixer(h, state_gla[:, j], gla_w_in, j, w, tok, gla_states)
            w_out = gla_w_out
        elif kind == 1:
            n_main = ssd_conv_w.shape[2] + ssd_d.shape[1] * SSD_HEADDIM
            w = _prep_ssd(ssd_w_in[j, :, n_main:], ssd_conv_w[j], ssd_conv_b[j], ssd_a_log[j],
                          ssd_dt_bias[j], ssd_d[j], ssd_norm_g[j])
            proj = _matmul(h, ssd_w_in, j, col_start=0, n_cols=n_main, bm=1024, bn=1024,
                           name="ssd_in_main")
            dtp = _matmul(h, w["w_dt"][None], 0, bm=1024, bn=SSD_GROUPS * _LANES, name="ssd_in_dt")
            gw = w["di"] // SSD_GROUPS
            s_lat0 = state_ssd[:, j].reshape(bs, 2, SSD_GROUPS, gw, SSD_STATE)
            y, s_new = _ssd_scan(proj, dtp, w, None, tok, latent=False)
            (y,) = _ssd_scan(proj, dtp, w, s_lat0, tok, latent=True, y_buf=y)
            ssd_new.append(s_new.reshape(bp, 2, -1, SSD_HEADDIM, SSD_STATE))
            w_out = ssd_w_out_bf
        else:
            w = _prep_lru(lru_conv_w[j], lru_conv_b[j], lru_w_a[j], lru_b_a[j],
                          lru_w_i[j], lru_b_i[j], lru_lambda[j])
            proj = _matmul(h, lru_w_in, j, bm=1024, bn=1024, name="lru_in")
            s_lat0 = state_lru[:, j].reshape(bs, 2, 1, -1)
            y, s_new = _lru_scan(proj, w, None, tok, latent=False)
            (y,) = _lru_scan(proj, w, s_lat0, tok, latent=True, y_buf=y)
            lru_new.append(s_new.reshape(bp, 2, -1))
            w_out = lru_w_out
        x = _matmul(y, w_out, j, bm=1024, bn=1024, bk=min(w_out.shape[1], 2048),
                    residual=(x, mod, l, tok, 2), name="mixer_out")
        h = _norm_mod(x, norm_ffn_g[l], mod, l, tok, shift_idx=3, scale_idx=4)
        g = _ffn_up(h, ffn_w_up, ffn_conv_w, ffn_conv_b, l, tok)
        x = _matmul(g, ffn_w_down_bf, l, bm=1024, bn=1024, bk=ffn_w_down.shape[1] // 2,
                    residual=(x, mod, l, tok, 5), name="ffn_down")
    y_prompt, y_sample = _final_norm(x, final_norm_g, tok)
    return (y_prompt.reshape(bp, lp, d), y_sample.reshape(bs, ls, d), gla_states,
            jnp.stack(ssd_new, axis=1), jnp.stack(lru_new, axis=1))
```

```python
import functools

import jax
import jax.numpy as jnp
from jax import lax
from jax.experimental import pallas as pl
from jax.experimental.pallas import tpu as pltpu

F32 = jnp.float32
BF16 = jnp.bfloat16

_V7X_VMEM_BYTES = 64 * 1024 * 1024
_VMEM_LIMIT = _V7X_VMEM_BYTES - 8 * 1024 * 1024
_SUBLANES = 8
_LANES = 128

NORM_EPS = 1e-6
MOD_CHUNKS = 6
MOD_ROWS = 8
GRID_W = 64

GLA_HEADS = 4
GLA_RANK = 16
GLA_GATE_NORM = 16.0
SSD_HEADDIM = 64
SSD_GROUPS = 8
SSD_STATE = 128
LRU_BLOCKS = 8
LRU_C = 8.0
CHUNK = 64


def _cparams(*sem):
    return pltpu.CompilerParams(dimension_semantics=sem, vmem_limit_bytes=_VMEM_LIMIT)


def _sigmoid(x):
    return 0.5 + 0.5 * jnp.tanh(0.5 * x)


def _silu_of_half(h):
    return h + h * jnp.tanh(h)


def _silu(x):
    return _silu_of_half(0.5 * x)


def _softplus(x):
    return jnp.maximum(x, 0.0) + jnp.log1p(jnp.exp(-jnp.abs(x)))


def _log_sigmoid(x):
    return jnp.minimum(x, 0.0) - jnp.log1p(jnp.exp(-jnp.abs(x)))


def _gelu_tanh(x):
    return 0.5 * x * (1.0 + jnp.tanh(0.7978845608028654 * (x + 0.044715 * (x * x * x))))


def _neg_expm1_2x(x):
    t = jnp.tanh(x)
    return -2.0 * t / (1.0 - t)


def _split3(x):
    x1 = x.astype(BF16)
    r1 = x - x1.astype(F32)
    x2 = r1.astype(BF16)
    x3 = (r1 - x2.astype(F32)).astype(BF16)
    return x1, x2, x3


def _dot(a, b):
    return jnp.dot(a, b, preferred_element_type=F32)


def _dot_nt(a, b):
    return lax.dot_general(a, b, (((1,), (1,)), ((), ())), preferred_element_type=F32)


def _dot_tn(a, b):
    return lax.dot_general(a, b, (((0,), (0,)), ((), ())), preferred_element_type=F32)


def _exact_dot_left01(m01, x):
    x1, x2, x3 = _split3(x)
    return _dot(m01, x1) + _dot(m01, x2) + _dot(m01, x3)


def _exact_select_copies(x, sel01, ncopy_lanes):
    t1 = x.astype(BF16).astype(F32)
    r1 = x - t1
    t2 = r1.astype(BF16).astype(F32)
    lane = lax.broadcasted_iota(jnp.int32, x.shape, 1)
    terms = jnp.where(lane < ncopy_lanes, t1, jnp.where(lane < 2 * ncopy_lanes, t2, r1 - t2))
    return _dot(terms.astype(BF16), sel01)


def _chunk_masks():
    ii = lax.broadcasted_iota(jnp.int32, (CHUNK, CHUNK), 0)
    jj = lax.broadcasted_iota(jnp.int32, (CHUNK, CHUNK), 1)
    masks = (jj <= ii, jj >= ii)
    tris = tuple(jnp.where(m, 1.0, 0.0).astype(BF16) for m in masks)
    return masks, tris


def _block_chunk_masks(rows):
    ii = lax.broadcasted_iota(jnp.int32, (rows, rows), 0)
    jj = lax.broadcasted_iota(jnp.int32, (rows, rows), 1)
    shift = CHUNK.bit_length() - 1
    same = jnp.where(jnp.right_shift(ii, shift) == jnp.right_shift(jj, shift), 1, 0)
    masks = (same * jnp.where(jj <= ii, 1, 0) > 0, same * jnp.where(jj >= ii, 1, 0) > 0)
    tris = tuple(jnp.where(m, 1.0, 0.0).astype(BF16) for m in masks)
    return masks, tris


def _shifted_rows(ref, r0, nrows, off, total):
    lo, hi = r0 + off, r0 + off + nrows
    clo, chi = max(lo, 0), min(hi, total)
    x = ref[clo:chi, :]
    parts = []
    if clo > lo:
        parts.append(jnp.zeros((clo - lo, x.shape[1]), x.dtype))
    parts.append(x)
    if hi > chi:
        parts.append(jnp.zeros((hi - chi, x.shape[1]), x.dtype))
    return x if len(parts) == 1 else jnp.concatenate(parts, axis=0)


def _conv4_block(ref, w_ref, b_ref, r0, nrows, total, scale=1.0):
    w, b = w_ref[...], b_ref[...]
    if scale != 1.0:
        w, b = scale * w, scale * b
    acc = b + w[0:1, :] * _shifted_rows(ref, r0, nrows, -1, total)
    acc = acc + w[1:2, :] * ref[r0:r0 + nrows, :]
    acc = acc + w[2:3, :] * _shifted_rows(ref, r0, nrows, 1, total)
    acc = acc + w[3:4, :] * _shifted_rows(ref, r0, nrows, 2, total)
    return acc


def _mod_kernel(c_ref, w_ref, b_ref, o_ref):
    c = c_ref[...]
    s = _silu(c).astype(BF16)
    o_ref[0] = _dot(s, w_ref[0].astype(BF16)) + b_ref[0]


def _mod_table(c_rows, w_mod, b_mod):
    depth, d, n = w_mod.shape
    bn = 1024
    return pl.pallas_call(
        _mod_kernel,
        grid=(depth, n // bn),
        in_specs=[
            pl.BlockSpec((MOD_ROWS, d), lambda l, j: (0, 0)),
            pl.BlockSpec((1, d, bn), lambda l, j: (l, 0, j)),
            pl.BlockSpec((1, 1, bn), lambda l, j: (l, 0, j)),
        ],
        out_specs=pl.BlockSpec((1, MOD_ROWS, bn), lambda l, j: (l, 0, j)),
        out_shape=jax.ShapeDtypeStruct((depth, MOD_ROWS, n), F32),
        compiler_params=_cparams("arbitrary", "arbitrary"),
        name="adaln_table",
    )(c_rows, w_mod, b_mod.reshape(depth, 1, n))


class _Tokens:
    def __init__(self, n_prompt, l_prompt, n_latent, l_latent):
        self.bp, self.lp, self.bs, self.ls = n_prompt, l_prompt, n_latent, l_latent
        self.np_rows = n_prompt * l_prompt
        self.ns_rows = n_latent * l_latent
        self.rows = self.np_rows + self.ns_rows

    def mod_row(self, i, bm):
        r0 = i * bm
        return jnp.where(r0 < self.np_rows, 0, 1 + (r0 - self.np_rows) // self.ls)


_NORM_ROWS = 2 * _SUBLANES


def _rmsnorm_rows(x_ref, g_ref, r0):
    x = x_ref[pl.ds(r0, _NORM_ROWS), :]
    ms = jnp.mean(x * x, axis=-1, keepdims=True)
    return x * lax.rsqrt(ms + NORM_EPS) * g_ref[...]


def _norm_mod_kernel(x_ref, g_ref, mod_ref, h_ref, *, shift_idx, scale_idx):
    def body(c, carry):
        r0 = pl.multiple_of(c * _NORM_ROWS, _NORM_ROWS)
        y = _rmsnorm_rows(x_ref, g_ref, r0)
        h = y * (1.0 + mod_ref[0, scale_idx:scale_idx + 1, :]) + mod_ref[0, shift_idx:shift_idx + 1, :]
        h_ref[pl.ds(r0, _NORM_ROWS), :] = h.astype(h_ref.dtype)
        return carry

    lax.fori_loop(0, x_ref.shape[0] // _NORM_ROWS, body, 0, unroll=8)


def _norm_mod(x, g, mod, layer, tok, *, shift_idx, scale_idx):
    rows, d = x.shape
    bm = 1024
    return pl.pallas_call(
        functools.partial(_norm_mod_kernel, shift_idx=shift_idx, scale_idx=scale_idx),
        grid=(rows // bm,),
        in_specs=[
            pl.BlockSpec((bm, d), lambda i: (i, 0)),
            pl.BlockSpec((1, d), lambda i: (0, 0)),
            pl.BlockSpec((1, MOD_CHUNKS, d), lambda i: (layer * MOD_ROWS + tok.mod_row(i, bm), 0, 0)),
        ],
        out_specs=pl.BlockSpec((bm, d), lambda i: (i, 0)),
        out_shape=jax.ShapeDtypeStruct((rows, d), BF16),
        compiler_params=_cparams("arbitrary"),
        name="norm_adaln",
    )(x, g.reshape(1, d), mod)


def _final_norm_kernel(x_ref, g_ref, op_ref, os_ref, *, n_prompt_tiles):
    def norm_into(o_ref):
        def body(c, carry):
            r0 = pl.multiple_of(c * _NORM_ROWS, _NORM_ROWS)
            o_ref[pl.ds(r0, _NORM_ROWS), :] = _rmsnorm_rows(x_ref, g_ref, r0)
            return carry

        lax.fori_loop(0, x_ref.shape[0] // _NORM_ROWS, body, 0, unroll=8)

    @pl.when(pl.program_id(0) < n_prompt_tiles)
    def _():
        norm_into(op_ref)

    @pl.when(pl.program_id(0) >= n_prompt_tiles)
    def _():
        norm_into(os_ref)


def _final_norm(x, g, tok):
    rows, d = x.shape
    bm = 1024
    npt = tok.np_rows // bm
    return pl.pallas_call(
        functools.partial(_final_norm_kernel, n_prompt_tiles=npt),
        grid=(rows // bm,),
        in_specs=[pl.BlockSpec((bm, d), lambda i: (i, 0)), pl.BlockSpec((1, d), lambda i: (0, 0))],
        out_specs=[pl.BlockSpec((bm, d), lambda i: (jnp.minimum(i, npt - 1), 0)),
                   pl.BlockSpec((bm, d), lambda i: (jnp.maximum(i - npt, 0), 0))],
        out_shape=[jax.ShapeDtypeStruct((tok.np_rows, d), F32),
                   jax.ShapeDtypeStruct((tok.ns_rows, d), F32)],
        compiler_params=_cparams("arbitrary"),
        name="final_norm",
    )(x, g.reshape(1, d))


def _mm_kernel(*refs, nk, gate_idx):
    if gate_idx is None:
        x_ref, w_ref, o_ref, acc_ref, wb_ref = refs
        res_ref = mod_ref = None
    else:
        x_ref, w_ref, res_ref, mod_ref, o_ref, acc_ref, wb_ref = refs
    i, k = pl.program_id(1), pl.program_id(2)
    if w_ref.dtype == BF16:
        w = w_ref[0]
    elif nk == 1:
        @pl.when(i == 0)
        def _():
            wb_ref[...] = w_ref[0].astype(BF16)

        w = wb_ref[...]
    else:
        w = w_ref[0].astype(BF16)
    part = _dot(x_ref[...], w)

    def finish(acc):
        if gate_idx is None:
            o_ref[...] = acc.astype(o_ref.dtype)
        else:
            o_ref[...] = res_ref[...] + mod_ref[0, gate_idx:gate_idx + 1, :] * acc

    if nk == 1:
        finish(part)
        return

    @pl.when(k == 0)
    def _():
        acc_ref[...] = part

    @pl.when(jnp.logical_and(k > 0, k < nk - 1))
    def _():
        acc_ref[...] += part

    @pl.when(k == nk - 1)
    def _():
        finish(acc_ref[...] + part)


def _matmul(x, w, layer, *, col_start=0, n_cols=None, bm, bn, bk=None, out_dtype=F32, residual=None,
            name="matmul"):
    m, kdim = x.shape
    n_cols = w.shape[2] - col_start if n_cols is None else n_cols
    bk = kdim if bk is None else bk
    nk = kdim // bk
    off = col_start // bn
    in_specs = [
        pl.BlockSpec((bm, bk), lambda j, i, k: (i, k)),
        pl.BlockSpec((1, bk, bn), lambda j, i, k: (layer, k, j + off)),
    ]
    args = [x, w]
    gate_idx = None
    if residual is not None:
        res, mod, mod_layer, tok, gate_idx = residual
        in_specs += [
            pl.BlockSpec((bm, bn), lambda j, i, k: (i, j)),
            pl.BlockSpec((1, MOD_CHUNKS, bn),
                         lambda j, i, k: (mod_layer * MOD_ROWS + tok.mod_row(i, bm), 0, j)),
        ]
        args += [res, mod]
    cast_once = w.dtype != BF16 and nk == 1
    tiny = (2 * _SUBLANES, _LANES)
    return pl.pallas_call(
        functools.partial(_mm_kernel, nk=nk, gate_idx=gate_idx),
        grid=(n_cols // bn, m // bm, nk),
        in_specs=in_specs,
        out_specs=pl.BlockSpec((bm, bn), lambda j, i, k: (i, j)),
        out_shape=jax.ShapeDtypeStruct((m, n_cols), out_dtype),
        scratch_shapes=[pltpu.VMEM((bm, bn) if nk > 1 else tiny, F32),
                        pltpu.VMEM((bk, bn) if cast_once else tiny, BF16)],
        compiler_params=_cparams("arbitrary", "arbitrary", "arbitrary"),
        name=name,
    )(*args)


def _ffn_up_kernel(h_ref, wa_ref, wv_ref, cw_ref, cb_ref, g_ref, *, n_prompt_tiles, w_prompt, w_latent):
    i = pl.program_id(0)
    x = h_ref[...]
    act = _dot(x, wa_ref[0].astype(BF16))
    val = _dot(x, wv_ref[0].astype(BF16))
    t, tn = act.shape
    row = lax.broadcasted_iota(jnp.int32, (t, tn), 0)
    prev = pltpu.roll(act, 1, 0)
    nxt = pltpu.roll(act, t - 1, 0)

    def horizontal(width):
        col = row & (width - 1)
        return jnp.where(col > 0, prev, 0.0), jnp.where(col < width - 1, nxt, 0.0)

    half_w = 0.5 * cw_ref[...]
    half_b = 0.5 * cb_ref[...]

    def tap_row(kh, left, right):
        return (half_w[3 * kh:3 * kh + 1, :] * left + half_w[3 * kh + 1:3 * kh + 2, :] * act
                + half_w[3 * kh + 2:3 * kh + 3, :] * right)

    def gate(half_conv):
        g_ref[...] = (_silu_of_half(half_conv) * val).astype(g_ref.dtype)

    @pl.when(i < n_prompt_tiles)
    def _():
        left, right = horizontal(w_prompt)
        gate(tap_row(1, left, right) + half_b)

    @pl.when(i >= n_prompt_tiles)
    def _():
        left, right = horizontal(w_latent)
        zeros = jnp.zeros((w_latent, tn), F32)
        above = jnp.concatenate([zeros, tap_row(0, left, right)[:t - w_latent, :]], axis=0)
        below = jnp.concatenate([tap_row(2, left, right)[w_latent:, :], zeros], axis=0)
        gate(above + tap_row(1, left, right) + below + half_b)


def _ffn_up(h, w_up, conv_w, conv_b, layer, tok):
    rows, d = h.shape
    f = w_up.shape[2] // 2
    tm = tok.ls
    tn = 256
    assert tok.np_rows % tm == 0 and tm % tok.lp == 0
    nj = f // tn
    return pl.pallas_call(
        functools.partial(_ffn_up_kernel, n_prompt_tiles=tok.np_rows // tm,
                          w_prompt=tok.lp, w_latent=GRID_W),
        grid=(rows // tm, nj),
        in_specs=[
            pl.BlockSpec((tm, d), lambda i, j: (i, 0)),
            pl.BlockSpec((1, d, tn), lambda i, j: (layer, 0, j)),
            pl.BlockSpec((1, d, tn), lambda i, j: (layer, 0, j + nj)),
            pl.BlockSpec((9, tn), lambda i, j: (0, j)),
            pl.BlockSpec((1, tn), lambda i, j: (0, j)),
        ],
        out_specs=pl.BlockSpec((tm, tn), lambda i, j: (i, j)),
        out_shape=jax.ShapeDtypeStruct((rows, f), BF16),
        compiler_params=_cparams("arbitrary", "arbitrary"),
        name="ffn_up_conv",
    )(h, w_up, w_up, conv_w[layer].reshape(9, f), conv_b[layer].reshape(1, f))


def _gla_kernel(*refs, seq_len, zero_init, emit_state, n_alias, dk, dv):
    it = iter(refs)
    q_ref, k_ref, v_ref, r_ref, low_ref, w2_ref, bg_ref, ng_ref = (next(it) for _ in range(8))
    s0_ref = None if zero_init else next(it)
    for _ in range(n_alias):
        next(it)
    y_ref = next(it)
    sout_ref = next(it).at[0, 0] if emit_state else None
    g_scr, o_scr, s_scr = next(it), next(it), next(it)

    nc = seq_len // CHUNK
    blk = 256
    cpb = blk // CHUNK
    nb = seq_len // blk
    masks, tris = _block_chunk_masks(blk)
    mid_row = (CHUNK // 2, CHUNK // 2 - 1)
    end_row = (CHUNK - 1, 0)
    q_scale = dk ** -0.5

    low = low_ref[...].astype(BF16)
    for e in range(2):
        z = _dot(low, w2_ref[e]) + bg_ref[e]
        g_scr[e] = _log_sigmoid(z) * (1.0 / GLA_GATE_NORM)
        if zero_init:
            s_scr[e] = jnp.zeros((dv, dk), F32)
        else:
            s_scr[e] = s0_ref[0, e, 0].T

    def block_step(e, b):
        r0 = pl.multiple_of(b * blk, blk)
        gc = _exact_dot_left01(tris[e], g_scr[e, pl.ds(r0, blk), :])
        g_scr[e, pl.ds(r0, blk), :] = gc
        g_mid = jnp.concatenate(
            [jnp.broadcast_to(gc[c * CHUNK + mid_row[e]:c * CHUNK + mid_row[e] + 1, :], (CHUNK, dk))
             for c in range(cpb)], axis=0)
        q = q_ref[pl.ds(r0, blk), :] * q_scale
        k = k_ref[pl.ds(r0, blk), :]
        scores = _dot_nt((q * jnp.exp(gc - g_mid)).astype(BF16), (k * jnp.exp(g_mid - gc)).astype(BF16))
        o_scr[e, pl.ds(r0, blk), :] = _dot(jnp.where(masks[e], scores, 0.0).astype(BF16),
                                            v_ref[pl.ds(r0, blk), :])

    def block_body(b, carry):
        block_step(0, b)
        block_step(1, b)
        return carry

    lax.fori_loop(0, nb, block_body, 0, unroll=min(2, nb))

    def chunk_step(e, c):
        r0 = pl.multiple_of(c * CHUNK, CHUNK)
        gc = g_scr[e, pl.ds(r0, CHUNK), :]
        g_end = gc[end_row[e]:end_row[e] + 1, :]
        q = q_ref[pl.ds(r0, CHUNK), :] * q_scale
        k = k_ref[pl.ds(r0, CHUNK), :]
        v = v_ref[pl.ds(r0, CHUNK), :]
        s_t = s_scr[e]
        o_scr[e, pl.ds(r0, CHUNK), :] += _dot_nt((q * jnp.exp(gc)).astype(BF16), s_t.astype(BF16))
        u_t = _dot_tn(v, (k * jnp.exp(g_end - gc)).astype(BF16))
        s_scr[e] = s_t * jnp.exp(g_end) + u_t

    def body(i, carry):
        chunk_step(0, i)
        chunk_step(1, nc - 1 - i)
        return carry

    lax.fori_loop(0, nc, body, 0, unroll=4)

    for r0 in range(0, seq_len, blk):
        o = o_scr[0, r0:r0 + blk, :] + o_scr[1, r0:r0 + blk, :]
        ms = jnp.mean(o * o, axis=-1, keepdims=True)
        y = o * lax.rsqrt(ms + NORM_EPS) * ng_ref[...]
        y_ref[r0:r0 + blk, :] = (y * _silu(r_ref[r0:r0 + blk, :])).astype(y_ref.dtype)

    if emit_state:
        for e in range(2):
            sout_ref[e, 0] = s_scr[e].T


def _in_place(buffers):
    return [pl.BlockSpec(memory_space=pl.ANY) for _ in buffers], list(buffers)


def _gla_scan(qk, v, r, low, w2, bg, ng, s0, tok, *, latent, y_buf=None, state_buf=None, layer=0,
              n_layers=1):
    dk = qk.shape[1] // (2 * GLA_HEADS)
    dv = v.shape[1] // GLA_HEADS
    if latent:
        nseq, seq_len, rb0 = tok.bs, tok.ls, tok.np_rows // tok.ls
    else:
        nseq, seq_len, rb0 = tok.bp, tok.lp, 0
    in_specs = [
        pl.BlockSpec((seq_len, dk), lambda b, h: (rb0 + b, h)),
        pl.BlockSpec((seq_len, dk), lambda b, h: (rb0 + b, GLA_HEADS + h)),
        pl.BlockSpec((seq_len, dv), lambda b, h: (rb0 + b, h)),
        pl.BlockSpec((seq_len, dv), lambda b, h: (rb0 + b, h)),
        pl.BlockSpec((seq_len, _LANES), lambda b, h: (rb0 + b, 0)),
        pl.BlockSpec((2, _LANES, dk), lambda b, h: (0, 0, h)),
        pl.BlockSpec((2, 1, dk), lambda b, h: (0, 0, h)),
        pl.BlockSpec((1, dv), lambda b, h: (0, 0)),
    ]
    args = [qk, qk, v, r, low, w2, bg, ng]
    if latent:
        in_specs.append(pl.BlockSpec((1, 2, 1, dk, dv), lambda b, h: (b, 0, h, 0, 0)))
        args.append(s0)
    y_spec = pl.BlockSpec((seq_len, dv), lambda b, h: (rb0 + b, h))
    y_shape = jax.ShapeDtypeStruct((tok.rows, v.shape[1]), BF16)
    if latent:
        out_specs, out_shape, bufs = [y_spec], [y_shape], [y_buf]
    else:
        out_specs = [y_spec, pl.BlockSpec((1, 1, 2, 1, dk, dv), lambda b, h: (b, layer, 0, h, 0, 0))]
        out_shape = [y_shape, jax.ShapeDtypeStruct((nseq, n_layers, 2, GLA_HEADS, dk, dv), F32)]
        bufs = [y_buf, state_buf]
    aliases = {}
    for o, buf in enumerate(bufs):
        if buf is not None:
            aliases[len(args)] = o
            spec, arg = _in_place([buf])
            in_specs += spec
            args += arg
    return pl.pallas_call(
        functools.partial(_gla_kernel, seq_len=seq_len, zero_init=not latent, emit_state=not latent,
                          n_alias=len(aliases), dk=dk, dv=dv),
        grid=(nseq, GLA_HEADS),
        in_specs=in_specs,
        out_specs=out_specs,
        out_shape=out_shape,
        input_output_aliases=aliases,
        scratch_shapes=[
            pltpu.VMEM((2, seq_len, dk), F32),
            pltpu.VMEM((2, seq_len, dv), F32),
            pltpu.VMEM((2, dv, dk), F32),
        ],
        compiler_params=_cparams("arbitrary", "arbitrary"),
        name="gla_scan_latent" if latent else "gla_scan_prompt",
    )(*args)


def _gla_mixer(h, s_lat0, w_in, layer, w, tok, state_buf):
    w_g1cat, w2, bg, ng = w["w_g1cat"], w["w2"], w["bg"], w["ng"]
    qkw = 2 * w2.shape[2]
    vw = (w_in.shape[2] - qkw) // 2
    qk = _matmul(h, w_in, layer, col_start=0, n_cols=qkw, bm=1024, bn=1024, name="gla_in_qk")
    v = _matmul(h, w_in, layer, col_start=qkw, n_cols=vw, bm=1024, bn=1024, out_dtype=BF16,
                name="gla_in_v")
    r = _matmul(h, w_in, layer, col_start=qkw + vw, n_cols=vw, bm=1024, bn=1024, name="gla_in_r")
    low = _matmul(h, w_g1cat[None], 0, bm=1024, bn=_LANES, name="gla_in_gate")
    y, states = _gla_scan(qk, v, r, low, w2, bg, ng, None, tok, latent=False, state_buf=state_buf,
                          layer=layer, n_layers=w_in.shape[0])
    (y,) = _gla_scan(qk, v, r, low, w2, bg, ng, s_lat0, tok, latent=True, y_buf=y)
    return y, states


def _ssd_kernel(*refs, seq_len, zero_init, emit_state, n_alias):
    it = iter(refs)
    (z_ref, x_ref, b_ref, c_ref, dt_ref, cwx_ref, cbx_ref, cwb_ref, cbb_ref, cwc_ref, cbc_ref,
     dtb_ref, alog_ref, dskip_ref, ng_ref, e_ref) = (next(it) for _ in range(16))
    s0_ref = None if zero_init else next(it)
    for _ in range(n_alias):
        next(it)
    y_ref = next(it)
    sout_ref = next(it) if emit_state else None
    xd_scr, cumx_scr, cumt_scr, bm_scr, cm_scr, y_scr, s_scr = (next(it) for _ in range(7))

    hpg = y_scr.shape[1] // SSD_HEADDIM
    nc = seq_len // CHUNK
    blk = 256
    cpb = blk // CHUNK
    masks, _ = _chunk_masks()
    _, blk_tris = _block_chunk_masks(blk)
    end_row = (CHUNK - 1, 0)
    a_row = -jnp.exp(alog_ref[0])

    for r0 in range(0, seq_len, blk):
        def conv_silu(ref, w_ref, bias_ref):
            return _silu_of_half(_conv4_block(ref, w_ref, bias_ref, r0, blk, seq_len, scale=0.5))

        xs = conv_silu(x_ref, cwx_ref, cbx_ref)
        y_scr[r0:r0 + blk, :] = dskip_ref[...] * xs
        bm_scr[r0:r0 + blk, :] = conv_silu(b_ref, cwb_ref, cbb_ref).astype(BF16)
        cm_scr[r0:r0 + blk, :] = conv_silu(c_ref, cwc_ref, cbc_ref).astype(BF16)
        dt = _softplus(dt_ref[r0:r0 + blk, :] + dtb_ref[0])
        for e in range(2):
            expand = e_ref[e]
            cum = _exact_dot_left01(blk_tris[e], dt * a_row)
            cum_t = cum.T
            for cc in range(cpb):
                cumt_scr[e, r0 // CHUNK + cc] = cum_t[0:2 * hpg, cc * CHUNK:(cc + 1) * CHUNK]
            cumx_scr[e, r0:r0 + blk, :] = _exact_select_copies(cum, expand, 2 * hpg)
            xd_scr[e, r0:r0 + blk, :] = xs * _exact_select_copies(dt, expand, 2 * hpg)

    for e in range(2):
        if zero_init:
            s_scr[e] = jnp.zeros(s_scr.shape[1:], F32)
        else:
            s_scr[e] = s0_ref[0, e, 0].T

    def chunk_step(e, c):
        r0 = pl.multiple_of(c * CHUNK, CHUNK)
        cum_x = cumx_scr[e, pl.ds(r0, CHUNK), :]
        cum_t = cumt_scr[e, c]
        xd = xd_scr[e, pl.ds(r0, CHUNK), :]
        xd_bf = xd.astype(BF16)
        bm = bm_scr[pl.ds(r0, CHUNK), :]
        cm = cm_scr[pl.ds(r0, CHUNK), :]
        cb = _dot_nt(cm, bm)
        parts = []
        for hh in range(hpg):
            lane = slice(hh * SSD_HEADDIM, (hh + 1) * SSD_HEADDIM)
            seg = cum_x[:, lane] - cum_t[e * hpg + hh:e * hpg + hh + 1, :]
            decay = jnp.where(masks[e], jnp.exp(jnp.where(masks[e], seg, 0.0)), 0.0)
            parts.append(_dot((cb * decay).astype(BF16), xd_bf[:, lane]))
        y = jnp.concatenate(parts, axis=1)
        s_t = s_scr[e]
        y = y + _dot(cm, s_t.astype(BF16)) * jnp.exp(cum_x)
        cum_end = cum_x[end_row[e]:end_row[e] + 1, :]
        st_t = _dot_tn(bm, (xd * jnp.exp(cum_end - cum_x)).astype(BF16))
        s_scr[e] = s_t * jnp.exp(cum_end) + st_t
        y_scr[pl.ds(r0, CHUNK), :] += y

    def body(i, carry):
        chunk_step(0, i)
        chunk_step(1, nc - 1 - i)
        return carry

    lax.fori_loop(0, nc, body, 0, unroll=4)

    for r0 in range(0, seq_len, blk):
        y = y_scr[r0:r0 + blk, :] * _silu(z_ref[r0:r0 + blk, :])
        ms = jnp.mean(y * y, axis=-1, keepdims=True)
        y_ref[r0:r0 + blk, :] = (y * lax.rsqrt(ms + NORM_EPS) * ng_ref[...]).astype(y_ref.dtype)

    if emit_state:
        for e in range(2):
            sout_ref[0, e, 0] = s_scr[e].T


def _ssd_scan(proj, dtp, w, s0, tok, *, latent, y_buf=None):
    di = w["di"]
    gw = di // SSD_GROUPS
    if latent:
        nseq, seq_len, rb0 = tok.bs, tok.ls, tok.np_rows // tok.ls
    else:
        nseq, seq_len, rb0 = tok.bp, tok.lp, 0
    nx = di // gw
    nb = (2 * di) // SSD_STATE
    ncb = nb + SSD_GROUPS

    def rows(width, col, single=False):
        mode = {"pipeline_mode": pl.Buffered(1)} if single and latent else {}
        return pl.BlockSpec((seq_len, width), lambda b, g: (rb0 + b, col(g)), **mode)

    def vec(width, col, nrows=1):
        return pl.BlockSpec((nrows, width), lambda b, g: (0, col(g)))

    in_specs = [
        rows(gw, lambda g: g),
        rows(gw, lambda g: nx + g),
        rows(SSD_STATE, lambda g: nb + g),
        rows(SSD_STATE, lambda g: ncb + g),
        pl.BlockSpec((seq_len, _LANES), lambda b, g: (rb0 + b, g)),
        vec(gw, lambda g: g, 4), vec(gw, lambda g: g),
        vec(SSD_STATE, lambda g: di // SSD_STATE + g, 4), vec(SSD_STATE, lambda g: di // SSD_STATE + g),
        vec(SSD_STATE, lambda g: di // SSD_STATE + SSD_GROUPS + g, 4),
        vec(SSD_STATE, lambda g: di // SSD_STATE + SSD_GROUPS + g),
        pl.BlockSpec((1, 1, _LANES), lambda b, g: (g, 0, 0)),
        pl.BlockSpec((1, 1, _LANES), lambda b, g: (g, 0, 0)),
        vec(gw, lambda g: g),
        vec(gw, lambda g: g),
        pl.BlockSpec((2, _LANES, gw), lambda b, g: (0, 0, 0)),
    ]
    args = [proj, proj, proj, proj, dtp, w["conv_w"], w["conv_b"], w["conv_w"], w["conv_b"],
            w["conv_w"], w["conv_b"], w["dt_bias"], w["a_log"], w["d_skip"], w["ng"], w["expand"]]
    st_spec = pl.BlockSpec((1, 2, 1, gw, SSD_STATE), lambda b, g: (b, 0, g, 0, 0))
    if latent:
        in_specs.append(st_spec)
        args.append(s0)
    y_spec = pl.BlockSpec((seq_len, gw), lambda b, g: (rb0 + b, g))
    y_shape = jax.ShapeDtypeStruct((tok.rows, di), BF16)
    aliases = {}
    if latent:
        out_specs, out_shape = [y_spec], [y_shape]
        aliases[len(args)] = 0
        spec, arg = _in_place([y_buf])
        in_specs += spec
        args += arg
    else:
        out_specs = [y_spec, st_spec]
        out_shape = [y_shape, jax.ShapeDtypeStruct((nseq, 2, SSD_GROUPS, gw, SSD_STATE), F32)]

    return pl.pallas_call(
        functools.partial(_ssd_kernel, seq_len=seq_len, zero_init=not latent, emit_state=not latent,
                          n_alias=len(aliases)),
        grid=(nseq, SSD_GROUPS),
        in_specs=in_specs,
        out_specs=out_specs,
        out_shape=out_shape,
        input_output_aliases=aliases,
        scratch_shapes=[
            pltpu.VMEM((2, seq_len, gw), F32),
            pltpu.VMEM((2, seq_len, gw), F32),
            pltpu.VMEM((2, seq_len // CHUNK, 2 * gw // SSD_HEADDIM, CHUNK), F32),
            pltpu.VMEM((seq_len, SSD_STATE), BF16),
            pltpu.VMEM((seq_len, SSD_STATE), BF16),
            pltpu.VMEM((seq_len, gw), F32),
            pltpu.VMEM((2, SSD_STATE, gw), F32),
        ],
        compiler_params=_cparams("arbitrary", "arbitrary"),
        name="ssd_scan_latent" if latent else "ssd_scan_prompt",
    )(*args)


def _lru_kernel(*refs, seq_len, zero_init, emit_state, n_alias):
    it = iter(refs)
    (x_ref, gate_ref, cw_ref, cb_ref, wa_ref, ba_ref, wi_ref, bi_ref, lam_ref) = (next(it) for _ in range(9))
    s0_ref = None if zero_init else next(it)
    for _ in range(n_alias):
        next(it)
    y_ref = next(it)
    sout_ref = next(it) if emit_state else None
    a_scr, u_scr, h_scr = next(it), next(it), next(it)
    width = a_scr.shape[2]

    blk = 256
    for r0 in range(0, seq_len, blk):
        xc = _conv4_block(x_ref, cw_ref, cb_ref, r0, blk, seq_len)
        xc_bf = xc.astype(BF16)
        for e in range(2):
            rg = _sigmoid(_dot(xc_bf, wa_ref[e, 0]) + ba_ref[e])
            ig = _sigmoid(_dot(xc_bf, wi_ref[e, 0]) + bi_ref[e])
            log_a = -LRU_C * rg * _softplus(-lam_ref[e])
            a_scr[e, r0:r0 + blk, :] = jnp.exp(log_a)
            u_scr[e, r0:r0 + blk, :] = jnp.sqrt(_neg_expm1_2x(log_a)) * ig * xc

    nt = seq_len // _SUBLANES
    row = lax.broadcasted_iota(jnp.int32, (_SUBLANES, width), 0)

    def tile_scan(e, t, carry):
        r0 = pl.multiple_of(t * _SUBLANES, _SUBLANES)
        a = a_scr[e, pl.ds(r0, _SUBLANES), :]
        u = u_scr[e, pl.ds(r0, _SUBLANES), :]
        for s in (1, 2, 4):
            if e == 0:
                valid = row >= s
                a_sh, u_sh = pltpu.roll(a, s, 0), pltpu.roll(u, s, 0)
            else:
                valid = row < _SUBLANES - s
                a_sh, u_sh = pltpu.roll(a, _SUBLANES - s, 0), pltpu.roll(u, _SUBLANES - s, 0)
            u = u + a * jnp.where(valid, u_sh, 0.0)
            a = a * jnp.where(valid, a_sh, 1.0)
        hcur = u + a * carry
        last = _SUBLANES - 1 if e == 0 else 0
        return hcur, hcur[last:last + 1, :]

    def body(i, carry):
        cf, cbk = carry
        hf, cf = tile_scan(0, i, cf)
        r0 = pl.multiple_of(i * _SUBLANES, _SUBLANES)
        h_scr[0, pl.ds(r0, _SUBLANES), :] = hf
        tb = nt - 1 - i
        hb, cbk = tile_scan(1, tb, cbk)
        rb = pl.multiple_of(tb * _SUBLANES, _SUBLANES)
        h_scr[1, pl.ds(rb, _SUBLANES), :] = hb
        return cf, cbk

    if zero_init:
        init = (jnp.zeros((1, width), F32), jnp.zeros((1, width), F32))
    else:
        init = (s0_ref[0, 0], s0_ref[0, 1])
    cf, cbk = lax.fori_loop(0, nt, body, init, unroll=4)

    for r0 in range(0, seq_len, blk):
        hsum = h_scr[0, r0:r0 + blk, :] + h_scr[1, r0:r0 + blk, :]
        y_ref[r0:r0 + blk, :] = (hsum * _gelu_tanh(gate_ref[r0:r0 + blk, :])).astype(y_ref.dtype)

    if emit_state:
        sout_ref[0, 0] = cf
        sout_ref[0, 1] = cbk


def _lru_scan(proj, w, s0, tok, *, latent, y_buf=None):
    width = proj.shape[1] // 2
    bw = width // LRU_BLOCKS
    if latent:
        nseq, seq_len, rb0 = tok.bs, tok.ls, tok.np_rows // tok.ls
    else:
        nseq, seq_len, rb0 = tok.bp, tok.lp, 0
    pair = pl.BlockSpec((2, 1, bw), lambda b, n: (0, 0, n))
    in_specs = [
        pl.BlockSpec((seq_len, bw), lambda b, n: (rb0 + b, n)),
        pl.BlockSpec((seq_len, bw), lambda b, n: (rb0 + b, LRU_BLOCKS + n)),
        pl.BlockSpec((4, bw), lambda b, n: (0, n)),
        pl.BlockSpec((1, bw), lambda b, n: (0, n)),
        pl.BlockSpec((2, 1, bw, bw), lambda b, n: (0, n, 0, 0)), pair,
        pl.BlockSpec((2, 1, bw, bw), lambda b, n: (0, n, 0, 0)), pair,
        pair,
    ]
    args = [proj, proj, w["conv_w"], w["conv_b"], w["w_a"], w["b_a"], w["w_i"], w["b_i"], w["lam"]]
    st_spec = pl.BlockSpec((1, 2, 1, bw), lambda b, n: (b, 0, 0, n))
    if latent:
        in_specs.append(st_spec)
        args.append(s0)
    y_spec = pl.BlockSpec((seq_len, bw), lambda b, n: (rb0 + b, n))
    y_shape = jax.ShapeDtypeStruct((tok.rows, width), BF16)
    aliases = {}
    if latent:
        out_specs, out_shape = [y_spec], [y_shape]
        aliases[len(args)] = 0
        spec, arg = _in_place([y_buf])
        in_specs += spec
        args += arg
    else:
        out_specs = [y_spec, st_spec]
        out_shape = [y_shape, jax.ShapeDtypeStruct((nseq, 2, 1, width), F32)]
    return pl.pallas_call(
        functools.partial(_lru_kernel, seq_len=seq_len, zero_init=not latent, emit_state=not latent,
                          n_alias=len(aliases)),
        grid=(nseq, LRU_BLOCKS),
        in_specs=in_specs,
        out_specs=out_specs,
        out_shape=out_shape,
        input_output_aliases=aliases,
        scratch_shapes=[
            pltpu.VMEM((2, seq_len, bw), F32),
            pltpu.VMEM((2, seq_len, bw), F32),
            pltpu.VMEM((2, seq_len, bw), F32),
        ],
        compiler_params=_cparams("arbitrary", "arbitrary"),
        name="lru_scan_latent" if latent else "lru_scan_prompt",
    )(*args)


def _prep_gla(w_g1, w_g2, b_g, norm_g):
    qkw = w_g2.shape[2]
    g1 = jnp.concatenate([w_g1[0], w_g1[1]], axis=1)
    g1 = jnp.pad(g1, ((0, 0), (0, _LANES - 2 * GLA_RANK)))
    w2 = jnp.zeros((2, _LANES, qkw), F32)
    w2 = w2.at[0, 0:GLA_RANK].set(w_g2[0]).at[1, GLA_RANK:2 * GLA_RANK].set(w_g2[1])
    return {
        "w_g1cat": g1.astype(BF16), "w2": w2.astype(BF16),
        "bg": b_g.reshape(2, 1, qkw), "ng": norm_g.reshape(1, -1),
    }


def _prep_ssd(w_in_dt, conv_w, conv_b, a_log, dt_bias, d_skip, norm_g):
    heads = a_log.shape[1]
    di = heads * SSD_HEADDIM
    hpg = heads // SSD_GROUPS
    gw = di // SSD_GROUPS
    main = 2 * di + 2 * SSD_GROUPS * SSD_STATE

    ncopy = 3
    pad = _LANES - ncopy * 2 * hpg

    def by_group(t):
        t = t.reshape(2, SSD_GROUPS, hpg).transpose(1, 0, 2).reshape(SSD_GROUPS, 1, 2 * hpg)
        return jnp.pad(jnp.tile(t, (1, 1, ncopy)), ((0, 0), (0, 0), (0, pad)))

    w_dt = w_in_dt.reshape(-1, 2, SSD_GROUPS, hpg).transpose(0, 2, 1, 3)
    w_dt = jnp.tile(w_dt.reshape(-1, SSD_GROUPS, 2 * hpg), (1, 1, ncopy))
    w_dt = jnp.pad(w_dt, ((0, 0), (0, 0), (0, pad))).reshape(-1, SSD_GROUPS * _LANES)
    lane = jnp.arange(_LANES)[:, None]
    chan = jnp.arange(gw)[None, :] // SSD_HEADDIM
    expand = jnp.stack([(lane % (2 * hpg) == e * hpg + chan) & (lane < ncopy * 2 * hpg)
                        for e in range(2)]).astype(BF16)
    return {
        "di": di, "w_dt": w_dt.astype(BF16), "n_main": main,
        "conv_w": conv_w, "conv_b": conv_b.reshape(1, -1),
        "dt_bias": by_group(dt_bias), "a_log": by_group(a_log),
        "d_skip": jnp.repeat(d_skip, SSD_HEADDIM).reshape(1, di), "ng": norm_g.reshape(1, di),
        "expand": expand,
    }


def _prep_lru(conv_w, conv_b, w_a, b_a, w_i, b_i, lam):
    width = conv_w.shape[1]
    return {
        "conv_w": conv_w, "conv_b": conv_b.reshape(1, width),
        "w_a": w_a.astype(BF16), "b_a": b_a.reshape(2, 1, width),
        "w_i": w_i.astype(BF16), "b_i": b_i.reshape(2, 1, width), "lam": lam.reshape(2, 1, width),
    }


def kernel(x_prompt, x_sample, state_gla, state_ssd, state_lru, c, c_ctx, w_mod, b_mod, norm_mix_g,
           norm_ffn_g, ffn_w_up, ffn_conv_w, ffn_conv_b, ffn_w_down, final_norm_g, gla_w_in, gla_w_g1,
           gla_w_g2, gla_b_g, gla_norm_g, gla_w_out, ssd_w_in, ssd_conv_w, ssd_conv_b, ssd_a_log,
           ssd_dt_bias, ssd_d, ssd_norm_g, ssd_w_out, lru_w_in, lru_conv_w, lru_conv_b, lru_w_a, lru_b_a,
           lru_w_i, lru_b_i, lru_lambda, lru_w_out):
    bp, lp, d = x_prompt.shape
    bs, ls, _ = x_sample.shape
    depth = w_mod.shape[0]
    tok = _Tokens(bp, lp, bs, ls)
    assert bs + 1 <= MOD_ROWS and ls // GRID_W * GRID_W == ls

    x = jnp.concatenate([x_prompt.reshape(bp * lp, d), x_sample.reshape(bs * ls, d)], axis=0)
    c_rows = jnp.concatenate([c_ctx[None], c, jnp.zeros((MOD_ROWS - 1 - bs, d), F32)], axis=0)
    mod = _mod_table(c_rows, w_mod, b_mod).reshape(depth * MOD_ROWS, MOD_CHUNKS, d)

    ffn_w_down_bf = ffn_w_down.astype(BF16)
    ssd_w_out_bf = ssd_w_out.astype(BF16)
    gla_states, ssd_new, lru_new = None, [], []
    for l in range(depth):
        kind, j = l % 3, l // 3
        h = _norm_mod(x, norm_mix_g[l], mod, l, tok, shift_idx=0, scale_idx=1)
        if kind == 0:
            w = _prep_gla(gla_w_g1[j], gla_w_g2[j], gla_b_g[j], gla_norm_g[j])
            y, gla_states = _gla_mixer(h, state_gla[:, j], gla_w_in, j, w, tok, gla_states)
            w_out = gla_w_out
        elif kind == 1:
            n_main = ssd_conv_w.shape[2] + ssd_d.shape[1] * SSD_HEADDIM
            w = _prep_ssd(ssd_w_in[j, :, n_main:], ssd_conv_w[j], ssd_conv_b[j], ssd_a_log[j],
                          ssd_dt_bias[j], ssd_d[j], ssd_norm_g[j])
            proj = _matmul(h, ssd_w_in, j, col_start=0, n_cols=n_main, bm=1024, bn=1024,
                           name="ssd_in_main")
            dtp = _matmul(h, w["w_dt"][None], 0, bm=1024, bn=SSD_GROUPS * _LANES, name="ssd_in_dt")
            gw = w["di"] // SSD_GROUPS
            s_lat0 = state_ssd[:, j].reshape(bs, 2, SSD_GROUPS, gw, SSD_STATE)
            y, s_new = _ssd_scan(proj, dtp, w, None, tok, latent=False)
            (y,) = _ssd_scan(proj, dtp, w, s_lat0, tok, latent=True, y_buf=y)
            ssd_new.append(s_new.reshape(bp, 2, -1, SSD_HEADDIM, SSD_STATE))
            w_out = ssd_w_out_bf
        else:
            w = _prep_lru(lru_conv_w[j], lru_conv_b[j], lru_w_a[j], lru_b_a[j],
                          lru_w_i[j], lru_b_i[j], lru_lambda[j])
            proj = _matmul(h, lru_w_in, j, bm=1024, bn=1024, name="lru_in")
            s_lat0 = state_lru[:, j].reshape(bs, 2, 1, -1)
            y, s_new = _lru_scan(proj, w, None, tok, latent=False)
            (y,) = _lru_scan(proj, w, s_lat0, tok, latent=True, y_buf=y)
            lru_new.append(s_new.reshape(bp, 2, -1))
            w_out = lru_w_out
        x = _matmul(y, w_out, j, bm=1024, bn=1024, bk=min(w_out.shape[1], 2048),
                    residual=(x, mod, l, tok, 2), name="mixer_out")
        h = _norm_mod(x, norm_ffn_g[l], mod, l, tok, shift_idx=3, scale_idx=4)
        g = _ffn_up(h, ffn_w_up, ffn_conv_w, ffn_conv_b, l, tok)
        x = _matmul(g, ffn_w_down_bf, l, bm=1024, bn=1024, bk=ffn_w_down.shape[1] // 2,
                    residual=(x, mod, l, tok, 5), name="ffn_down")
    y_prompt, y_sample = _final_norm(x, final_norm_g, tok)
    return (y_prompt.reshape(bp, lp, d), y_sample.reshape(bs, ls, d), gla_states,
            jnp.stack(ssd_new, axis=1), jnp.stack(lru_new, axis=1))
```

```python
import functools

import jax
import jax.numpy as jnp
from jax import lax
from jax.experimental import pallas as pl
from jax.experimental.pallas import tpu as pltpu

F32 = jnp.float32
BF16 = jnp.bfloat16

_V7X_VMEM_BYTES = 64 * 1024 * 1024
_VMEM_LIMIT = _V7X_VMEM_BYTES - 8 * 1024 * 1024
_SUBLANES = 8
_LANES = 128

NORM_EPS = 1e-6
MOD_CHUNKS = 6
MOD_ROWS = 8
GRID_W = 64

GLA_HEADS = 4
GLA_RANK = 16
GLA_GATE_NORM = 16.0
SSD_HEADDIM = 64
SSD_GROUPS = 8
SSD_STATE = 128
LRU_BLOCKS = 8
LRU_C = 8.0
CHUNK = 64


def _cparams(*sem):
    return pltpu.CompilerParams(dimension_semantics=sem, vmem_limit_bytes=_VMEM_LIMIT)


def _sigmoid(x):
    return 0.5 + 0.5 * jnp.tanh(0.5 * x)


def _silu_of_half(h):
    return h + h * jnp.tanh(h)


def _silu(x):
    return _silu_of_half(0.5 * x)


def _softplus(x):
    return jnp.maximum(x, 0.0) + jnp.log1p(jnp.exp(-jnp.abs(x)))


def _log_sigmoid(x):
    return jnp.minimum(x, 0.0) - jnp.log1p(jnp.exp(-jnp.abs(x)))


def _gelu_tanh(x):
    return 0.5 * x * (1.0 + jnp.tanh(0.7978845608028654 * (x + 0.044715 * (x * x * x))))


def _neg_expm1_2x(x):
    t = jnp.tanh(x)
    return -2.0 * t / (1.0 - t)


def _split3(x):
    x1 = x.astype(BF16)
    r1 = x - x1.astype(F32)
    x2 = r1.astype(BF16)
    x3 = (r1 - x2.astype(F32)).astype(BF16)
    return x1, x2, x3


def _dot(a, b):
    return jnp.dot(a, b, preferred_element_type=F32)


def _dot_nt(a, b):
    return lax.dot_general(a, b, (((1,), (1,)), ((), ())), preferred_element_type=F32)


def _dot_tn(a, b):
    return lax.dot_general(a, b, (((0,), (0,)), ((), ())), preferred_element_type=F32)


def _exact_dot_left01(m01, x):
    x1, x2, x3 = _split3(x)
    return _dot(m01, x1) + _dot(m01, x2) + _dot(m01, x3)


def _exact_select_copies(x, sel01, ncopy_lanes):
    t1 = x.astype(BF16).astype(F32)
    r1 = x - t1
    t2 = r1.astype(BF16).astype(F32)
    lane = lax.broadcasted_iota(jnp.int32, x.shape, 1)
    terms = jnp.where(lane < ncopy_lanes, t1, jnp.where(lane < 2 * ncopy_lanes, t2, r1 - t2))
    return _dot(terms.astype(BF16), sel01)


def _chunk_masks():
    ii = lax.broadcasted_iota(jnp.int32, (CHUNK, CHUNK), 0)
    jj = lax.broadcasted_iota(jnp.int32, (CHUNK, CHUNK), 1)
    masks = (jj <= ii, jj >= ii)
    tris = tuple(jnp.where(m, 1.0, 0.0).astype(BF16) for m in masks)
    return masks, tris


def _block_chunk_masks(rows):
    ii = lax.broadcasted_iota(jnp.int32, (rows, rows), 0)
    jj = lax.broadcasted_iota(jnp.int32, (rows, rows), 1)
    shift = CHUNK.bit_length() - 1
    same = jnp.where(jnp.right_shift(ii, shift) == jnp.right_shift(jj, shift), 1, 0)
    masks = (same * jnp.where(jj <= ii, 1, 0) > 0, same * jnp.where(jj >= ii, 1, 0) > 0)
    tris = tuple(jnp.where(m, 1.0, 0.0).astype(BF16) for m in masks)
    return masks, tris


def _shifted_rows(ref, r0, nrows, off, total):
    lo, hi = r0 + off, r0 + off + nrows
    clo, chi = max(lo, 0), min(hi, total)
    x = ref[clo:chi, :]
    parts = []
    if clo > lo:
        parts.append(jnp.zeros((clo - lo, x.shape[1]), x.dtype))
    parts.append(x)
    if hi > chi:
        parts.append(jnp.zeros((hi - chi, x.shape[1]), x.dtype))
    return x if len(parts) == 1 else jnp.concatenate(parts, axis=0)


def _conv4_block(ref, w_ref, b_ref, r0, nrows, total, scale=1.0):
    w, b = w_ref[...], b_ref[...]
    if scale != 1.0:
        w, b = scale * w, scale * b
    acc = b + w[0:1, :] * _shifted_rows(ref, r0, nrows, -1, total)
    acc = acc + w[1:2, :] * ref[r0:r0 + nrows, :]
    acc = acc + w[2:3, :] * _shifted_rows(ref, r0, nrows, 1, total)
    acc = acc + w[3:4, :] * _shifted_rows(ref, r0, nrows, 2, total)
    return acc


def _mod_kernel(c_ref, w_ref, b_ref, o_ref):
    c = c_ref[...]
    s = _silu(c).astype(BF16)
    o_ref[0] = _dot(s, w_ref[0].astype(BF16)) + b_ref[0]


def _mod_table(c_rows, w_mod, b_mod):
    depth, d, n = w_mod.shape
    bn = 1024
    return pl.pallas_call(
        _mod_kernel,
        grid=(depth, n // bn),
        in_specs=[
            pl.BlockSpec((MOD_ROWS, d), lambda l, j: (0, 0)),
            pl.BlockSpec((1, d, bn), lambda l, j: (l, 0, j)),
            pl.BlockSpec((1, 1, bn), lambda l, j: (l, 0, j)),
        ],
        out_specs=pl.BlockSpec((1, MOD_ROWS, bn), lambda l, j: (l, 0, j)),
        out_shape=jax.ShapeDtypeStruct((depth, MOD_ROWS, n), F32),
        compiler_params=_cparams("arbitrary", "arbitrary"),
        name="adaln_table",
    )(c_rows, w_mod, b_mod.reshape(depth, 1, n))


class _Tokens:
    def __init__(self, n_prompt, l_prompt, n_latent, l_latent):
        self.bp, self.lp, self.bs, self.ls = n_prompt, l_prompt, n_latent, l_latent
        self.np_rows = n_prompt * l_prompt
        self.ns_rows = n_latent * l_latent
        self.rows = self.np_rows + self.ns_rows

    def mod_row(self, i, bm):
        r0 = i * bm
        return jnp.where(r0 < self.np_rows, 0, 1 + (r0 - self.np_rows) // self.ls)


_NORM_ROWS = 2 * _SUBLANES


def _rmsnorm_rows(x_ref, g_ref, r0):
    x = x_ref[pl.ds(r0, _NORM_ROWS), :]
    ms = jnp.mean(x * x, axis=-1, keepdims=True)
    return x * lax.rsqrt(ms + NORM_EPS) * g_ref[...]


def _row_specs(x, bm, bn, tok, row_of, col_of):
    if not isinstance(x, tuple):
        return [pl.BlockSpec((bm, bn), lambda *g: (row_of(*g), col_of(*g)))], [x]
    npt = tok.np_rows // bm
    return ([pl.BlockSpec((bm, bn), lambda *g: (jnp.minimum(row_of(*g), npt - 1), col_of(*g))),
             pl.BlockSpec((bm, bn), lambda *g: (jnp.maximum(row_of(*g) - npt, 0), col_of(*g)))],
            list(x))


def _on_row_source(x_refs, row_tile, n_prompt_tiles, fn):
    if len(x_refs) == 1:
        fn(x_refs[0])
        return

    @pl.when(row_tile < n_prompt_tiles)
    def _():
        fn(x_refs[0])

    @pl.when(row_tile >= n_prompt_tiles)
    def _():
        fn(x_refs[1])


def _norm_mod_kernel(*refs, shift_idx, scale_idx, n_prompt_tiles):
    *x_refs, g_ref, mod_ref, h_ref = refs

    def normalise(x_ref):
        def body(c, carry):
            r0 = pl.multiple_of(c * _NORM_ROWS, _NORM_ROWS)
            y = _rmsnorm_rows(x_ref, g_ref, r0)
            h = y * (1.0 + mod_ref[0, scale_idx:scale_idx + 1, :]) + mod_ref[0, shift_idx:shift_idx + 1, :]
            h_ref[pl.ds(r0, _NORM_ROWS), :] = h.astype(h_ref.dtype)
            return carry

        lax.fori_loop(0, x_ref.shape[0] // _NORM_ROWS, body, 0, unroll=8)

    _on_row_source(x_refs, pl.program_id(0), n_prompt_tiles, normalise)


def _norm_mod(x, g, mod, layer, tok, *, shift_idx, scale_idx):
    d = g.shape[0]
    bm = 1024
    x_specs, x_args = _row_specs(x, bm, d, tok, lambda i: i, lambda i: 0)
    return pl.pallas_call(
        functools.partial(_norm_mod_kernel, shift_idx=shift_idx, scale_idx=scale_idx,
                          n_prompt_tiles=tok.np_rows // bm),
        grid=(tok.rows // bm,),
        in_specs=x_specs + [
            pl.BlockSpec((1, d), lambda i: (0, 0)),
            pl.BlockSpec((1, MOD_CHUNKS, d), lambda i: (layer * MOD_ROWS + tok.mod_row(i, bm), 0, 0)),
        ],
        out_specs=pl.BlockSpec((bm, d), lambda i: (i, 0)),
        out_shape=jax.ShapeDtypeStruct((tok.rows, d), BF16),
        compiler_params=_cparams("arbitrary"),
        name="norm_adaln",
    )(*x_args, g.reshape(1, d), mod)


def _final_norm_kernel(x_ref, g_ref, op_ref, os_ref, *, n_prompt_tiles):
    def norm_into(o_ref):
        def body(c, carry):
            r0 = pl.multiple_of(c * _NORM_ROWS, _NORM_ROWS)
            o_ref[pl.ds(r0, _NORM_ROWS), :] = _rmsnorm_rows(x_ref, g_ref, r0)
            return carry

        lax.fori_loop(0, x_ref.shape[0] // _NORM_ROWS, body, 0, unroll=8)

    @pl.when(pl.program_id(0) < n_prompt_tiles)
    def _():
        norm_into(op_ref)

    @pl.when(pl.program_id(0) >= n_prompt_tiles)
    def _():
        norm_into(os_ref)


def _final_norm(x, g, tok):
    rows, d = x.shape
    bm = 1024
    npt = tok.np_rows // bm
    return pl.pallas_call(
        functools.partial(_final_norm_kernel, n_prompt_tiles=npt),
        grid=(rows // bm,),
        in_specs=[pl.BlockSpec((bm, d), lambda i: (i, 0)), pl.BlockSpec((1, d), lambda i: (0, 0))],
        out_specs=[pl.BlockSpec((bm, d), lambda i: (jnp.minimum(i, npt - 1), 0)),
                   pl.BlockSpec((bm, d), lambda i: (jnp.maximum(i - npt, 0), 0))],
        out_shape=[jax.ShapeDtypeStruct((tok.np_rows, d), F32),
                   jax.ShapeDtypeStruct((tok.ns_rows, d), F32)],
        compiler_params=_cparams("arbitrary"),
        name="final_norm",
    )(x, g.reshape(1, d))


def _mm_kernel(*refs, nk, gate_idx, n_prompt_tiles):
    if gate_idx is None:
        x_ref, w_ref, o_ref, acc_ref, wb_ref = refs
        res_refs, mod_ref = [], None
    else:
        x_ref, w_ref, *res_refs, mod_ref, o_ref, acc_ref, wb_ref = refs
    i, k = pl.program_id(1), pl.program_id(2)
    if w_ref.dtype == BF16:
        w = w_ref[0]
    elif nk == 1:
        @pl.when(i == 0)
        def _():
            wb_ref[...] = w_ref[0].astype(BF16)

        w = wb_ref[...]
    else:
        w = w_ref[0].astype(BF16)
    part = _dot(x_ref[...], w)

    def finish(acc):
        if gate_idx is None:
            o_ref[...] = acc.astype(o_ref.dtype)
            return

        def gated_residual(res_ref):
            o_ref[...] = res_ref[...] + mod_ref[0, gate_idx:gate_idx + 1, :] * acc

        _on_row_source(res_refs, i, n_prompt_tiles, gated_residual)

    if nk == 1:
        finish(part)
        return

    @pl.when(k == 0)
    def _():
        acc_ref[...] = part

    @pl.when(jnp.logical_and(k > 0, k < nk - 1))
    def _():
        acc_ref[...] += part

    @pl.when(k == nk - 1)
    def _():
        finish(acc_ref[...] + part)


def _matmul(x, w, layer, *, col_start=0, n_cols=None, bm, bn, bk=None, out_dtype=F32, residual=None,
            name="matmul"):
    m, kdim = x.shape
    n_cols = w.shape[2] - col_start if n_cols is None else n_cols
    bk = kdim if bk is None else bk
    nk = kdim // bk
    off = col_start // bn
    in_specs = [
        pl.BlockSpec((bm, bk), lambda j, i, k: (i, k)),
        pl.BlockSpec((1, bk, bn), lambda j, i, k: (layer, k, j + off)),
    ]
    args = [x, w]
    gate_idx, n_prompt_tiles = None, 0
    if residual is not None:
        res, mod, mod_layer, tok, gate_idx = residual
        n_prompt_tiles = tok.np_rows // bm
        res_specs, res_args = _row_specs(res, bm, bn, tok, lambda j, i, k: i, lambda j, i, k: j)
        in_specs += res_specs + [
            pl.BlockSpec((1, MOD_CHUNKS, bn),
                         lambda j, i, k: (mod_layer * MOD_ROWS + tok.mod_row(i, bm), 0, j)),
        ]
        args += res_args + [mod]
    cast_once = w.dtype != BF16 and nk == 1
    tiny = (2 * _SUBLANES, _LANES)
    return pl.pallas_call(
        functools.partial(_mm_kernel, nk=nk, gate_idx=gate_idx, n_prompt_tiles=n_prompt_tiles),
        grid=(n_cols // bn, m // bm, nk),
        in_specs=in_specs,
        out_specs=pl.BlockSpec((bm, bn), lambda j, i, k: (i, j)),
        out_shape=jax.ShapeDtypeStruct((m, n_cols), out_dtype),
        scratch_shapes=[pltpu.VMEM((bm, bn) if nk > 1 else tiny, F32),
                        pltpu.VMEM((bk, bn) if cast_once else tiny, BF16)],
        compiler_params=_cparams("arbitrary", "arbitrary", "arbitrary"),
        name=name,
    )(*args)


def _ffn_up_kernel(h_ref, wa_ref, wv_ref, cw_ref, cb_ref, g_ref, *, n_prompt_tiles, w_prompt, w_latent):
    i = pl.program_id(0)
    x = h_ref[...]
    act = _dot(x, wa_ref[0].astype(BF16))
    val = _dot(x, wv_ref[0].astype(BF16))
    t, tn = act.shape
    row = lax.broadcasted_iota(jnp.int32, (t, tn), 0)
    prev = pltpu.roll(act, 1, 0)
    nxt = pltpu.roll(act, t - 1, 0)

    def horizontal(width):
        col = row & (width - 1)
        return jnp.where(col > 0, prev, 0.0), jnp.where(col < width - 1, nxt, 0.0)

    half_w = 0.5 * cw_ref[...]
    half_b = 0.5 * cb_ref[...]

    def tap_row(kh, left, right):
        return (half_w[3 * kh:3 * kh + 1, :] * left + half_w[3 * kh + 1:3 * kh + 2, :] * act
                + half_w[3 * kh + 2:3 * kh + 3, :] * right)

    def gate(half_conv):
        g_ref[...] = (_silu_of_half(half_conv) * val).astype(g_ref.dtype)

    @pl.when(i < n_prompt_tiles)
    def _():
        left, right = horizontal(w_prompt)
        gate(tap_row(1, left, right) + half_b)

    @pl.when(i >= n_prompt_tiles)
    def _():
        left, right = horizontal(w_latent)
        zeros = jnp.zeros((w_latent, tn), F32)
        above = jnp.concatenate([zeros, tap_row(0, left, right)[:t - w_latent, :]], axis=0)
        below = jnp.concatenate([tap_row(2, left, right)[w_latent:, :], zeros], axis=0)
        gate(above + tap_row(1, left, right) + below + half_b)


def _ffn_up(h, w_up, conv_w, conv_b, layer, tok):
    rows, d = h.shape
    f = w_up.shape[2] // 2
    tm = tok.ls
    tn = 256
    assert tok.np_rows % tm == 0 and tm % tok.lp == 0
    nj = f // tn
    return pl.pallas_call(
        functools.partial(_ffn_up_kernel, n_prompt_tiles=tok.np_rows // tm,
                          w_prompt=tok.lp, w_latent=GRID_W),
        grid=(rows // tm, nj),
        in_specs=[
            pl.BlockSpec((tm, d), lambda i, j: (i, 0)),
            pl.BlockSpec((1, d, tn), lambda i, j: (layer, 0, j)),
            pl.BlockSpec((1, d, tn), lambda i, j: (layer, 0, j + nj)),
            pl.BlockSpec((9, tn), lambda i, j: (0, j)),
            pl.BlockSpec((1, tn), lambda i, j: (0, j)),
        ],
        out_specs=pl.BlockSpec((tm, tn), lambda i, j: (i, j)),
        out_shape=jax.ShapeDtypeStruct((rows, f), BF16),
        compiler_params=_cparams("arbitrary", "arbitrary"),
        name="ffn_up_conv",
    )(h, w_up, w_up, conv_w[layer].reshape(9, f), conv_b[layer].reshape(1, f))


def _gla_kernel(*refs, seq_len, zero_init, emit_state, n_alias, dk, dv):
    it = iter(refs)
    q_ref, k_ref, v_ref, r_ref, low_ref, w2_ref, bg_ref, ng_ref = (next(it) for _ in range(8))
    s0_ref = None if zero_init else next(it)
    for _ in range(n_alias):
        next(it)
    y_ref = next(it)
    sout_ref = next(it).at[0, 0] if emit_state else None
    g_scr, o_scr, s_scr = next(it), next(it), next(it)

    nc = seq_len // CHUNK
    blk = 256
    cpb = blk // CHUNK
    nb = seq_len // blk
    masks, tris = _block_chunk_masks(blk)
    mid_row = (CHUNK // 2, CHUNK // 2 - 1)
    end_row = (CHUNK - 1, 0)
    q_scale = dk ** -0.5

    low = low_ref[...].astype(BF16)
    for e in range(2):
        z = _dot(low, w2_ref[e]) + bg_ref[e]
        g_scr[e] = _log_sigmoid(z) * (1.0 / GLA_GATE_NORM)
        if zero_init:
            s_scr[e] = jnp.zeros((dv, dk), F32)
        else:
            s_scr[e] = s0_ref[0, e, 0].T

    def block_step(e, b):
        r0 = pl.multiple_of(b * blk, blk)
        gc = _exact_dot_left01(tris[e], g_scr[e, pl.ds(r0, blk), :])
        g_scr[e, pl.ds(r0, blk), :] = gc
        g_mid = jnp.concatenate(
            [jnp.broadcast_to(gc[c * CHUNK + mid_row[e]:c * CHUNK + mid_row[e] + 1, :], (CHUNK, dk))
             for c in range(cpb)], axis=0)
        q = q_ref[pl.ds(r0, blk), :] * q_scale
        k = k_ref[pl.ds(r0, blk), :]
        scores = _dot_nt((q * jnp.exp(gc - g_mid)).astype(BF16), (k * jnp.exp(g_mid - gc)).astype(BF16))
        o_scr[e, pl.ds(r0, blk), :] = _dot(jnp.where(masks[e], scores, 0.0).astype(BF16),
                                            v_ref[pl.ds(r0, blk), :])

    def block_body(b, carry):
        block_step(0, b)
        block_step(1, b)
        return carry

    lax.fori_loop(0, nb, block_body, 0, unroll=min(2, nb))

    def chunk_step(e, c):
        r0 = pl.multiple_of(c * CHUNK, CHUNK)
        gc = g_scr[e, pl.ds(r0, CHUNK), :]
        g_end = gc[end_row[e]:end_row[e] + 1, :]
        q = q_ref[pl.ds(r0, CHUNK), :] * q_scale
        k = k_ref[pl.ds(r0, CHUNK), :]
        v = v_ref[pl.ds(r0, CHUNK), :]
        s_t = s_scr[e]
        o_scr[e, pl.ds(r0, CHUNK), :] += _dot_nt((q * jnp.exp(gc)).astype(BF16), s_t.astype(BF16))
        u_t = _dot_tn(v, (k * jnp.exp(g_end - gc)).astype(BF16))
        s_scr[e] = s_t * jnp.exp(g_end) + u_t

    def body(i, carry):
        chunk_step(0, i)
        chunk_step(1, nc - 1 - i)
        return carry

    lax.fori_loop(0, nc, body, 0, unroll=4)

    for r0 in range(0, seq_len, blk):
        o = o_scr[0, r0:r0 + blk, :] + o_scr[1, r0:r0 + blk, :]
        ms = jnp.mean(o * o, axis=-1, keepdims=True)
        y = o * lax.rsqrt(ms + NORM_EPS) * ng_ref[...]
        y_ref[r0:r0 + blk, :] = (y * _silu(r_ref[r0:r0 + blk, :])).astype(y_ref.dtype)

    if emit_state:
        for e in range(2):
            sout_ref[e, 0] = s_scr[e].T


def _in_place(buffers):
    return [pl.BlockSpec(memory_space=pl.ANY) for _ in buffers], list(buffers)


def _gla_scan(qk, v, r, low, w2, bg, ng, s0, tok, *, latent, y_buf=None, state_buf=None, layer=0,
              n_layers=1):
    dk = qk.shape[1] // (2 * GLA_HEADS)
    dv = v.shape[1] // GLA_HEADS
    if latent:
        nseq, seq_len, rb0 = tok.bs, tok.ls, tok.np_rows // tok.ls
    else:
        nseq, seq_len, rb0 = tok.bp, tok.lp, 0
    in_specs = [
        pl.BlockSpec((seq_len, dk), lambda b, h: (rb0 + b, h)),
        pl.BlockSpec((seq_len, dk), lambda b, h: (rb0 + b, GLA_HEADS + h)),
        pl.BlockSpec((seq_len, dv), lambda b, h: (rb0 + b, h)),
        pl.BlockSpec((seq_len, dv), lambda b, h: (rb0 + b, h)),
        pl.BlockSpec((seq_len, _LANES), lambda b, h: (rb0 + b, 0)),
        pl.BlockSpec((2, _LANES, dk), lambda b, h: (0, 0, h)),
        pl.BlockSpec((2, 1, dk), lambda b, h: (0, 0, h)),
        pl.BlockSpec((1, dv), lambda b, h: (0, 0)),
    ]
    args = [qk, qk, v, r, low, w2, bg, ng]
    if latent:
        in_specs.append(pl.BlockSpec((1, 2, 1, dk, dv), lambda b, h: (b, 0, h, 0, 0)))
        args.append(s0)
    y_spec = pl.BlockSpec((seq_len, dv), lambda b, h: (rb0 + b, h))
    y_shape = jax.ShapeDtypeStruct((tok.rows, v.shape[1]), BF16)
    if latent:
        out_specs, out_shape, bufs = [y_spec], [y_shape], [y_buf]
    else:
        out_specs = [y_spec, pl.BlockSpec((1, 1, 2, 1, dk, dv), lambda b, h: (b, layer, 0, h, 0, 0))]
        out_shape = [y_shape, jax.ShapeDtypeStruct((nseq, n_layers, 2, GLA_HEADS, dk, dv), F32)]
        bufs = [y_buf, state_buf]
    aliases = {}
    for o, buf in enumerate(bufs):
        if buf is not None:
            aliases[len(args)] = o
            spec, arg = _in_place([buf])
            in_specs += spec
            args += arg
    return pl.pallas_call(
        functools.partial(_gla_kernel, seq_len=seq_len, zero_init=not latent, emit_state=not latent,
                          n_alias=len(aliases), dk=dk, dv=dv),
        grid=(nseq, GLA_HEADS),
        in_specs=in_specs,
        out_specs=out_specs,
        out_shape=out_shape,
        input_output_aliases=aliases,
        scratch_shapes=[
            pltpu.VMEM((2, seq_len, dk), F32),
            pltpu.VMEM((2, seq_len, dv), F32),
            pltpu.VMEM((2, dv, dk), F32),
        ],
        compiler_params=_cparams("arbitrary", "arbitrary"),
        name="gla_scan_latent" if latent else "gla_scan_prompt",
    )(*args)


def _gla_mixer(h, s_lat0, w_in, layer, w, tok, state_buf):
    w_g1cat, w2, bg, ng = w["w_g1cat"], w["w2"], w["bg"], w["ng"]
    qkw = 2 * w2.shape[2]
    vw = (w_in.shape[2] - qkw) // 2
    qk = _matmul(h, w_in, layer, col_start=0, n_cols=qkw, bm=1024, bn=1024, name="gla_in_qk")
    v = _matmul(h, w_in, layer, col_start=qkw, n_cols=vw, bm=1024, bn=1024, out_dtype=BF16,
                name="gla_in_v")
    r = _matmul(h, w_in, layer, col_start=qkw + vw, n_cols=vw, bm=1024, bn=1024, name="gla_in_r")
    low = _matmul(h, w_g1cat[None], 0, bm=1024, bn=_LANES, name="gla_in_gate")
    y, states = _gla_scan(qk, v, r, low, w2, bg, ng, None, tok, latent=False, state_buf=state_buf,
                          layer=layer, n_layers=w_in.shape[0])
    (y,) = _gla_scan(qk, v, r, low, w2, bg, ng, s_lat0, tok, latent=True, y_buf=y)
    return y, states


def _ssd_kernel(*refs, seq_len, zero_init, emit_state, n_alias):
    it = iter(refs)
    (z_ref, x_ref, b_ref, c_ref, dt_ref, cwx_ref, cbx_ref, cwb_ref, cbb_ref, cwc_ref, cbc_ref,
     dtb_ref, alog_ref, dskip_ref, ng_ref, e_ref) = (next(it) for _ in range(16))
    s0_ref = None if zero_init else next(it)
    for _ in range(n_alias):
        next(it)
    y_ref = next(it)
    sout_ref = next(it) if emit_state else None
    xd_scr, cumx_scr, cumt_scr, bm_scr, cm_scr, y_scr, s_scr = (next(it) for _ in range(7))

    hpg = y_scr.shape[1] // SSD_HEADDIM
    nc = seq_len // CHUNK
    blk = 256
    cpb = blk // CHUNK
    masks, _ = _chunk_masks()
    _, blk_tris = _block_chunk_masks(blk)
    end_row = (CHUNK - 1, 0)
    a_row = -jnp.exp(alog_ref[0])

    for r0 in range(0, seq_len, blk):
        def conv_silu(ref, w_ref, bias_ref):
            return _silu_of_half(_conv4_block(ref, w_ref, bias_ref, r0, blk, seq_len, scale=0.5))

        xs = conv_silu(x_ref, cwx_ref, cbx_ref)
        y_scr[r0:r0 + blk, :] = dskip_ref[...] * xs
        bm_scr[r0:r0 + blk, :] = conv_silu(b_ref, cwb_ref, cbb_ref).astype(BF16)
        cm_scr[r0:r0 + blk, :] = conv_silu(c_ref, cwc_ref, cbc_ref).astype(BF16)
        dt = _softplus(dt_ref[r0:r0 + blk, :] + dtb_ref[0])
        for e in range(2):
            expand = e_ref[e]
            cum = _exact_dot_left01(blk_tris[e], dt * a_row)
            cum_t = cum.T
            for cc in range(cpb):
                cumt_scr[e, r0 // CHUNK + cc] = cum_t[0:2 * hpg, cc * CHUNK:(cc + 1) * CHUNK]
            cumx_scr[e, r0:r0 + blk, :] = _exact_select_copies(cum, expand, 2 * hpg)
            xd_scr[e, r0:r0 + blk, :] = xs * _exact_select_copies(dt, expand, 2 * hpg)

    for e in range(2):
        if zero_init:
            s_scr[e] = jnp.zeros(s_scr.shape[1:], F32)
        else:
            s_scr[e] = s0_ref[0, e, 0].T

    def chunk_step(e, c):
        r0 = pl.multiple_of(c * CHUNK, CHUNK)
        cum_x = cumx_scr[e, pl.ds(r0, CHUNK), :]
        cum_t = cumt_scr[e, c]
        xd = xd_scr[e, pl.ds(r0, CHUNK), :]
        xd_bf = xd.astype(BF16)
        bm = bm_scr[pl.ds(r0, CHUNK), :]
        cm = cm_scr[pl.ds(r0, CHUNK), :]
        cb = _dot_nt(cm, bm)
        parts = []
        for hh in range(hpg):
            lane = slice(hh * SSD_HEADDIM, (hh + 1) * SSD_HEADDIM)
            seg = cum_x[:, lane] - cum_t[e * hpg + hh:e * hpg + hh + 1, :]
            decay = jnp.where(masks[e], jnp.exp(jnp.where(masks[e], seg, 0.0)), 0.0)
            parts.append(_dot((cb * decay).astype(BF16), xd_bf[:, lane]))
        y = jnp.concatenate(parts, axis=1)
        s_t = s_scr[e]
        y = y + _dot(cm, s_t.astype(BF16)) * jnp.exp(cum_x)
        cum_end = cum_x[end_row[e]:end_row[e] + 1, :]
        st_t = _dot_tn(bm, (xd * jnp.exp(cum_end - cum_x)).astype(BF16))
        s_scr[e] = s_t * jnp.exp(cum_end) + st_t
        y_scr[pl.ds(r0, CHUNK), :] += y

    def body(i, carry):
        chunk_step(0, i)
        chunk_step(1, nc - 1 - i)
        return carry

    lax.fori_loop(0, nc, body, 0, unroll=4)

    for r0 in range(0, seq_len, blk):
        y = y_scr[r0:r0 + blk, :] * _silu(z_ref[r0:r0 + blk, :])
        ms = jnp.mean(y * y, axis=-1, keepdims=True)
        y_ref[r0:r0 + blk, :] = (y * lax.rsqrt(ms + NORM_EPS) * ng_ref[...]).astype(y_ref.dtype)

    if emit_state:
        for e in range(2):
            sout_ref[0, e, 0] = s_scr[e].T


def _ssd_scan(proj, dtp, w, s0, tok, *, latent, y_buf=None):
    di = w["di"]
    gw = di // SSD_GROUPS
    if latent:
        nseq, seq_len, rb0 = tok.bs, tok.ls, tok.np_rows // tok.ls
    else:
        nseq, seq_len, rb0 = tok.bp, tok.lp, 0
    nx = di // gw
    nb = (2 * di) // SSD_STATE
    ncb = nb + SSD_GROUPS

    def rows(width, col, single=False):
        mode = {"pipeline_mode": pl.Buffered(1)} if single and latent else {}
        return pl.BlockSpec((seq_len, width), lambda b, g: (rb0 + b, col(g)), **mode)

    def vec(width, col, nrows=1):
        return pl.BlockSpec((nrows, width), lambda b, g: (0, col(g)))

    in_specs = [
        rows(gw, lambda g: g),
        rows(gw, lambda g: nx + g),
        rows(SSD_STATE, lambda g: nb + g),
        rows(SSD_STATE, lambda g: ncb + g),
        pl.BlockSpec((seq_len, _LANES), lambda b, g: (rb0 + b, g)),
        vec(gw, lambda g: g, 4), vec(gw, lambda g: g),
        vec(SSD_STATE, lambda g: di // SSD_STATE + g, 4), vec(SSD_STATE, lambda g: di // SSD_STATE + g),
        vec(SSD_STATE, lambda g: di // SSD_STATE + SSD_GROUPS + g, 4),
        vec(SSD_STATE, lambda g: di // SSD_STATE + SSD_GROUPS + g),
        pl.BlockSpec((1, 1, _LANES), lambda b, g: (g, 0, 0)),
        pl.BlockSpec((1, 1, _LANES), lambda b, g: (g, 0, 0)),
        vec(gw, lambda g: g),
        vec(gw, lambda g: g),
        pl.BlockSpec((2, _LANES, gw), lambda b, g: (0, 0, 0)),
    ]
    args = [proj, proj, proj, proj, dtp, w["conv_w"], w["conv_b"], w["conv_w"], w["conv_b"],
            w["conv_w"], w["conv_b"], w["dt_bias"], w["a_log"], w["d_skip"], w["ng"], w["expand"]]
    st_spec = pl.BlockSpec((1, 2, 1, gw, SSD_STATE), lambda b, g: (b, 0, g, 0, 0))
    if latent:
        in_specs.append(st_spec)
        args.append(s0)
    y_spec = pl.BlockSpec((seq_len, gw), lambda b, g: (rb0 + b, g))
    y_shape = jax.ShapeDtypeStruct((tok.rows, di), BF16)
    aliases = {}
    if latent:
        out_specs, out_shape = [y_spec], [y_shape]
        aliases[len(args)] = 0
        spec, arg = _in_place([y_buf])
        in_specs += spec
        args += arg
    else:
        out_specs = [y_spec, st_spec]
        out_shape = [y_shape, jax.ShapeDtypeStruct((nseq, 2, SSD_GROUPS, gw, SSD_STATE), F32)]

    return pl.pallas_call(
        functools.partial(_ssd_kernel, seq_len=seq_len, zero_init=not latent, emit_state=not latent,
                          n_alias=len(aliases)),
        grid=(nseq, SSD_GROUPS),
        in_specs=in_specs,
        out_specs=out_specs,
        out_shape=out_shape,
        input_output_aliases=aliases,
        scratch_shapes=[
            pltpu.VMEM((2, seq_len, gw), F32),
            pltpu.VMEM((2, seq_len, gw), F32),
            pltpu.VMEM((2, seq_len // CHUNK, 2 * gw // SSD_HEADDIM, CHUNK), F32),
            pltpu.VMEM((seq_len, SSD_STATE), BF16),
            pltpu.VMEM((seq_len, SSD_STATE), BF16),
            pltpu.VMEM((seq_len, gw), F32),
            pltpu.VMEM((2, SSD_STATE, gw), F32),
        ],
        compiler_params=_cparams("arbitrary", "arbitrary"),
        name="ssd_scan_latent" if latent else "ssd_scan_prompt",
    )(*args)


def _lru_kernel(*refs, seq_len, zero_init, emit_state, n_alias):
    it = iter(refs)
    (x_ref, gate_ref, cw_ref, cb_ref, wa_ref, ba_ref, wi_ref, bi_ref, lam_ref) = (next(it) for _ in range(9))
    s0_ref = None if zero_init else next(it)
    for _ in range(n_alias):
        next(it)
    y_ref = next(it)
    sout_ref = next(it) if emit_state else None
    a_scr, u_scr, h_scr = next(it), next(it), next(it)
    width = a_scr.shape[2]

    blk = 256
    for r0 in range(0, seq_len, blk):
        xc = _conv4_block(x_ref, cw_ref, cb_ref, r0, blk, seq_len)
        xc_bf = xc.astype(BF16)
        for e in range(2):
            rg = _sigmoid(_dot(xc_bf, wa_ref[e, 0]) + ba_ref[e])
            ig = _sigmoid(_dot(xc_bf, wi_ref[e, 0]) + bi_ref[e])
            log_a = -LRU_C * rg * _softplus(-lam_ref[e])
            a_scr[e, r0:r0 + blk, :] = jnp.exp(log_a)
            u_scr[e, r0:r0 + blk, :] = jnp.sqrt(_neg_expm1_2x(log_a)) * ig * xc

    nt = seq_len // _SUBLANES
    row = lax.broadcasted_iota(jnp.int32, (_SUBLANES, width), 0)

    def tile_scan(e, t, carry):
        r0 = pl.multiple_of(t * _SUBLANES, _SUBLANES)
        a = a_scr[e, pl.ds(r0, _SUBLANES), :]
        u = u_scr[e, pl.ds(r0, _SUBLANES), :]
        for s in (1, 2, 4):
            if e == 0:
                valid = row >= s
                a_sh, u_sh = pltpu.roll(a, s, 0), pltpu.roll(u, s, 0)
            else:
                valid = row < _SUBLANES - s
                a_sh, u_sh = pltpu.roll(a, _SUBLANES - s, 0), pltpu.roll(u, _SUBLANES - s, 0)
            u = u + a * jnp.where(valid, u_sh, 0.0)
            a = a * jnp.where(valid, a_sh, 1.0)
        hcur = u + a * carry
        last = _SUBLANES - 1 if e == 0 else 0
        return hcur, hcur[last:last + 1, :]

    def body(i, carry):
        cf, cbk = carry
        hf, cf = tile_scan(0, i, cf)
        r0 = pl.multiple_of(i * _SUBLANES, _SUBLANES)
        h_scr[0, pl.ds(r0, _SUBLANES), :] = hf
        tb = nt - 1 - i
        hb, cbk = tile_scan(1, tb, cbk)
        rb = pl.multiple_of(tb * _SUBLANES, _SUBLANES)
        h_scr[1, pl.ds(rb, _SUBLANES), :] = hb
        return cf, cbk

    if zero_init:
        init = (jnp.zeros((1, width), F32), jnp.zeros((1, width), F32))
    else:
        init = (s0_ref[0, 0], s0_ref[0, 1])
    cf, cbk = lax.fori_loop(0, nt, body, init, unroll=4)

    for r0 in range(0, seq_len, blk):
        hsum = h_scr[0, r0:r0 + blk, :] + h_scr[1, r0:r0 + blk, :]
        y_ref[r0:r0 + blk, :] = (hsum * _gelu_tanh(gate_ref[r0:r0 + blk, :])).astype(y_ref.dtype)

    if emit_state:
        sout_ref[0, 0] = cf
        sout_ref[0, 1] = cbk


def _lru_scan(proj, w, s0, tok, *, latent, y_buf=None):
    width = proj.shape[1] // 2
    bw = width // LRU_BLOCKS
    if latent:
        nseq, seq_len, rb0 = tok.bs, tok.ls, tok.np_rows // tok.ls
    else:
        nseq, seq_len, rb0 = tok.bp, tok.lp, 0
    pair = pl.BlockSpec((2, 1, bw), lambda b, n: (0, 0, n))
    in_specs = [
        pl.BlockSpec((seq_len, bw), lambda b, n: (rb0 + b, n)),
        pl.BlockSpec((seq_len, bw), lambda b, n: (rb0 + b, LRU_BLOCKS + n)),
        pl.BlockSpec((4, bw), lambda b, n: (0, n)),
        pl.BlockSpec((1, bw), lambda b, n: (0, n)),
        pl.BlockSpec((2, 1, bw, bw), lambda b, n: (0, n, 0, 0)), pair,
        pl.BlockSpec((2, 1, bw, bw), lambda b, n: (0, n, 0, 0)), pair,
        pair,
    ]
    args = [proj, proj, w["conv_w"], w["conv_b"], w["w_a"], w["b_a"], w["w_i"], w["b_i"], w["lam"]]
    st_spec = pl.BlockSpec((1, 2, 1, bw), lambda b, n: (b, 0, 0, n))
    if latent:
        in_specs.append(st_spec)
        args.append(s0)
    y_spec = pl.BlockSpec((seq_len, bw), lambda b, n: (rb0 + b, n))
    y_shape = jax.ShapeDtypeStruct((tok.rows, width), BF16)
    aliases = {}
    if latent:
        out_specs, out_shape = [y_spec], [y_shape]
        aliases[len(args)] = 0
        spec, arg = _in_place([y_buf])
        in_specs += spec
        args += arg
    else:
        out_specs = [y_spec, st_spec]
        out_shape = [y_shape, jax.ShapeDtypeStruct((nseq, 2, 1, width), F32)]
    return pl.pallas_call(
        functools.partial(_lru_kernel, seq_len=seq_len, zero_init=not latent, emit_state=not latent,
                          n_alias=len(aliases)),
        grid=(nseq, LRU_BLOCKS),
        in_specs=in_specs,
        out_specs=out_specs,
        out_shape=out_shape,
        input_output_aliases=aliases,
        scratch_shapes=[
            pltpu.VMEM((2, seq_len, bw), F32),
            pltpu.VMEM((2, seq_len, bw), F32),
            pltpu.VMEM((2, seq_len, bw), F32),
        ],
        compiler_params=_cparams("arbitrary", "arbitrary"),
        name="lru_scan_latent" if latent else "lru_scan_prompt",
    )(*args)


def _prep_gla(w_g1, w_g2, b_g, norm_g):
    qkw = w_g2.shape[2]
    g1 = jnp.concatenate([w_g1[0], w_g1[1]], axis=1)
    g1 = jnp.pad(g1, ((0, 0), (0, _LANES - 2 * GLA_RANK)))
    w2 = jnp.zeros((2, _LANES, qkw), F32)
    w2 = w2.at[0, 0:GLA_RANK].set(w_g2[0]).at[1, GLA_RANK:2 * GLA_RANK].set(w_g2[1])
    return {
        "w_g1cat": g1.astype(BF16), "w2": w2.astype(BF16),
        "bg": b_g.reshape(2, 1, qkw), "ng": norm_g.reshape(1, -1),
    }


def _prep_ssd(w_in_dt, conv_w, conv_b, a_log, dt_bias, d_skip, norm_g):
    heads = a_log.shape[1]
    di = heads * SSD_HEADDIM
    hpg = heads // SSD_GROUPS
    gw = di // SSD_GROUPS
    main = 2 * di + 2 * SSD_GROUPS * SSD_STATE

    ncopy = 3
    pad = _LANES - ncopy * 2 * hpg

    def by_group(t):
        t = t.reshape(2, SSD_GROUPS, hpg).transpose(1, 0, 2).reshape(SSD_GROUPS, 1, 2 * hpg)
        return jnp.pad(jnp.tile(t, (1, 1, ncopy)), ((0, 0), (0, 0), (0, pad)))

    w_dt = w_in_dt.reshape(-1, 2, SSD_GROUPS, hpg).transpose(0, 2, 1, 3)
    w_dt = jnp.tile(w_dt.reshape(-1, SSD_GROUPS, 2 * hpg), (1, 1, ncopy))
    w_dt = jnp.pad(w_dt, ((0, 0), (0, 0), (0, pad))).reshape(-1, SSD_GROUPS * _LANES)
    lane = jnp.arange(_LANES)[:, None]
    chan = jnp.arange(gw)[None, :] // SSD_HEADDIM
    expand = jnp.stack([(lane % (2 * hpg) == e * hpg + chan) & (lane < ncopy * 2 * hpg)
                        for e in range(2)]).astype(BF16)
    return {
        "di": di, "w_dt": w_dt.astype(BF16), "n_main": main,
        "conv_w": conv_w, "conv_b": conv_b.reshape(1, -1),
        "dt_bias": by_group(dt_bias), "a_log": by_group(a_log),
        "d_skip": jnp.repeat(d_skip, SSD_HEADDIM).reshape(1, di), "ng": norm_g.reshape(1, di),
        "expand": expand,
    }


def _prep_lru(conv_w, conv_b, w_a, b_a, w_i, b_i, lam):
    width = conv_w.shape[1]
    return {
        "conv_w": conv_w, "conv_b": conv_b.reshape(1, width),
        "w_a": w_a.astype(BF16), "b_a": b_a.reshape(2, 1, width),
        "w_i": w_i.astype(BF16), "b_i": b_i.reshape(2, 1, width), "lam": lam.reshape(2, 1, width),
    }


def kernel(x_prompt, x_sample, state_gla, state_ssd, state_lru, c, c_ctx, w_mod, b_mod, norm_mix_g,
           norm_ffn_g, ffn_w_up, ffn_conv_w, ffn_conv_b, ffn_w_down, final_norm_g, gla_w_in, gla_w_g1,
           gla_w_g2, gla_b_g, gla_norm_g, gla_w_out, ssd_w_in, ssd_conv_w, ssd_conv_b, ssd_a_log,
           ssd_dt_bias, ssd_d, ssd_norm_g, ssd_w_out, lru_w_in, lru_conv_w, lru_conv_b, lru_w_a, lru_b_a,
           lru_w_i, lru_b_i, lru_lambda, lru_w_out):
    bp, lp, d = x_prompt.shape
    bs, ls, _ = x_sample.shape
    depth = w_mod.shape[0]
    tok = _Tokens(bp, lp, bs, ls)
    assert bs + 1 <= MOD_ROWS and ls // GRID_W * GRID_W == ls

    x = (x_prompt.reshape(bp * lp, d), x_sample.reshape(bs * ls, d))
    c_rows = jnp.concatenate([c_ctx[None], c, jnp.zeros((MOD_ROWS - 1 - bs, d), F32)], axis=0)
    mod = _mod_table(c_rows, w_mod, b_mod).reshape(depth * MOD_ROWS, MOD_CHUNKS, d)

    ffn_w_down_bf = ffn_w_down.astype(BF16)
    ssd_w_out_bf = ssd_w_out.astype(BF16)
    gla_states, ssd_new, lru_new = None, [], []
    for l in range(depth):
        kind, j = l % 3, l // 3
        h = _norm_mod(x, norm_mix_g[l], mod, l, tok, shift_idx=0, scale_idx=1)
        if kind == 0:
            w = _prep_gla(gla_w_g1[j], gla_w_g2[j], gla_b_g[j], gla_norm_g[j])
            y, gla_states = _gla_mixer(h, state_gla[:, j], gla_w_in, j, w, tok, gla_states)
            w_out = gla_w_out
        elif kind == 1:
            n_main = ssd_conv_w.shape[2] + ssd_d.shape[1] * SSD_HEADDIM
            w = _prep_ssd(ssd_w_in[j, :, n_main:], ssd_conv_w[j], ssd_conv_b[j], ssd_a_log[j],
                          ssd_dt_bias[j], ssd_d[j], ssd_norm_g[j])
            proj = _matmul(h, ssd_w_in, j, col_start=0, n_cols=n_main, bm=1024, bn=1024,
                           name="ssd_in_main")
            dtp = _matmul(h, w["w_dt"][None], 0, bm=1024, bn=SSD_GROUPS * _LANES, name="ssd_in_dt")
            gw = w["di"] // SSD_GROUPS
            s_lat0 = state_ssd[:, j].reshape(bs, 2, SSD_GROUPS, gw, SSD_STATE)
            y, s_new = _ssd_scan(proj, dtp, w, None, tok, latent=False)
            (y,) = _ssd_scan(proj, dtp, w, s_lat0, tok, latent=True, y_buf=y)
            ssd_new.append(s_new.reshape(bp, 2, -1, SSD_HEADDIM, SSD_STATE))
            w_out = ssd_w_out_bf
        else:
            w = _prep_lru(lru_conv_w[j], lru_conv_b[j], lru_w_a[j], lru_b_a[j],
                          lru_w_i[j], lru_b_i[j], lru_lambda[j])
            proj = _matmul(h, lru_w_in, j, bm=1024, bn=1024, name="lru_in")
            s_lat0 = state_lru[:, j].reshape(bs, 2, 1, -1)
            y, s_new = _lru_scan(proj, w, None, tok, latent=False)
            (y,) = _lru_scan(proj, w, s_lat0, tok, latent=True, y_buf=y)
            lru_new.append(s_new.reshape(bp, 2, -1))
            w_out = lru_w_out
        x = _matmul(y, w_out, j, bm=512 if isinstance(x, tuple) else 1024, bn=1024,
                    bk=min(w_out.shape[1], 2048), residual=(x, mod, l, tok, 2), name="mixer_out")
        h = _norm_mod(x, norm_ffn_g[l], mod, l, tok, shift_idx=3, scale_idx=4)
        g = _ffn_up(h, ffn_w_up, ffn_conv_w, ffn_conv_b, l, tok)
        x = _matmul(g, ffn_w_down_bf, l, bm=1024, bn=1024, bk=ffn_w_down.shape[1] // 2,
                    residual=(x, mod, l, tok, 5), name="ffn_down")
    y_prompt, y_sample = _final_norm(x, final_norm_g, tok)
    return (y_prompt.reshape(bp, lp, d), y_sample.reshape(bs, ls, d), gla_states,
            jnp.stack(ssd_new, axis=1), jnp.stack(lru_new, axis=1))
```

```python
import functools

import jax
import jax.numpy as jnp
from jax import lax
from jax.experimental import pallas as pl
from jax.experimental.pallas import tpu as pltpu

F32 = jnp.float32
BF16 = jnp.bfloat16

_V7X_VMEM_BYTES = 64 * 1024 * 1024
_VMEM_LIMIT = _V7X_VMEM_BYTES - 8 * 1024 * 1024
_SUBLANES = 8
_LANES = 128

NORM_EPS = 1e-6
MOD_CHUNKS = 6
MOD_ROWS = 8
GRID_W = 64

GLA_HEADS = 4
GLA_RANK = 16
GLA_GATE_NORM = 16.0
SSD_HEADDIM = 64
SSD_GROUPS = 8
SSD_STATE = 128
LRU_BLOCKS = 8
LRU_C = 8.0
CHUNK = 64


def _cparams(*sem):
    return pltpu.CompilerParams(dimension_semantics=sem, vmem_limit_bytes=_VMEM_LIMIT)


def _sigmoid(x):
    return 0.5 + 0.5 * jnp.tanh(0.5 * x)


def _silu_of_half(h):
    return h + h * jnp.tanh(h)


def _silu(x):
    return _silu_of_half(0.5 * x)


def _softplus(x):
    return jnp.maximum(x, 0.0) + jnp.log1p(jnp.exp(-jnp.abs(x)))


def _log_sigmoid(x):
    return jnp.minimum(x, 0.0) - jnp.log1p(jnp.exp(-jnp.abs(x)))


def _gelu_tanh(x):
    return 0.5 * x * (1.0 + jnp.tanh(0.7978845608028654 * (x + 0.044715 * (x * x * x))))


def _neg_expm1_2x(x):
    t = jnp.tanh(x)
    return -2.0 * t / (1.0 - t)


def _split3(x):
    x1 = x.astype(BF16)
    r1 = x - x1.astype(F32)
    x2 = r1.astype(BF16)
    x3 = (r1 - x2.astype(F32)).astype(BF16)
    return x1, x2, x3


def _dot(a, b):
    return jnp.dot(a, b, preferred_element_type=F32)


def _dot_nt(a, b):
    return lax.dot_general(a, b, (((1,), (1,)), ((), ())), preferred_element_type=F32)


def _dot_tn(a, b):
    return lax.dot_general(a, b, (((0,), (0,)), ((), ())), preferred_element_type=F32)


def _exact_dot_left01(m01, x):
    x1, x2, x3 = _split3(x)
    return _dot(m01, x1) + _dot(m01, x2) + _dot(m01, x3)


def _exact_select_copies(x, sel01, ncopy_lanes):
    t1 = x.astype(BF16).astype(F32)
    r1 = x - t1
    t2 = r1.astype(BF16).astype(F32)
    lane = lax.broadcasted_iota(jnp.int32, x.shape, 1)
    terms = jnp.where(lane < ncopy_lanes, t1, jnp.where(lane < 2 * ncopy_lanes, t2, r1 - t2))
    return _dot(terms.astype(BF16), sel01)


def _chunk_masks():
    ii = lax.broadcasted_iota(jnp.int32, (CHUNK, CHUNK), 0)
    jj = lax.broadcasted_iota(jnp.int32, (CHUNK, CHUNK), 1)
    masks = (jj <= ii, jj >= ii)
    tris = tuple(jnp.where(m, 1.0, 0.0).astype(BF16) for m in masks)
    return masks, tris


def _block_chunk_masks(rows):
    ii = lax.broadcasted_iota(jnp.int32, (rows, rows), 0)
    jj = lax.broadcasted_iota(jnp.int32, (rows, rows), 1)
    shift = CHUNK.bit_length() - 1
    same = jnp.where(jnp.right_shift(ii, shift) == jnp.right_shift(jj, shift), 1, 0)
    masks = (same * jnp.where(jj <= ii, 1, 0) > 0, same * jnp.where(jj >= ii, 1, 0) > 0)
    tris = tuple(jnp.where(m, 1.0, 0.0).astype(BF16) for m in masks)
    return masks, tris


def _shifted_rows(ref, r0, nrows, off, total):
    lo, hi = r0 + off, r0 + off + nrows
    clo, chi = max(lo, 0), min(hi, total)
    x = ref[clo:chi, :]
    parts = []
    if clo > lo:
        parts.append(jnp.zeros((clo - lo, x.shape[1]), x.dtype))
    parts.append(x)
    if hi > chi:
        parts.append(jnp.zeros((hi - chi, x.shape[1]), x.dtype))
    return x if len(parts) == 1 else jnp.concatenate(parts, axis=0)


def _conv4_block(ref, w_ref, b_ref, r0, nrows, total, scale=1.0):
    w, b = w_ref[...], b_ref[...]
    if scale != 1.0:
        w, b = scale * w, scale * b
    acc = b + w[0:1, :] * _shifted_rows(ref, r0, nrows, -1, total)
    acc = acc + w[1:2, :] * ref[r0:r0 + nrows, :]
    acc = acc + w[2:3, :] * _shifted_rows(ref, r0, nrows, 1, total)
    acc = acc + w[3:4, :] * _shifted_rows(ref, r0, nrows, 2, total)
    return acc


def _mod_kernel(c_ref, w_ref, b_ref, o_ref):
    c = c_ref[...]
    s = _silu(c).astype(BF16)
    o_ref[0] = _dot(s, w_ref[0].astype(BF16)) + b_ref[0]


def _mod_table(c_rows, w_mod, b_mod):
    depth, d, n = w_mod.shape
    bn = 1024
    return pl.pallas_call(
        _mod_kernel,
        grid=(depth, n // bn),
        in_specs=[
            pl.BlockSpec((MOD_ROWS, d), lambda l, j: (0, 0)),
            pl.BlockSpec((1, d, bn), lambda l, j: (l, 0, j)),
            pl.BlockSpec((1, 1, bn), lambda l, j: (l, 0, j)),
        ],
        out_specs=pl.BlockSpec((1, MOD_ROWS, bn), lambda l, j: (l, 0, j)),
        out_shape=jax.ShapeDtypeStruct((depth, MOD_ROWS, n), F32),
        compiler_params=_cparams("arbitrary", "arbitrary"),
        name="adaln_table",
    )(c_rows, w_mod, b_mod.reshape(depth, 1, n))


class _Tokens:
    def __init__(self, n_prompt, l_prompt, n_latent, l_latent):
        self.bp, self.lp, self.bs, self.ls = n_prompt, l_prompt, n_latent, l_latent
        self.np_rows = n_prompt * l_prompt
        self.ns_rows = n_latent * l_latent
        self.rows = self.np_rows + self.ns_rows

    def mod_row(self, i, bm):
        r0 = i * bm
        return jnp.where(r0 < self.np_rows, 0, 1 + (r0 - self.np_rows) // self.ls)


_NORM_ROWS = 2 * _SUBLANES


def _rmsnorm_rows(x_ref, g_ref, r0):
    x = x_ref[pl.ds(r0, _NORM_ROWS), :]
    ms = jnp.mean(x * x, axis=-1, keepdims=True)
    return x * lax.rsqrt(ms + NORM_EPS) * g_ref[...]


def _row_specs(x, bm, bn, tok, row_of, col_of):
    if not isinstance(x, tuple):
        return [pl.BlockSpec((bm, bn), lambda *g: (row_of(*g), col_of(*g)))], [x]
    npt = tok.np_rows // bm
    return ([pl.BlockSpec((bm, bn), lambda *g: (jnp.minimum(row_of(*g), npt - 1), col_of(*g))),
             pl.BlockSpec((bm, bn), lambda *g: (jnp.maximum(row_of(*g) - npt, 0), col_of(*g)))],
            list(x))


def _on_row_source(x_refs, row_tile, n_prompt_tiles, fn):
    if len(x_refs) == 1:
        fn(x_refs[0])
        return

    @pl.when(row_tile < n_prompt_tiles)
    def _():
        fn(x_refs[0])

    @pl.when(row_tile >= n_prompt_tiles)
    def _():
        fn(x_refs[1])


def _norm_mod_kernel(*refs, shift_idx, scale_idx, n_prompt_tiles):
    *x_refs, g_ref, mod_ref, h_ref = refs

    def normalise(x_ref):
        def body(c, carry):
            r0 = pl.multiple_of(c * _NORM_ROWS, _NORM_ROWS)
            y = _rmsnorm_rows(x_ref, g_ref, r0)
            h = y * (1.0 + mod_ref[0, scale_idx:scale_idx + 1, :]) + mod_ref[0, shift_idx:shift_idx + 1, :]
            h_ref[pl.ds(r0, _NORM_ROWS), :] = h.astype(h_ref.dtype)
            return carry

        lax.fori_loop(0, x_ref.shape[0] // _NORM_ROWS, body, 0, unroll=8)

    _on_row_source(x_refs, pl.program_id(0), n_prompt_tiles, normalise)


def _norm_mod(x, g, mod, layer, tok, *, shift_idx, scale_idx):
    d = g.shape[0]
    bm = 1024
    x_specs, x_args = _row_specs(x, bm, d, tok, lambda i: i, lambda i: 0)
    return pl.pallas_call(
        functools.partial(_norm_mod_kernel, shift_idx=shift_idx, scale_idx=scale_idx,
                          n_prompt_tiles=tok.np_rows // bm),
        grid=(tok.rows // bm,),
        in_specs=x_specs + [
            pl.BlockSpec((1, d), lambda i: (0, 0)),
            pl.BlockSpec((1, MOD_CHUNKS, d), lambda i: (layer * MOD_ROWS + tok.mod_row(i, bm), 0, 0)),
        ],
        out_specs=pl.BlockSpec((bm, d), lambda i: (i, 0)),
        out_shape=jax.ShapeDtypeStruct((tok.rows, d), BF16),
        compiler_params=_cparams("arbitrary"),
        name="norm_adaln",
    )(*x_args, g.reshape(1, d), mod)


def _final_norm_kernel(x_ref, g_ref, op_ref, os_ref, *, n_prompt_tiles):
    def norm_into(o_ref):
        def body(c, carry):
            r0 = pl.multiple_of(c * _NORM_ROWS, _NORM_ROWS)
            o_ref[pl.ds(r0, _NORM_ROWS), :] = _rmsnorm_rows(x_ref, g_ref, r0)
            return carry

        lax.fori_loop(0, x_ref.shape[0] // _NORM_ROWS, body, 0, unroll=8)

    @pl.when(pl.program_id(0) < n_prompt_tiles)
    def _():
        norm_into(op_ref)

    @pl.when(pl.program_id(0) >= n_prompt_tiles)
    def _():
        norm_into(os_ref)


def _final_norm(x, g, tok):
    rows, d = x.shape
    bm = 1024
    npt = tok.np_rows // bm
    return pl.pallas_call(
        functools.partial(_final_norm_kernel, n_prompt_tiles=npt),
        grid=(rows // bm,),
        in_specs=[pl.BlockSpec((bm, d), lambda i: (i, 0)), pl.BlockSpec((1, d), lambda i: (0, 0))],
        out_specs=[pl.BlockSpec((bm, d), lambda i: (jnp.minimum(i, npt - 1), 0)),
                   pl.BlockSpec((bm, d), lambda i: (jnp.maximum(i - npt, 0), 0))],
        out_shape=[jax.ShapeDtypeStruct((tok.np_rows, d), F32),
                   jax.ShapeDtypeStruct((tok.ns_rows, d), F32)],
        compiler_params=_cparams("arbitrary"),
        name="final_norm",
    )(x, g.reshape(1, d))


def _mm_kernel(*refs, nk, gate_idx, n_prompt_tiles):
    if gate_idx is None:
        x_ref, w_ref, o_ref, acc_ref, wb_ref = refs
        res_refs, mod_ref = [], None
    else:
        x_ref, w_ref, *res_refs, mod_ref, o_ref, acc_ref, wb_ref = refs
    i, k = pl.program_id(1), pl.program_id(2)
    if w_ref.dtype == BF16:
        w = w_ref[0]
    elif nk == 1:
        @pl.when(i == 0)
        def _():
            wb_ref[...] = w_ref[0].astype(BF16)

        w = wb_ref[...]
    else:
        w = w_ref[0].astype(BF16)
    part = _dot(x_ref[...], w)

    def finish(acc):
        if gate_idx is None:
            o_ref[...] = acc.astype(o_ref.dtype)
            return

        def gated_residual(res_ref):
            o_ref[...] = res_ref[...] + mod_ref[0, gate_idx:gate_idx + 1, :] * acc

        _on_row_source(res_refs, i, n_prompt_tiles, gated_residual)

    if nk == 1:
        finish(part)
        return

    @pl.when(k == 0)
    def _():
        acc_ref[...] = part

    @pl.when(jnp.logical_and(k > 0, k < nk - 1))
    def _():
        acc_ref[...] += part

    @pl.when(k == nk - 1)
    def _():
        finish(acc_ref[...] + part)


def _matmul(x, w, layer, *, col_start=0, n_cols=None, bm, bn, bk=None, out_dtype=F32, residual=None,
            name="matmul"):
    m, kdim = x.shape
    n_cols = w.shape[2] - col_start if n_cols is None else n_cols
    bk = kdim if bk is None else bk
    nk = kdim // bk
    off = col_start // bn
    in_specs = [
        pl.BlockSpec((bm, bk), lambda j, i, k: (i, k)),
        pl.BlockSpec((1, bk, bn), lambda j, i, k: (layer, k, j + off)),
    ]
    args = [x, w]
    gate_idx, n_prompt_tiles = None, 0
    if residual is not None:
        res, mod, mod_layer, tok, gate_idx = residual
        n_prompt_tiles = tok.np_rows // bm
        res_specs, res_args = _row_specs(res, bm, bn, tok, lambda j, i, k: i, lambda j, i, k: j)
        in_specs += res_specs + [
            pl.BlockSpec((1, MOD_CHUNKS, bn),
                         lambda j, i, k: (mod_layer * MOD_ROWS + tok.mod_row(i, bm), 0, j)),
        ]
        args += res_args + [mod]
    cast_once = w.dtype != BF16 and nk == 1
    tiny = (2 * _SUBLANES, _LANES)
    return pl.pallas_call(
        functools.partial(_mm_kernel, nk=nk, gate_idx=gate_idx, n_prompt_tiles=n_prompt_tiles),
        grid=(n_cols // bn, m // bm, nk),
        in_specs=in_specs,
        out_specs=pl.BlockSpec((bm, bn), lambda j, i, k: (i, j)),
        out_shape=jax.ShapeDtypeStruct((m, n_cols), out_dtype),
        scratch_shapes=[pltpu.VMEM((bm, bn) if nk > 1 else tiny, F32),
                        pltpu.VMEM((bk, bn) if cast_once else tiny, BF16)],
        compiler_params=_cparams("arbitrary", "arbitrary", "arbitrary"),
        name=name,
    )(*args)


def _ffn_up_kernel(h_ref, wa_ref, wv_ref, cw_ref, cb_ref, g_ref, *, n_prompt_tiles, w_prompt, w_latent):
    i = pl.program_id(0)
    x = h_ref[...]
    act = _dot(x, wa_ref[0].astype(BF16))
    val = _dot(x, wv_ref[0].astype(BF16))
    t, tn = act.shape
    row = lax.broadcasted_iota(jnp.int32, (t, tn), 0)
    prev = pltpu.roll(act, 1, 0)
    nxt = pltpu.roll(act, t - 1, 0)

    def horizontal(width):
        col = row & (width - 1)
        return jnp.where(col > 0, prev, 0.0), jnp.where(col < width - 1, nxt, 0.0)

    half_w = 0.5 * cw_ref[...]
    half_b = 0.5 * cb_ref[...]

    def tap_row(kh, left, right):
        return (half_w[3 * kh:3 * kh + 1, :] * left + half_w[3 * kh + 1:3 * kh + 2, :] * act
                + half_w[3 * kh + 2:3 * kh + 3, :] * right)

    def gate(half_conv):
        g_ref[...] = (_silu_of_half(half_conv) * val).astype(g_ref.dtype)

    @pl.when(i < n_prompt_tiles)
    def _():
        left, right = horizontal(w_prompt)
        gate(tap_row(1, left, right) + half_b)

    @pl.when(i >= n_prompt_tiles)
    def _():
        left, right = horizontal(w_latent)
        zeros = jnp.zeros((w_latent, tn), F32)
        above = jnp.concatenate([zeros, tap_row(0, left, right)[:t - w_latent, :]], axis=0)
        below = jnp.concatenate([tap_row(2, left, right)[w_latent:, :], zeros], axis=0)
        gate(above + tap_row(1, left, right) + below + half_b)


def _ffn_up(h, w_up, conv_w, conv_b, layer, tok):
    rows, d = h.shape
    f = w_up.shape[2] // 2
    tm = tok.ls
    tn = 256
    assert tok.np_rows % tm == 0 and tm % tok.lp == 0
    nj = f // tn
    return pl.pallas_call(
        functools.partial(_ffn_up_kernel, n_prompt_tiles=tok.np_rows // tm,
                          w_prompt=tok.lp, w_latent=GRID_W),
        grid=(rows // tm, nj),
        in_specs=[
            pl.BlockSpec((tm, d), lambda i, j: (i, 0)),
            pl.BlockSpec((1, d, tn), lambda i, j: (layer, 0, j)),
            pl.BlockSpec((1, d, tn), lambda i, j: (layer, 0, j + nj)),
            pl.BlockSpec((9, tn), lambda i, j: (0, j)),
            pl.BlockSpec((1, tn), lambda i, j: (0, j)),
        ],
        out_specs=pl.BlockSpec((tm, tn), lambda i, j: (i, j)),
        out_shape=jax.ShapeDtypeStruct((rows, f), BF16),
        compiler_params=_cparams("arbitrary", "arbitrary"),
        name="ffn_up_conv",
    )(h, w_up, w_up, conv_w[layer].reshape(9, f), conv_b[layer].reshape(1, f))


def _gla_kernel(*refs, seq_len, zero_init, emit_state, n_alias, dk, dv):
    it = iter(refs)
    q_ref, k_ref, v_ref, r_ref, low_ref, w2_ref, bg_ref, ng_ref = (next(it) for _ in range(8))
    s0_ref = None if zero_init else next(it)
    for _ in range(n_alias):
        next(it)
    y_ref = next(it)
    sout_ref = next(it).at[0, 0] if emit_state else None
    g_scr, o_scr, s_scr = next(it), next(it), next(it)

    nc = seq_len // CHUNK
    blk = 256
    cpb = blk // CHUNK
    nb = seq_len // blk
    masks, tris = _block_chunk_masks(blk)
    mid_row = (CHUNK // 2, CHUNK // 2 - 1)
    end_row = (CHUNK - 1, 0)
    q_scale = dk ** -0.5

    low = low_ref[...].astype(BF16)
    for e in range(2):
        z = _dot(low, w2_ref[e]) + bg_ref[e]
        g_scr[e] = _log_sigmoid(z) * (1.0 / GLA_GATE_NORM)
        if zero_init:
            s_scr[e] = jnp.zeros((dv, dk), F32)
        else:
            s_scr[e] = s0_ref[0, e, 0].T

    def block_step(e, b):
        r0 = pl.multiple_of(b * blk, blk)
        gc = _exact_dot_left01(tris[e], g_scr[e, pl.ds(r0, blk), :])
        g_scr[e, pl.ds(r0, blk), :] = gc
        g_mid = jnp.concatenate(
            [jnp.broadcast_to(gc[c * CHUNK + mid_row[e]:c * CHUNK + mid_row[e] + 1, :], (CHUNK, dk))
             for c in range(cpb)], axis=0)
        q = q_ref[pl.ds(r0, blk), :] * q_scale
        k = k_ref[pl.ds(r0, blk), :]
        scores = _dot_nt((q * jnp.exp(gc - g_mid)).astype(BF16), (k * jnp.exp(g_mid - gc)).astype(BF16))
        o_scr[e, pl.ds(r0, blk), :] = _dot(jnp.where(masks[e], scores, 0.0).astype(BF16),
                                            v_ref[pl.ds(r0, blk), :])

    def block_body(b, carry):
        block_step(0, b)
        block_step(1, b)
        return carry

    lax.fori_loop(0, nb, block_body, 0, unroll=min(4, nb))

    def chunk_step(e, c):
        r0 = pl.multiple_of(c * CHUNK, CHUNK)
        gc = g_scr[e, pl.ds(r0, CHUNK), :]
        g_end = gc[end_row[e]:end_row[e] + 1, :]
        q = q_ref[pl.ds(r0, CHUNK), :] * q_scale
        k = k_ref[pl.ds(r0, CHUNK), :]
        v = v_ref[pl.ds(r0, CHUNK), :]
        s_t = s_scr[e]
        o_scr[e, pl.ds(r0, CHUNK), :] += _dot_nt((q * jnp.exp(gc)).astype(BF16), s_t.astype(BF16))
        u_t = _dot_tn(v, (k * jnp.exp(g_end - gc)).astype(BF16))
        s_scr[e] = s_t * jnp.exp(g_end) + u_t

    def body(i, carry):
        chunk_step(0, i)
        chunk_step(1, nc - 1 - i)
        return carry

    lax.fori_loop(0, nc, body, 0, unroll=min(8, nc))

    for r0 in range(0, seq_len, blk):
        o = o_scr[0, r0:r0 + blk, :] + o_scr[1, r0:r0 + blk, :]
        ms = jnp.mean(o * o, axis=-1, keepdims=True)
        y = o * lax.rsqrt(ms + NORM_EPS) * ng_ref[...]
        y_ref[r0:r0 + blk, :] = (y * _silu(r_ref[r0:r0 + blk, :])).astype(y_ref.dtype)

    if emit_state:
        for e in range(2):
            sout_ref[e, 0] = s_scr[e].T


def _in_place(buffers):
    return [pl.BlockSpec(memory_space=pl.ANY) for _ in buffers], list(buffers)


def _gla_scan(qk, v, r, low, w2, bg, ng, s0, tok, *, latent, y_buf=None, state_buf=None, layer=0,
              n_layers=1):
    dk = qk.shape[1] // (2 * GLA_HEADS)
    dv = v.shape[1] // GLA_HEADS
    if latent:
        nseq, seq_len, rb0 = tok.bs, tok.ls, tok.np_rows // tok.ls
    else:
        nseq, seq_len, rb0 = tok.bp, tok.lp, 0
    in_specs = [
        pl.BlockSpec((seq_len, dk), lambda b, h: (rb0 + b, h)),
        pl.BlockSpec((seq_len, dk), lambda b, h: (rb0 + b, GLA_HEADS + h)),
        pl.BlockSpec((seq_len, dv), lambda b, h: (rb0 + b, h)),
        pl.BlockSpec((seq_len, dv), lambda b, h: (rb0 + b, h)),
        pl.BlockSpec((seq_len, _LANES), lambda b, h: (rb0 + b, 0)),
        pl.BlockSpec((2, _LANES, dk), lambda b, h: (0, 0, h)),
        pl.BlockSpec((2, 1, dk), lambda b, h: (0, 0, h)),
        pl.BlockSpec((1, dv), lambda b, h: (0, 0)),
    ]
    args = [qk, qk, v, r, low, w2, bg, ng]
    if latent:
        in_specs.append(pl.BlockSpec((1, 2, 1, dk, dv), lambda b, h: (b, 0, h, 0, 0)))
        args.append(s0)
    y_spec = pl.BlockSpec((seq_len, dv), lambda b, h: (rb0 + b, h))
    y_shape = jax.ShapeDtypeStruct((tok.rows, v.shape[1]), BF16)
    if latent:
        out_specs, out_shape, bufs = [y_spec], [y_shape], [y_buf]
    else:
        out_specs = [y_spec, pl.BlockSpec((1, 1, 2, 1, dk, dv), lambda b, h: (b, layer, 0, h, 0, 0))]
        out_shape = [y_shape, jax.ShapeDtypeStruct((nseq, n_layers, 2, GLA_HEADS, dk, dv), F32)]
        bufs = [y_buf, state_buf]
    aliases = {}
    for o, buf in enumerate(bufs):
        if buf is not None:
            aliases[len(args)] = o
            spec, arg = _in_place([buf])
            in_specs += spec
            args += arg
    return pl.pallas_call(
        functools.partial(_gla_kernel, seq_len=seq_len, zero_init=not latent, emit_state=not latent,
                          n_alias=len(aliases), dk=dk, dv=dv),
        grid=(nseq, GLA_HEADS),
        in_specs=in_specs,
        out_specs=out_specs,
        out_shape=out_shape,
        input_output_aliases=aliases,
        scratch_shapes=[
            pltpu.VMEM((2, seq_len, dk), F32),
            pltpu.VMEM((2, seq_len, dv), F32),
            pltpu.VMEM((2, dv, dk), F32),
        ],
        compiler_params=_cparams("arbitrary", "arbitrary"),
        name="gla_scan_latent" if latent else "gla_scan_prompt",
    )(*args)


def _gla_mixer(h, s_lat0, w_in, layer, w, tok, state_buf):
    w_g1cat, w2, bg, ng = w["w_g1cat"], w["w2"], w["bg"], w["ng"]
    qkw = 2 * w2.shape[2]
    vw = (w_in.shape[2] - qkw) // 2
    qk = _matmul(h, w_in, layer, col_start=0, n_cols=qkw, bm=1024, bn=1024, name="gla_in_qk")
    v = _matmul(h, w_in, layer, col_start=qkw, n_cols=vw, bm=1024, bn=1024, out_dtype=BF16,
                name="gla_in_v")
    r = _matmul(h, w_in, layer, col_start=qkw + vw, n_cols=vw, bm=1024, bn=1024, name="gla_in_r")
    low = _matmul(h, w_g1cat[None], 0, bm=1024, bn=_LANES, name="gla_in_gate")
    y, states = _gla_scan(qk, v, r, low, w2, bg, ng, None, tok, latent=False, state_buf=state_buf,
                          layer=layer, n_layers=w_in.shape[0])
    (y,) = _gla_scan(qk, v, r, low, w2, bg, ng, s_lat0, tok, latent=True, y_buf=y)
    return y, states


def _ssd_kernel(*refs, seq_len, zero_init, emit_state, n_alias):
    it = iter(refs)
    (z_ref, x_ref, b_ref, c_ref, dt_ref, cwx_ref, cbx_ref, cwb_ref, cbb_ref, cwc_ref, cbc_ref,
     dtb_ref, alog_ref, dskip_ref, ng_ref, e_ref) = (next(it) for _ in range(16))
    s0_ref = None if zero_init else next(it)
    for _ in range(n_alias):
        next(it)
    y_ref = next(it)
    sout_ref = next(it) if emit_state else None
    xd_scr, cumx_scr, cumt_scr, bm_scr, cm_scr, y_scr, s_scr = (next(it) for _ in range(7))

    hpg = y_scr.shape[1] // SSD_HEADDIM
    nc = seq_len // CHUNK
    blk = 256
    cpb = blk // CHUNK
    masks, _ = _chunk_masks()
    _, blk_tris = _block_chunk_masks(blk)
    end_row = (CHUNK - 1, 0)
    a_row = -jnp.exp(alog_ref[0])

    for r0 in range(0, seq_len, blk):
        def conv_silu(ref, w_ref, bias_ref):
            return _silu_of_half(_conv4_block(ref, w_ref, bias_ref, r0, blk, seq_len, scale=0.5))

        xs = conv_silu(x_ref, cwx_ref, cbx_ref)
        y_scr[r0:r0 + blk, :] = dskip_ref[...] * xs
        bm_scr[r0:r0 + blk, :] = conv_silu(b_ref, cwb_ref, cbb_ref).astype(BF16)
        cm_scr[r0:r0 + blk, :] = conv_silu(c_ref, cwc_ref, cbc_ref).astype(BF16)
        dt = _softplus(dt_ref[r0:r0 + blk, :] + dtb_ref[0])
        for e in range(2):
            expand = e_ref[e]
            cum = _exact_dot_left01(blk_tris[e], dt * a_row)
            cum_t = cum.T
            for cc in range(cpb):
                cumt_scr[e, r0 // CHUNK + cc] = cum_t[0:2 * hpg, cc * CHUNK:(cc + 1) * CHUNK]
            cumx_scr[e, r0:r0 + blk, :] = _exact_select_copies(cum, expand, 2 * hpg)
            xd_scr[e, r0:r0 + blk, :] = xs * _exact_select_copies(dt, expand, 2 * hpg)

    for e in range(2):
        if zero_init:
            s_scr[e] = jnp.zeros(s_scr.shape[1:], F32)
        else:
            s_scr[e] = s0_ref[0, e, 0].T

    def chunk_step(e, c):
        r0 = pl.multiple_of(c * CHUNK, CHUNK)
        cum_x = cumx_scr[e, pl.ds(r0, CHUNK), :]
        cum_t = cumt_scr[e, c]
        xd = xd_scr[e, pl.ds(r0, CHUNK), :]
        xd_bf = xd.astype(BF16)
        bm = bm_scr[pl.ds(r0, CHUNK), :]
        cm = cm_scr[pl.ds(r0, CHUNK), :]
        cb = _dot_nt(cm, bm)
        parts = []
        for hh in range(hpg):
            lane = slice(hh * SSD_HEADDIM, (hh + 1) * SSD_HEADDIM)
            seg = cum_x[:, lane] - cum_t[e * hpg + hh:e * hpg + hh + 1, :]
            decay = jnp.where(masks[e], jnp.exp(jnp.where(masks[e], seg, 0.0)), 0.0)
            parts.append(_dot((cb * decay).astype(BF16), xd_bf[:, lane]))
        y = jnp.concatenate(parts, axis=1)
        s_t = s_scr[e]
        y = y + _dot(cm, s_t.astype(BF16)) * jnp.exp(cum_x)
        cum_end = cum_x[end_row[e]:end_row[e] + 1, :]
        st_t = _dot_tn(bm, (xd * jnp.exp(cum_end - cum_x)).astype(BF16))
        s_scr[e] = s_t * jnp.exp(cum_end) + st_t
        y_scr[pl.ds(r0, CHUNK), :] += y

    def body(i, carry):
        chunk_step(0, i)
        chunk_step(1, nc - 1 - i)
        return carry

    lax.fori_loop(0, nc, body, 0, unroll=min(8, nc))

    for r0 in range(0, seq_len, blk):
        y = y_scr[r0:r0 + blk, :] * _silu(z_ref[r0:r0 + blk, :])
        ms = jnp.mean(y * y, axis=-1, keepdims=True)
        y_ref[r0:r0 + blk, :] = (y * lax.rsqrt(ms + NORM_EPS) * ng_ref[...]).astype(y_ref.dtype)

    if emit_state:
        for e in range(2):
            sout_ref[0, e, 0] = s_scr[e].T


def _ssd_scan(proj, dtp, w, s0, tok, *, latent, y_buf=None):
    di = w["di"]
    gw = di // SSD_GROUPS
    if latent:
        nseq, seq_len, rb0 = tok.bs, tok.ls, tok.np_rows // tok.ls
    else:
        nseq, seq_len, rb0 = tok.bp, tok.lp, 0
    nx = di // gw
    nb = (2 * di) // SSD_STATE
    ncb = nb + SSD_GROUPS

    def rows(width, col, single=False):
        mode = {"pipeline_mode": pl.Buffered(1)} if single and latent else {}
        return pl.BlockSpec((seq_len, width), lambda b, g: (rb0 + b, col(g)), **mode)

    def vec(width, col, nrows=1):
        return pl.BlockSpec((nrows, width), lambda b, g: (0, col(g)))

    in_specs = [
        rows(gw, lambda g: g),
        rows(gw, lambda g: nx + g),
        rows(SSD_STATE, lambda g: nb + g),
        rows(SSD_STATE, lambda g: ncb + g),
        pl.BlockSpec((seq_len, _LANES), lambda b, g: (rb0 + b, g)),
        vec(gw, lambda g: g, 4), vec(gw, lambda g: g),
        vec(SSD_STATE, lambda g: di // SSD_STATE + g, 4), vec(SSD_STATE, lambda g: di // SSD_STATE + g),
        vec(SSD_STATE, lambda g: di // SSD_STATE + SSD_GROUPS + g, 4),
        vec(SSD_STATE, lambda g: di // SSD_STATE + SSD_GROUPS + g),
        pl.BlockSpec((1, 1, _LANES), lambda b, g: (g, 0, 0)),
        pl.BlockSpec((1, 1, _LANES), lambda b, g: (g, 0, 0)),
        vec(gw, lambda g: g),
        vec(gw, lambda g: g),
        pl.BlockSpec((2, _LANES, gw), lambda b, g: (0, 0, 0)),
    ]
    args = [proj, proj, proj, proj, dtp, w["conv_w"], w["conv_b"], w["conv_w"], w["conv_b"],
            w["conv_w"], w["conv_b"], w["dt_bias"], w["a_log"], w["d_skip"], w["ng"], w["expand"]]
    st_spec = pl.BlockSpec((1, 2, 1, gw, SSD_STATE), lambda b, g: (b, 0, g, 0, 0))
    if latent:
        in_specs.append(st_spec)
        args.append(s0)
    y_spec = pl.BlockSpec((seq_len, gw), lambda b, g: (rb0 + b, g))
    y_shape = jax.ShapeDtypeStruct((tok.rows, di), BF16)
    aliases = {}
    if latent:
        out_specs, out_shape = [y_spec], [y_shape]
        aliases[len(args)] = 0
        spec, arg = _in_place([y_buf])
        in_specs += spec
        args += arg
    else:
        out_specs = [y_spec, st_spec]
        out_shape = [y_shape, jax.ShapeDtypeStruct((nseq, 2, SSD_GROUPS, gw, SSD_STATE), F32)]

    return pl.pallas_call(
        functools.partial(_ssd_kernel, seq_len=seq_len, zero_init=not latent, emit_state=not latent,
                          n_alias=len(aliases)),
        grid=(nseq, SSD_GROUPS),
        in_specs=in_specs,
        out_specs=out_specs,
        out_shape=out_shape,
        input_output_aliases=aliases,
        scratch_shapes=[
            pltpu.VMEM((2, seq_len, gw), F32),
            pltpu.VMEM((2, seq_len, gw), F32),
            pltpu.VMEM((2, seq_len // CHUNK, 2 * gw // SSD_HEADDIM, CHUNK), F32),
            pltpu.VMEM((seq_len, SSD_STATE), BF16),
            pltpu.VMEM((seq_len, SSD_STATE), BF16),
            pltpu.VMEM((seq_len, gw), F32),
            pltpu.VMEM((2, SSD_STATE, gw), F32),
        ],
        compiler_params=_cparams("arbitrary", "arbitrary"),
        name="ssd_scan_latent" if latent else "ssd_scan_prompt",
    )(*args)


def _lru_kernel(*refs, seq_len, zero_init, emit_state, n_alias):
    it = iter(refs)
    (x_ref, gate_ref, cw_ref, cb_ref, wa_ref, ba_ref, wi_ref, bi_ref, lam_ref) = (next(it) for _ in range(9))
    s0_ref = None if zero_init else next(it)
    for _ in range(n_alias):
        next(it)
    y_ref = next(it)
    sout_ref = next(it) if emit_state else None
    a_scr, u_scr, h_scr = next(it), next(it), next(it)
    width = a_scr.shape[2]

    blk = 256
    for r0 in range(0, seq_len, blk):
        xc = _conv4_block(x_ref, cw_ref, cb_ref, r0, blk, seq_len)
        xc_bf = xc.astype(BF16)
        for e in range(2):
            rg = _sigmoid(_dot(xc_bf, wa_ref[e, 0]) + ba_ref[e])
            ig = _sigmoid(_dot(xc_bf, wi_ref[e, 0]) + bi_ref[e])
            log_a = -LRU_C * rg * _softplus(-lam_ref[e])
            a_scr[e, r0:r0 + blk, :] = jnp.exp(log_a)
            u_scr[e, r0:r0 + blk, :] = jnp.sqrt(_neg_expm1_2x(log_a)) * ig * xc

    nt = seq_len // _SUBLANES
    row = lax.broadcasted_iota(jnp.int32, (_SUBLANES, width), 0)

    def tile_scan(e, t, carry):
        r0 = pl.multiple_of(t * _SUBLANES, _SUBLANES)
        a = a_scr[e, pl.ds(r0, _SUBLANES), :]
        u = u_scr[e, pl.ds(r0, _SUBLANES), :]
        for s in (1, 2, 4):
            if e == 0:
                valid = row >= s
                a_sh, u_sh = pltpu.roll(a, s, 0), pltpu.roll(u, s, 0)
            else:
                valid = row < _SUBLANES - s
                a_sh, u_sh = pltpu.roll(a, _SUBLANES - s, 0), pltpu.roll(u, _SUBLANES - s, 0)
            u = u + a * jnp.where(valid, u_sh, 0.0)
            a = a * jnp.where(valid, a_sh, 1.0)
        hcur = u + a * carry
        last = _SUBLANES - 1 if e == 0 else 0
        return hcur, hcur[last:last + 1, :]

    def body(i, carry):
        cf, cbk = carry
        hf, cf = tile_scan(0, i, cf)
        r0 = pl.multiple_of(i * _SUBLANES, _SUBLANES)
        h_scr[0, pl.ds(r0, _SUBLANES), :] = hf
        tb = nt - 1 - i
        hb, cbk = tile_scan(1, tb, cbk)
        rb = pl.multiple_of(tb * _SUBLANES, _SUBLANES)
        h_scr[1, pl.ds(rb, _SUBLANES), :] = hb
        return cf, cbk

    if zero_init:
        init = (jnp.zeros((1, width), F32), jnp.zeros((1, width), F32))
    else:
        init = (s0_ref[0, 0], s0_ref[0, 1])
    cf, cbk = lax.fori_loop(0, nt, body, init, unroll=4)

    for r0 in range(0, seq_len, blk):
        hsum = h_scr[0, r0:r0 + blk, :] + h_scr[1, r0:r0 + blk, :]
        y_ref[r0:r0 + blk, :] = (hsum * _gelu_tanh(gate_ref[r0:r0 + blk, :])).astype(y_ref.dtype)

    if emit_state:
        sout_ref[0, 0] = cf
        sout_ref[0, 1] = cbk


def _lru_scan(proj, w, s0, tok, *, latent, y_buf=None):
    width = proj.shape[1] // 2
    bw = width // LRU_BLOCKS
    if latent:
        nseq, seq_len, rb0 = tok.bs, tok.ls, tok.np_rows // tok.ls
    else:
        nseq, seq_len, rb0 = tok.bp, tok.lp, 0
    pair = pl.BlockSpec((2, 1, bw), lambda b, n: (0, 0, n))
    in_specs = [
        pl.BlockSpec((seq_len, bw), lambda b, n: (rb0 + b, n)),
        pl.BlockSpec((seq_len, bw), lambda b, n: (rb0 + b, LRU_BLOCKS + n)),
        pl.BlockSpec((4, bw), lambda b, n: (0, n)),
        pl.BlockSpec((1, bw), lambda b, n: (0, n)),
        pl.BlockSpec((2, 1, bw, bw), lambda b, n: (0, n, 0, 0)), pair,
        pl.BlockSpec((2, 1, bw, bw), lambda b, n: (0, n, 0, 0)), pair,
        pair,
    ]
    args = [proj, proj, w["conv_w"], w["conv_b"], w["w_a"], w["b_a"], w["w_i"], w["b_i"], w["lam"]]
    st_spec = pl.BlockSpec((1, 2, 1, bw), lambda b, n: (b, 0, 0, n))
    if latent:
        in_specs.append(st_spec)
        args.append(s0)
    y_spec = pl.BlockSpec((seq_len, bw), lambda b, n: (rb0 + b, n))
    y_shape = jax.ShapeDtypeStruct((tok.rows, width), BF16)
    aliases = {}
    if latent:
        out_specs, out_shape = [y_spec], [y_shape]
        aliases[len(args)] = 0
        spec, arg = _in_place([y_buf])
        in_specs += spec
        args += arg
    else:
        out_specs = [y_spec, st_spec]
        out_shape = [y_shape, jax.ShapeDtypeStruct((nseq, 2, 1, width), F32)]
    return pl.pallas_call(
        functools.partial(_lru_kernel, seq_len=seq_len, zero_init=not latent, emit_state=not latent,
                          n_alias=len(aliases)),
        grid=(nseq, LRU_BLOCKS),
        in_specs=in_specs,
        out_specs=out_specs,
        out_shape=out_shape,
        input_output_aliases=aliases,
        scratch_shapes=[
            pltpu.VMEM((2, seq_len, bw), F32),
            pltpu.VMEM((2, seq_len, bw), F32),
            pltpu.VMEM((2, seq_len, bw), F32),
        ],
        compiler_params=_cparams("arbitrary", "arbitrary"),
        name="lru_scan_latent" if latent else "lru_scan_prompt",
    )(*args)


def _prep_gla(w_g1, w_g2, b_g, norm_g):
    qkw = w_g2.shape[2]
    g1 = jnp.concatenate([w_g1[0], w_g1[1]], axis=1)
    g1 = jnp.pad(g1, ((0, 0), (0, _LANES - 2 * GLA_RANK)))
    w2 = jnp.zeros((2, _LANES, qkw), F32)
    w2 = w2.at[0, 0:GLA_RANK].set(w_g2[0]).at[1, GLA_RANK:2 * GLA_RANK].set(w_g2[1])
    return {
        "w_g1cat": g1.astype(BF16), "w2": w2.astype(BF16),
        "bg": b_g.reshape(2, 1, qkw), "ng": norm_g.reshape(1, -1),
    }


def _prep_ssd(w_in_dt, conv_w, conv_b, a_log, dt_bias, d_skip, norm_g):
    heads = a_log.shape[1]
    di = heads * SSD_HEADDIM
    hpg = heads // SSD_GROUPS
    gw = di // SSD_GROUPS
    main = 2 * di + 2 * SSD_GROUPS * SSD_STATE

    ncopy = 3
    pad = _LANES - ncopy * 2 * hpg

    def by_group(t):
        t = t.reshape(2, SSD_GROUPS, hpg).transpose(1, 0, 2).reshape(SSD_GROUPS, 1, 2 * hpg)
        return jnp.pad(jnp.tile(t, (1, 1, ncopy)), ((0, 0), (0, 0), (0, pad)))

    w_dt = w_in_dt.reshape(-1, 2, SSD_GROUPS, hpg).transpose(0, 2, 1, 3)
    w_dt = jnp.tile(w_dt.reshape(-1, SSD_GROUPS, 2 * hpg), (1, 1, ncopy))
    w_dt = jnp.pad(w_dt, ((0, 0), (0, 0), (0, pad))).reshape(-1, SSD_GROUPS * _LANES)
    lane = jnp.arange(_LANES)[:, None]
    chan = jnp.arange(gw)[None, :] // SSD_HEADDIM
    expand = jnp.stack([(lane % (2 * hpg) == e * hpg + chan) & (lane < ncopy * 2 * hpg)
                        for e in range(2)]).astype(BF16)
    return {
        "di": di, "w_dt": w_dt.astype(BF16), "n_main": main,
        "conv_w": conv_w, "conv_b": conv_b.reshape(1, -1),
        "dt_bias": by_group(dt_bias), "a_log": by_group(a_log),
        "d_skip": jnp.repeat(d_skip, SSD_HEADDIM).reshape(1, di), "ng": norm_g.reshape(1, di),
        "expand": expand,
    }


def _prep_lru(conv_w, conv_b, w_a, b_a, w_i, b_i, lam):
    width = conv_w.shape[1]
    return {
        "conv_w": conv_w, "conv_b": conv_b.reshape(1, width),
        "w_a": w_a.astype(BF16), "b_a": b_a.reshape(2, 1, width),
        "w_i": w_i.astype(BF16), "b_i": b_i.reshape(2, 1, width), "lam": lam.reshape(2, 1, width),
    }


def kernel(x_prompt, x_sample, state_gla, state_ssd, state_lru, c, c_ctx, w_mod, b_mod, norm_mix_g,
           norm_ffn_g, ffn_w_up, ffn_conv_w, ffn_conv_b, ffn_w_down, final_norm_g, gla_w_in, gla_w_g1,
           gla_w_g2, gla_b_g, gla_norm_g, gla_w_out, ssd_w_in, ssd_conv_w, ssd_conv_b, ssd_a_log,
           ssd_dt_bias, ssd_d, ssd_norm_g, ssd_w_out, lru_w_in, lru_conv_w, lru_conv_b, lru_w_a, lru_b_a,
           lru_w_i, lru_b_i, lru_lambda, lru_w_out):
    bp, lp, d = x_prompt.shape
    bs, ls, _ = x_sample.shape
    depth = w_mod.shape[0]
    tok = _Tokens(bp, lp, bs, ls)
    assert bs + 1 <= MOD_ROWS and ls // GRID_W * GRID_W == ls

    x = (x_prompt.reshape(bp * lp, d), x_sample.reshape(bs * ls, d))
    c_rows = jnp.concatenate([c_ctx[None], c, jnp.zeros((MOD_ROWS - 1 - bs, d), F32)], axis=0)
    mod = _mod_table(c_rows, w_mod, b_mod).reshape(depth * MOD_ROWS, MOD_CHUNKS, d)

    ffn_w_down_bf = ffn_w_down.astype(BF16)
    ssd_w_out_bf = ssd_w_out.astype(BF16)
    gla_states, ssd_new, lru_new = None, [], []
    for l in range(depth):
        kind, j = l % 3, l // 3
        h = _norm_mod(x, norm_mix_g[l], mod, l, tok, shift_idx=0, scale_idx=1)
        if kind == 0:
            w = _prep_gla(gla_w_g1[j], gla_w_g2[j], gla_b_g[j], gla_norm_g[j])
            y, gla_states = _gla_mixer(h, state_gla[:, j], gla_w_in, j, w, tok, gla_states)
            w_out = gla_w_out
        elif kind == 1:
            n_main = ssd_conv_w.shape[2] + ssd_d.shape[1] * SSD_HEADDIM
            w = _prep_ssd(ssd_w_in[j, :, n_main:], ssd_conv_w[j], ssd_conv_b[j], ssd_a_log[j],
                          ssd_dt_bias[j], ssd_d[j], ssd_norm_g[j])
            proj = _matmul(h, ssd_w_in, j, col_start=0, n_cols=n_main, bm=1024, bn=1024,
                           name="ssd_in_main")
            dtp = _matmul(h, w["w_dt"][None], 0, bm=1024, bn=SSD_GROUPS * _LANES, name="ssd_in_dt")
            gw = w["di"] // SSD_GROUPS
            s_lat0 = state_ssd[:, j].reshape(bs, 2, SSD_GROUPS, gw, SSD_STATE)
            y, s_new = _ssd_scan(proj, dtp, w, None, tok, latent=False)
            (y,) = _ssd_scan(proj, dtp, w, s_lat0, tok, latent=True, y_buf=y)
            ssd_new.append(s_new.reshape(bp, 2, -1, SSD_HEADDIM, SSD_STATE))
            w_out = ssd_w_out_bf
        else:
            w = _prep_lru(lru_conv_w[j], lru_conv_b[j], lru_w_a[j], lru_b_a[j],
                          lru_w_i[j], lru_b_i[j], lru_lambda[j])
            proj = _matmul(h, lru_w_in, j, bm=1024, bn=1024, name="lru_in")
            s_lat0 = state_lru[:, j].reshape(bs, 2, 1, -1)
            y, s_new = _lru_scan(proj, w, None, tok, latent=False)
            (y,) = _lru_scan(proj, w, s_lat0, tok, latent=True, y_buf=y)
            lru_new.append(s_new.reshape(bp, 2, -1))
            w_out = lru_w_out
        x = _matmul(y, w_out, j, bm=512 if isinstance(x, tuple) else 1024, bn=1024,
                    bk=min(w_out.shape[1], 2048), residual=(x, mod, l, tok, 2), name="mixer_out")
        h = _norm_mod(x, norm_ffn_g[l], mod, l, tok, shift_idx=3, scale_idx=4)
        g = _ffn_up(h, ffn_w_up, ffn_conv_w, ffn_conv_b, l, tok)
        x = _matmul(g, ffn_w_down_bf, l, bm=1024, bn=1024, bk=ffn_w_down.shape[1] // 2,
                    residual=(x, mod, l, tok, 5), name="ffn_down")
    y_prompt, y_sample = _final_norm(x, final_norm_g, tok)
    return (y_prompt.reshape(bp, lp, d), y_sample.reshape(bs, ls, d), gla_states,
            jnp.stack(ssd_new, axis=1), jnp.stack(lru_new, axis=1))
```

```python
import functools

import jax
import jax.numpy as jnp
from jax import lax
from jax.experimental import pallas as pl
from jax.experimental.pallas import tpu as pltpu

F32 = jnp.float32
BF16 = jnp.bfloat16

_V7X_VMEM_BYTES = 64 * 1024 * 1024
_VMEM_LIMIT = _V7X_VMEM_BYTES - 8 * 1024 * 1024
_SUBLANES = 8
_LANES = 128

NORM_EPS = 1e-6
MOD_CHUNKS = 6
MOD_ROWS = 8
GRID_W = 64

GLA_HEADS = 4
GLA_RANK = 16
GLA_GATE_NORM = 16.0
SSD_HEADDIM = 64
SSD_GROUPS = 8
SSD_STATE = 128
LRU_BLOCKS = 8
LRU_C = 8.0
CHUNK = 64


def _cparams(*sem):
    return pltpu.CompilerParams(dimension_semantics=sem, vmem_limit_bytes=_VMEM_LIMIT)


def _sigmoid(x):
    return 0.5 + 0.5 * jnp.tanh(0.5 * x)


def _silu_of_half(h):
    return h + h * jnp.tanh(h)


def _silu(x):
    return _silu_of_half(0.5 * x)


def _softplus(x):
    return jnp.maximum(x, 0.0) + jnp.log1p(jnp.exp(-jnp.abs(x)))


def _log_sigmoid(x):
    return jnp.minimum(x, 0.0) - jnp.log1p(jnp.exp(-jnp.abs(x)))


def _gelu_tanh(x):
    return 0.5 * x * (1.0 + jnp.tanh(0.7978845608028654 * (x + 0.044715 * (x * x * x))))


def _neg_expm1_2x(x):
    t = jnp.tanh(x)
    return -2.0 * t / (1.0 - t)


def _split3(x):
    x1 = x.astype(BF16)
    r1 = x - x1.astype(F32)
    x2 = r1.astype(BF16)
    x3 = (r1 - x2.astype(F32)).astype(BF16)
    return x1, x2, x3


def _dot(a, b):
    return jnp.dot(a, b, preferred_element_type=F32)


def _dot_nt(a, b):
    return lax.dot_general(a, b, (((1,), (1,)), ((), ())), preferred_element_type=F32)


def _dot_tn(a, b):
    return lax.dot_general(a, b, (((0,), (0,)), ((), ())), preferred_element_type=F32)


def _exact_dot_left01(m01, x):
    x1, x2, x3 = _split3(x)
    return _dot(m01, x1) + _dot(m01, x2) + _dot(m01, x3)


def _exact_select_copies(x, sel01, ncopy_lanes):
    t1 = x.astype(BF16).astype(F32)
    r1 = x - t1
    t2 = r1.astype(BF16).astype(F32)
    lane = lax.broadcasted_iota(jnp.int32, x.shape, 1)
    terms = jnp.where(lane < ncopy_lanes, t1, jnp.where(lane < 2 * ncopy_lanes, t2, r1 - t2))
    return _dot(terms.astype(BF16), sel01)


def _chunk_masks():
    ii = lax.broadcasted_iota(jnp.int32, (CHUNK, CHUNK), 0)
    jj = lax.broadcasted_iota(jnp.int32, (CHUNK, CHUNK), 1)
    masks = (jj <= ii, jj >= ii)
    tris = tuple(jnp.where(m, 1.0, 0.0).astype(BF16) for m in masks)
    return masks, tris


def _block_chunk_masks(rows):
    ii = lax.broadcasted_iota(jnp.int32, (rows, rows), 0)
    jj = lax.broadcasted_iota(jnp.int32, (rows, rows), 1)
    shift = CHUNK.bit_length() - 1
    same = jnp.where(jnp.right_shift(ii, shift) == jnp.right_shift(jj, shift), 1, 0)
    masks = (same * jnp.where(jj <= ii, 1, 0) > 0, same * jnp.where(jj >= ii, 1, 0) > 0)
    tris = tuple(jnp.where(m, 1.0, 0.0).astype(BF16) for m in masks)
    return masks, tris


def _shifted_rows(ref, r0, nrows, off, total):
    lo, hi = r0 + off, r0 + off + nrows
    clo, chi = max(lo, 0), min(hi, total)
    x = ref[clo:chi, :]
    parts = []
    if clo > lo:
        parts.append(jnp.zeros((clo - lo, x.shape[1]), x.dtype))
    parts.append(x)
    if hi > chi:
        parts.append(jnp.zeros((hi - chi, x.shape[1]), x.dtype))
    return x if len(parts) == 1 else jnp.concatenate(parts, axis=0)


def _conv4_block(ref, w_ref, b_ref, r0, nrows, total, scale=1.0):
    w, b = w_ref[...], b_ref[...]
    if scale != 1.0:
        w, b = scale * w, scale * b
    acc = b + w[0:1, :] * _shifted_rows(ref, r0, nrows, -1, total)
    acc = acc + w[1:2, :] * ref[r0:r0 + nrows, :]
    acc = acc + w[2:3, :] * _shifted_rows(ref, r0, nrows, 1, total)
    acc = acc + w[3:4, :] * _shifted_rows(ref, r0, nrows, 2, total)
    return acc


def _mod_kernel(c_ref, w_ref, b_ref, o_ref):
    c = c_ref[...]
    s = _silu(c).astype(BF16)
    o_ref[0] = _dot(s, w_ref[0].astype(BF16)) + b_ref[0]


def _mod_table(c_rows, w_mod, b_mod):
    depth, d, n = w_mod.shape
    bn = 1024
    return pl.pallas_call(
        _mod_kernel,
        grid=(depth, n // bn),
        in_specs=[
            pl.BlockSpec((MOD_ROWS, d), lambda l, j: (0, 0)),
            pl.BlockSpec((1, d, bn), lambda l, j: (l, 0, j)),
            pl.BlockSpec((1, 1, bn), lambda l, j: (l, 0, j)),
        ],
        out_specs=pl.BlockSpec((1, MOD_ROWS, bn), lambda l, j: (l, 0, j)),
        out_shape=jax.ShapeDtypeStruct((depth, MOD_ROWS, n), F32),
        compiler_params=_cparams("arbitrary", "arbitrary"),
        name="adaln_table",
    )(c_rows, w_mod, b_mod.reshape(depth, 1, n))


class _Tokens:
    def __init__(self, n_prompt, l_prompt, n_latent, l_latent):
        self.bp, self.lp, self.bs, self.ls = n_prompt, l_prompt, n_latent, l_latent
        self.np_rows = n_prompt * l_prompt
        self.ns_rows = n_latent * l_latent
        self.rows = self.np_rows + self.ns_rows

    def mod_row(self, i, bm):
        r0 = i * bm
        return jnp.where(r0 < self.np_rows, 0, 1 + (r0 - self.np_rows) // self.ls)


_NORM_ROWS = 2 * _SUBLANES


def _rmsnorm_rows(x_ref, g_ref, r0):
    x = x_ref[pl.ds(r0, _NORM_ROWS), :]
    ms = jnp.mean(x * x, axis=-1, keepdims=True)
    return x * lax.rsqrt(ms + NORM_EPS) * g_ref[...]


def _row_specs(x, bm, bn, tok, row_of, col_of):
    if not isinstance(x, tuple):
        return [pl.BlockSpec((bm, bn), lambda *g: (row_of(*g), col_of(*g)))], [x]
    npt = tok.np_rows // bm
    return ([pl.BlockSpec((bm, bn), lambda *g: (jnp.minimum(row_of(*g), npt - 1), col_of(*g))),
             pl.BlockSpec((bm, bn), lambda *g: (jnp.maximum(row_of(*g) - npt, 0), col_of(*g)))],
            list(x))


def _on_row_source(x_refs, row_tile, n_prompt_tiles, fn):
    if len(x_refs) == 1:
        fn(x_refs[0])
        return

    @pl.when(row_tile < n_prompt_tiles)
    def _():
        fn(x_refs[0])

    @pl.when(row_tile >= n_prompt_tiles)
    def _():
        fn(x_refs[1])


def _norm_mod_kernel(*refs, shift_idx, scale_idx, n_prompt_tiles):
    *x_refs, g_ref, mod_ref, h_ref = refs

    def normalise(x_ref):
        def body(c, carry):
            r0 = pl.multiple_of(c * _NORM_ROWS, _NORM_ROWS)
            y = _rmsnorm_rows(x_ref, g_ref, r0)
            h = y * (1.0 + mod_ref[0, scale_idx:scale_idx + 1, :]) + mod_ref[0, shift_idx:shift_idx + 1, :]
            h_ref[pl.ds(r0, _NORM_ROWS), :] = h.astype(h_ref.dtype)
            return carry

        lax.fori_loop(0, x_ref.shape[0] // _NORM_ROWS, body, 0, unroll=8)

    _on_row_source(x_refs, pl.program_id(0), n_prompt_tiles, normalise)


def _norm_mod(x, g, mod, layer, tok, *, shift_idx, scale_idx):
    d = g.shape[0]
    bm = 1024
    x_specs, x_args = _row_specs(x, bm, d, tok, lambda i: i, lambda i: 0)
    return pl.pallas_call(
        functools.partial(_norm_mod_kernel, shift_idx=shift_idx, scale_idx=scale_idx,
                          n_prompt_tiles=tok.np_rows // bm),
        grid=(tok.rows // bm,),
        in_specs=x_specs + [
            pl.BlockSpec((1, d), lambda i: (0, 0)),
            pl.BlockSpec((1, MOD_CHUNKS, d), lambda i: (layer * MOD_ROWS + tok.mod_row(i, bm), 0, 0)),
        ],
        out_specs=pl.BlockSpec((bm, d), lambda i: (i, 0)),
        out_shape=jax.ShapeDtypeStruct((tok.rows, d), BF16),
        compiler_params=_cparams("arbitrary"),
        name="norm_adaln",
    )(*x_args, g.reshape(1, d), mod)


def _final_norm_kernel(x_ref, g_ref, op_ref, os_ref, *, n_prompt_tiles):
    def norm_into(o_ref):
        def body(c, carry):
            r0 = pl.multiple_of(c * _NORM_ROWS, _NORM_ROWS)
            o_ref[pl.ds(r0, _NORM_ROWS), :] = _rmsnorm_rows(x_ref, g_ref, r0)
            return carry

        lax.fori_loop(0, x_ref.shape[0] // _NORM_ROWS, body, 0, unroll=8)

    @pl.when(pl.program_id(0) < n_prompt_tiles)
    def _():
        norm_into(op_ref)

    @pl.when(pl.program_id(0) >= n_prompt_tiles)
    def _():
        norm_into(os_ref)


def _final_norm(x, g, tok):
    rows, d = x.shape
    bm = 1024
    npt = tok.np_rows // bm
    return pl.pallas_call(
        functools.partial(_final_norm_kernel, n_prompt_tiles=npt),
        grid=(rows // bm,),
        in_specs=[pl.BlockSpec((bm, d), lambda i: (i, 0)), pl.BlockSpec((1, d), lambda i: (0, 0))],
        out_specs=[pl.BlockSpec((bm, d), lambda i: (jnp.minimum(i, npt - 1), 0)),
                   pl.BlockSpec((bm, d), lambda i: (jnp.maximum(i - npt, 0), 0))],
        out_shape=[jax.ShapeDtypeStruct((tok.np_rows, d), F32),
                   jax.ShapeDtypeStruct((tok.ns_rows, d), F32)],
        compiler_params=_cparams("arbitrary"),
        name="final_norm",
    )(x, g.reshape(1, d))


def _mm_kernel(*refs, nk, gate_idx, n_prompt_tiles):
    if gate_idx is None:
        x_ref, w_ref, o_ref, acc_ref, wb_ref = refs
        res_refs, mod_ref = [], None
    else:
        x_ref, w_ref, *res_refs, mod_ref, o_ref, acc_ref, wb_ref = refs
    i, k = pl.program_id(1), pl.program_id(2)
    if w_ref.dtype == BF16:
        w = w_ref[0]
    elif nk == 1:
        @pl.when(i == 0)
        def _():
            wb_ref[...] = w_ref[0].astype(BF16)

        w = wb_ref[...]
    else:
        w = w_ref[0].astype(BF16)
    part = _dot(x_ref[...], w)

    def finish(acc):
        if gate_idx is None:
            o_ref[...] = acc.astype(o_ref.dtype)
            return

        def gated_residual(res_ref):
            o_ref[...] = res_ref[...] + mod_ref[0, gate_idx:gate_idx + 1, :] * acc

        _on_row_source(res_refs, i, n_prompt_tiles, gated_residual)

    if nk == 1:
        finish(part)
        return

    @pl.when(k == 0)
    def _():
        acc_ref[...] = part

    @pl.when(jnp.logical_and(k > 0, k < nk - 1))
    def _():
        acc_ref[...] += part

    @pl.when(k == nk - 1)
    def _():
        finish(acc_ref[...] + part)


def _matmul(x, w, layer, *, col_start=0, n_cols=None, bm, bn, bk=None, out_dtype=F32, residual=None,
            name="matmul"):
    m, kdim = x.shape
    n_cols = w.shape[2] - col_start if n_cols is None else n_cols
    bk = kdim if bk is None else bk
    nk = kdim // bk
    off = col_start // bn
    in_specs = [
        pl.BlockSpec((bm, bk), lambda j, i, k: (i, k)),
        pl.BlockSpec((1, bk, bn), lambda j, i, k: (layer, k, j + off)),
    ]
    args = [x, w]
    gate_idx, n_prompt_tiles = None, 0
    if residual is not None:
        res, mod, mod_layer, tok, gate_idx = residual
        n_prompt_tiles = tok.np_rows // bm
        res_specs, res_args = _row_specs(res, bm, bn, tok, lambda j, i, k: i, lambda j, i, k: j)
        in_specs += res_specs + [
            pl.BlockSpec((1, MOD_CHUNKS, bn),
                         lambda j, i, k: (mod_layer * MOD_ROWS + tok.mod_row(i, bm), 0, j)),
        ]
        args += res_args + [mod]
    cast_once = w.dtype != BF16 and nk == 1
    tiny = (2 * _SUBLANES, _LANES)
    return pl.pallas_call(
        functools.partial(_mm_kernel, nk=nk, gate_idx=gate_idx, n_prompt_tiles=n_prompt_tiles),
        grid=(n_cols // bn, m // bm, nk),
        in_specs=in_specs,
        out_specs=pl.BlockSpec((bm, bn), lambda j, i, k: (i, j)),
        out_shape=jax.ShapeDtypeStruct((m, n_cols), out_dtype),
        scratch_shapes=[pltpu.VMEM((bm, bn) if nk > 1 else tiny, F32),
                        pltpu.VMEM((bk, bn) if cast_once else tiny, BF16)],
        compiler_params=_cparams("arbitrary", "arbitrary", "arbitrary"),
        name=name,
    )(*args)


def _ffn_up_kernel(h_ref, wa_ref, wv_ref, cw_ref, cb_ref, g_ref, *, n_prompt_tiles, w_prompt, w_latent):
    i = pl.program_id(0)
    x = h_ref[...]
    act = _dot(x, wa_ref[0].astype(BF16))
    val = _dot(x, wv_ref[0].astype(BF16))
    t, tn = act.shape
    row = lax.broadcasted_iota(jnp.int32, (t, tn), 0)
    prev = pltpu.roll(act, 1, 0)
    nxt = pltpu.roll(act, t - 1, 0)

    def horizontal(width):
        col = row & (width - 1)
        return jnp.where(col > 0, prev, 0.0), jnp.where(col < width - 1, nxt, 0.0)

    half_w = 0.5 * cw_ref[...]
    half_b = 0.5 * cb_ref[...]

    def tap_row(kh, left, right):
        return (half_w[3 * kh:3 * kh + 1, :] * left + half_w[3 * kh + 1:3 * kh + 2, :] * act
                + half_w[3 * kh + 2:3 * kh + 3, :] * right)

    def gate(half_conv):
        g_ref[...] = (_silu_of_half(half_conv) * val).astype(g_ref.dtype)

    @pl.when(i < n_prompt_tiles)
    def _():
        left, right = horizontal(w_prompt)
        gate(tap_row(1, left, right) + half_b)

    @pl.when(i >= n_prompt_tiles)
    def _():
        left, right = horizontal(w_latent)
        zeros = jnp.zeros((w_latent, tn), F32)
        above = jnp.concatenate([zeros, tap_row(0, left, right)[:t - w_latent, :]], axis=0)
        below = jnp.concatenate([tap_row(2, left, right)[w_latent:, :], zeros], axis=0)
        gate(above + tap_row(1, left, right) + below + half_b)


def _ffn_up(h, w_up, conv_w, conv_b, layer, tok):
    rows, d = h.shape
    f = w_up.shape[2] // 2
    tm = tok.ls
    tn = 256
    assert tok.np_rows % tm == 0 and tm % tok.lp == 0
    nj = f // tn
    return pl.pallas_call(
        functools.partial(_ffn_up_kernel, n_prompt_tiles=tok.np_rows // tm,
                          w_prompt=tok.lp, w_latent=GRID_W),
        grid=(rows // tm, nj),
        in_specs=[
            pl.BlockSpec((tm, d), lambda i, j: (i, 0)),
            pl.BlockSpec((1, d, tn), lambda i, j: (layer, 0, j)),
            pl.BlockSpec((1, d, tn), lambda i, j: (layer, 0, j + nj)),
            pl.BlockSpec((9, tn), lambda i, j: (0, j)),
            pl.BlockSpec((1, tn), lambda i, j: (0, j)),
        ],
        out_specs=pl.BlockSpec((tm, tn), lambda i, j: (i, j)),
        out_shape=jax.ShapeDtypeStruct((rows, f), BF16),
        compiler_params=_cparams("arbitrary", "arbitrary"),
        name="ffn_up_conv",
    )(h, w_up, w_up, conv_w[layer].reshape(9, f), conv_b[layer].reshape(1, f))


def _gla_kernel(*refs, seq_len, zero_init, emit_state, n_alias, dk, dv):
    it = iter(refs)
    q_ref, k_ref, v_ref, r_ref, low_ref, w2_ref, bg_ref, ng_ref = (next(it) for _ in range(8))
    s0_ref = None if zero_init else next(it)
    for _ in range(n_alias):
        next(it)
    y_ref = next(it)
    sout_ref = next(it).at[0, 0] if emit_state else None
    g_scr, o_scr, s_scr = next(it), next(it), next(it)

    nc = seq_len // CHUNK
    blk = 256
    cpb = blk // CHUNK
    nb = seq_len // blk
    masks, tris = _block_chunk_masks(blk)
    mid_row = (CHUNK // 2, CHUNK // 2 - 1)
    end_row = (CHUNK - 1, 0)
    q_scale = dk ** -0.5

    low = low_ref[...].astype(BF16)
    for e in range(2):
        z = _dot(low, w2_ref[e]) + bg_ref[e]
        g_scr[e] = _log_sigmoid(z) * (1.0 / GLA_GATE_NORM)
        if zero_init:
            s_scr[e] = jnp.zeros((dv, dk), F32)
        else:
            s_scr[e] = s0_ref[0, e, 0].T

    def block_step(e, b):
        r0 = pl.multiple_of(b * blk, blk)
        gc = _exact_dot_left01(tris[e], g_scr[e, pl.ds(r0, blk), :])
        g_scr[e, pl.ds(r0, blk), :] = gc
        g_mid = jnp.concatenate(
            [jnp.broadcast_to(gc[c * CHUNK + mid_row[e]:c * CHUNK + mid_row[e] + 1, :], (CHUNK, dk))
             for c in range(cpb)], axis=0)
        q = q_ref[pl.ds(r0, blk), :] * q_scale
        k = k_ref[pl.ds(r0, blk), :]
        scores = _dot_nt((q * jnp.exp(gc - g_mid)).astype(BF16), (k * jnp.exp(g_mid - gc)).astype(BF16))
        o_scr[e, pl.ds(r0, blk), :] = _dot(jnp.where(masks[e], scores, 0.0).astype(BF16),
                                            v_ref[pl.ds(r0, blk), :])

    def block_body(b, carry):
        block_step(0, b)
        block_step(1, b)
        return carry

    lax.fori_loop(0, nb, block_body, 0, unroll=min(4, nb))

    def chunk_step(e, c):
        r0 = pl.multiple_of(c * CHUNK, CHUNK)
        gc = g_scr[e, pl.ds(r0, CHUNK), :]
        g_end = gc[end_row[e]:end_row[e] + 1, :]
        q = q_ref[pl.ds(r0, CHUNK), :] * q_scale
        k = k_ref[pl.ds(r0, CHUNK), :]
        v = v_ref[pl.ds(r0, CHUNK), :]
        s_t = s_scr[e]
        o_scr[e, pl.ds(r0, CHUNK), :] += _dot_nt((q * jnp.exp(gc)).astype(BF16), s_t.astype(BF16))
        u_t = _dot_tn(v, (k * jnp.exp(g_end - gc)).astype(BF16))
        s_scr[e] = s_t * jnp.exp(g_end) + u_t

    def body(i, carry):
        chunk_step(0, i)
        chunk_step(1, nc - 1 - i)
        return carry

    lax.fori_loop(0, nc, body, 0, unroll=min(8, nc))

    for r0 in range(0, seq_len, blk):
        o = o_scr[0, r0:r0 + blk, :] + o_scr[1, r0:r0 + blk, :]
        ms = jnp.mean(o * o, axis=-1, keepdims=True)
        y = o * lax.rsqrt(ms + NORM_EPS) * ng_ref[...]
        y_ref[r0:r0 + blk, :] = (y * _silu(r_ref[r0:r0 + blk, :])).astype(y_ref.dtype)

    if emit_state:
        for e in range(2):
            sout_ref[e, 0] = s_scr[e].T


def _in_place(buffers):
    return [pl.BlockSpec(memory_space=pl.ANY) for _ in buffers], list(buffers)


def _gla_scan(qk, v, r, low, w2, bg, ng, s0, tok, *, latent, y_buf=None, state_buf=None, layer=0,
              n_layers=1):
    dk = qk.shape[1] // (2 * GLA_HEADS)
    dv = v.shape[1] // GLA_HEADS
    if latent:
        nseq, seq_len, rb0 = tok.bs, tok.ls, tok.np_rows // tok.ls
    else:
        nseq, seq_len, rb0 = tok.bp, tok.lp, 0
    in_specs = [
        pl.BlockSpec((seq_len, dk), lambda b, h: (rb0 + b, h)),
        pl.BlockSpec((seq_len, dk), lambda b, h: (rb0 + b, GLA_HEADS + h)),
        pl.BlockSpec((seq_len, dv), lambda b, h: (rb0 + b, h)),
        pl.BlockSpec((seq_len, dv), lambda b, h: (rb0 + b, h)),
        pl.BlockSpec((seq_len, _LANES), lambda b, h: (rb0 + b, 0)),
        pl.BlockSpec((2, _LANES, dk), lambda b, h: (0, 0, h)),
        pl.BlockSpec((2, 1, dk), lambda b, h: (0, 0, h)),
        pl.BlockSpec((1, dv), lambda b, h: (0, 0)),
    ]
    args = [qk, qk, v, r, low, w2, bg, ng]
    if latent:
        in_specs.append(pl.BlockSpec((1, 2, 1, dk, dv), lambda b, h: (b, 0, h, 0, 0)))
        args.append(s0)
    y_spec = pl.BlockSpec((seq_len, dv), lambda b, h: (rb0 + b, h))
    y_shape = jax.ShapeDtypeStruct((tok.rows, v.shape[1]), BF16)
    if latent:
        out_specs, out_shape, bufs = [y_spec], [y_shape], [y_buf]
    else:
        out_specs = [y_spec, pl.BlockSpec((1, 1, 2, 1, dk, dv), lambda b, h: (b, layer, 0, h, 0, 0))]
        out_shape = [y_shape, jax.ShapeDtypeStruct((nseq, n_layers, 2, GLA_HEADS, dk, dv), F32)]
        bufs = [y_buf, state_buf]
    aliases = {}
    for o, buf in enumerate(bufs):
        if buf is not None:
            aliases[len(args)] = o
            spec, arg = _in_place([buf])
            in_specs += spec
            args += arg
    return pl.pallas_call(
        functools.partial(_gla_kernel, seq_len=seq_len, zero_init=not latent, emit_state=not latent,
                          n_alias=len(aliases), dk=dk, dv=dv),
        grid=(nseq, GLA_HEADS),
        in_specs=in_specs,
        out_specs=out_specs,
        out_shape=out_shape,
        input_output_aliases=aliases,
        scratch_shapes=[
            pltpu.VMEM((2, seq_len, dk), F32),
            pltpu.VMEM((2, seq_len, dv), F32),
            pltpu.VMEM((2, dv, dk), F32),
        ],
        compiler_params=_cparams("arbitrary", "arbitrary"),
        name="gla_scan_latent" if latent else "gla_scan_prompt",
    )(*args)


def _gla_mixer(h, s_lat0, w_in, layer, w, tok, state_buf):
    w_g1cat, w2, bg, ng = w["w_g1cat"], w["w2"], w["bg"], w["ng"]
    qkw = 2 * w2.shape[2]
    vw = (w_in.shape[2] - qkw) // 2
    qk = _matmul(h, w_in, layer, col_start=0, n_cols=qkw, bm=1024, bn=1024, name="gla_in_qk")
    v = _matmul(h, w_in, layer, col_start=qkw, n_cols=vw, bm=1024, bn=1024, out_dtype=BF16,
                name="gla_in_v")
    r = _matmul(h, w_in, layer, col_start=qkw + vw, n_cols=vw, bm=1024, bn=1024, name="gla_in_r")
    low = _matmul(h, w_g1cat[None], 0, bm=1024, bn=_LANES, name="gla_in_gate")
    y, states = _gla_scan(qk, v, r, low, w2, bg, ng, None, tok, latent=False, state_buf=state_buf,
                          layer=layer, n_layers=w_in.shape[0])
    (y,) = _gla_scan(qk, v, r, low, w2, bg, ng, s_lat0, tok, latent=True, y_buf=y)
    return y, states


def _ssd_kernel(*refs, seq_len, zero_init, emit_state, n_alias):
    it = iter(refs)
    (z_ref, x_ref, b_ref, c_ref, dt_ref, cwx_ref, cbx_ref, cwb_ref, cbb_ref, cwc_ref, cbc_ref,
     dtb_ref, alog_ref, dskip_ref, ng_ref, e_ref) = (next(it) for _ in range(16))
    s0_ref = None if zero_init else next(it)
    for _ in range(n_alias):
        next(it)
    y_ref = next(it)
    sout_ref = next(it) if emit_state else None
    xd_scr, cumx_scr, cumt_scr, bm_scr, cm_scr, y_scr, s_scr = (next(it) for _ in range(7))

    hpg = y_scr.shape[1] // SSD_HEADDIM
    nc = seq_len // CHUNK
    blk = 256
    cpb = blk // CHUNK
    masks, _ = _chunk_masks()
    _, blk_tris = _block_chunk_masks(blk)
    end_row = (CHUNK - 1, 0)
    a_row = -jnp.exp(alog_ref[0])

    for r0 in range(0, seq_len, blk):
        def conv_silu(ref, w_ref, bias_ref):
            return _silu_of_half(_conv4_block(ref, w_ref, bias_ref, r0, blk, seq_len, scale=0.5))

        xs = conv_silu(x_ref, cwx_ref, cbx_ref)
        y_scr[r0:r0 + blk, :] = dskip_ref[...] * xs
        bm_scr[r0:r0 + blk, :] = conv_silu(b_ref, cwb_ref, cbb_ref).astype(BF16)
        cm_scr[r0:r0 + blk, :] = conv_silu(c_ref, cwc_ref, cbc_ref).astype(BF16)
        dt = _softplus(dt_ref[r0:r0 + blk, :] + dtb_ref[0])
        for e in range(2):
            expand = e_ref[e]
            cum = _exact_dot_left01(blk_tris[e], dt * a_row)
            cum_t = cum.T
            for cc in range(cpb):
                cumt_scr[e, r0 // CHUNK + cc] = cum_t[0:2 * hpg, cc * CHUNK:(cc + 1) * CHUNK]
            cumx_scr[e, r0:r0 + blk, :] = _exact_select_copies(cum, expand, 2 * hpg)
            xd_scr[e, r0:r0 + blk, :] = xs * _exact_select_copies(dt, expand, 2 * hpg)

    for e in range(2):
        if zero_init:
            s_scr[e] = jnp.zeros(s_scr.shape[1:], F32)
        else:
            s_scr[e] = s0_ref[0, e, 0].T

    def chunk_step(e, c):
        r0 = pl.multiple_of(c * CHUNK, CHUNK)
        cum_x = cumx_scr[e, pl.ds(r0, CHUNK), :]
        cum_t = cumt_scr[e, c]
        xd = xd_scr[e, pl.ds(r0, CHUNK), :]
        xd_bf = xd.astype(BF16)
        bm = bm_scr[pl.ds(r0, CHUNK), :]
        cm = cm_scr[pl.ds(r0, CHUNK), :]
        cb = _dot_nt(cm, bm)
        parts = []
        for hh in range(hpg):
            lane = slice(hh * SSD_HEADDIM, (hh + 1) * SSD_HEADDIM)
            seg = cum_x[:, lane] - cum_t[e * hpg + hh:e * hpg + hh + 1, :]
            decay = jnp.where(masks[e], jnp.exp(jnp.where(masks[e], seg, 0.0)), 0.0)
            parts.append(_dot((cb * decay).astype(BF16), xd_bf[:, lane]))
        y = jnp.concatenate(parts, axis=1)
        s_t = s_scr[e]
        y = y + _dot(cm, s_t.astype(BF16)) * jnp.exp(cum_x)
        cum_end = cum_x[end_row[e]:end_row[e] + 1, :]
        st_t = _dot_tn(bm, (xd * jnp.exp(cum_end - cum_x)).astype(BF16))
        s_scr[e] = s_t * jnp.exp(cum_end) + st_t
        y_scr[pl.ds(r0, CHUNK), :] += y

    def body(i, carry):
        chunk_step(0, i)
        chunk_step(1, nc - 1 - i)
        return carry

    lax.fori_loop(0, nc, body, 0, unroll=min(8, nc))

    for r0 in range(0, seq_len, blk):
        y = y_scr[r0:r0 + blk, :] * _silu(z_ref[r0:r0 + blk, :])
        ms = jnp.mean(y * y, axis=-1, keepdims=True)
        y_ref[r0:r0 + blk, :] = (y * lax.rsqrt(ms + NORM_EPS) * ng_ref[...]).astype(y_ref.dtype)

    if emit_state:
        for e in range(2):
            sout_ref[0, e, 0] = s_scr[e].T


def _ssd_scan(proj, dtp, w, s0, tok, *, latent, y_buf=None):
    di = w["di"]
    gw = di // SSD_GROUPS
    if latent:
        nseq, seq_len, rb0 = tok.bs, tok.ls, tok.np_rows // tok.ls
    else:
        nseq, seq_len, rb0 = tok.bp, tok.lp, 0
    nx = di // gw
    nb = (2 * di) // SSD_STATE
    ncb = nb + SSD_GROUPS

    def rows(width, col):
        return pl.BlockSpec((seq_len, width), lambda b, g: (rb0 + b, col(g)))

    def vec(width, col, nrows=1):
        return pl.BlockSpec((nrows, width), lambda b, g: (0, col(g)))

    in_specs = [
        rows(gw, lambda g: g),
        rows(gw, lambda g: nx + g),
        rows(SSD_STATE, lambda g: nb + g),
        rows(SSD_STATE, lambda g: ncb + g),
        pl.BlockSpec((seq_len, _LANES), lambda b, g: (rb0 + b, g)),
        vec(gw, lambda g: g, 4), vec(gw, lambda g: g),
        vec(SSD_STATE, lambda g: di // SSD_STATE + g, 4), vec(SSD_STATE, lambda g: di // SSD_STATE + g),
        vec(SSD_STATE, lambda g: di // SSD_STATE + SSD_GROUPS + g, 4),
        vec(SSD_STATE, lambda g: di // SSD_STATE + SSD_GROUPS + g),
        pl.BlockSpec((1, 1, _LANES), lambda b, g: (g, 0, 0)),
        pl.BlockSpec((1, 1, _LANES), lambda b, g: (g, 0, 0)),
        vec(gw, lambda g: g),
        vec(gw, lambda g: g),
        pl.BlockSpec((2, _LANES, gw), lambda b, g: (0, 0, 0)),
    ]
    args = [proj, proj, proj, proj, dtp, w["conv_w"], w["conv_b"], w["conv_w"], w["conv_b"],
            w["conv_w"], w["conv_b"], w["dt_bias"], w["a_log"], w["d_skip"], w["ng"], w["expand"]]
    st_spec = pl.BlockSpec((1, 2, 1, gw, SSD_STATE), lambda b, g: (b, 0, g, 0, 0))
    if latent:
        in_specs.append(st_spec)
        args.append(s0)
    y_spec = pl.BlockSpec((seq_len, gw), lambda b, g: (rb0 + b, g))
    y_shape = jax.ShapeDtypeStruct((tok.rows, di), BF16)
    aliases = {}
    if latent:
        out_specs, out_shape = [y_spec], [y_shape]
        aliases[len(args)] = 0
        spec, arg = _in_place([y_buf])
        in_specs += spec
        args += arg
    else:
        out_specs = [y_spec, st_spec]
        out_shape = [y_shape, jax.ShapeDtypeStruct((nseq, 2, SSD_GROUPS, gw, SSD_STATE), F32)]

    return pl.pallas_call(
        functools.partial(_ssd_kernel, seq_len=seq_len, zero_init=not latent, emit_state=not latent,
                          n_alias=len(aliases)),
        grid=(nseq, SSD_GROUPS),
        in_specs=in_specs,
        out_specs=out_specs,
        out_shape=out_shape,
        input_output_aliases=aliases,
        scratch_shapes=[
            pltpu.VMEM((2, seq_len, gw), F32),
            pltpu.VMEM((2, seq_len, gw), F32),
            pltpu.VMEM((2, seq_len // CHUNK, 2 * gw // SSD_HEADDIM, CHUNK), F32),
            pltpu.VMEM((seq_len, SSD_STATE), BF16),
            pltpu.VMEM((seq_len, SSD_STATE), BF16),
            pltpu.VMEM((seq_len, gw), F32),
            pltpu.VMEM((2, SSD_STATE, gw), F32),
        ],
        compiler_params=_cparams("arbitrary", "arbitrary"),
        name="ssd_scan_latent" if latent else "ssd_scan_prompt",
    )(*args)


def _lru_kernel(*refs, seq_len, zero_init, emit_state, n_alias):
    it = iter(refs)
    (x_ref, gate_ref, cw_ref, cb_ref, wa_ref, ba_ref, wi_ref, bi_ref, lam_ref) = (next(it) for _ in range(9))
    s0_ref = None if zero_init else next(it)
    for _ in range(n_alias):
        next(it)
    y_ref = next(it)
    sout_ref = next(it) if emit_state else None
    a_scr, u_scr, h_scr = next(it), next(it), next(it)
    width = a_scr.shape[2]

    blk = 256
    for r0 in range(0, seq_len, blk):
        xc = _conv4_block(x_ref, cw_ref, cb_ref, r0, blk, seq_len)
        xc_bf = xc.astype(BF16)
        for e in range(2):
            rg = _sigmoid(_dot(xc_bf, wa_ref[e, 0]) + ba_ref[e])
            ig = _sigmoid(_dot(xc_bf, wi_ref[e, 0]) + bi_ref[e])
            log_a = -LRU_C * rg * _softplus(-lam_ref[e])
            a_scr[e, r0:r0 + blk, :] = jnp.exp(log_a)
            u_scr[e, r0:r0 + blk, :] = jnp.sqrt(_neg_expm1_2x(log_a)) * ig * xc

    nt = seq_len // _SUBLANES
    row = lax.broadcasted_iota(jnp.int32, (_SUBLANES, width), 0)

    def tile_scan(e, t, carry):
        r0 = pl.multiple_of(t * _SUBLANES, _SUBLANES)
        a = a_scr[e, pl.ds(r0, _SUBLANES), :]
        u = u_scr[e, pl.ds(r0, _SUBLANES), :]
        for s in (1, 2, 4):
            if e == 0:
                valid = row >= s
                a_sh, u_sh = pltpu.roll(a, s, 0), pltpu.roll(u, s, 0)
            else:
                valid = row < _SUBLANES - s
                a_sh, u_sh = pltpu.roll(a, _SUBLANES - s, 0), pltpu.roll(u, _SUBLANES - s, 0)
            u = u + a * jnp.where(valid, u_sh, 0.0)
            a = a * jnp.where(valid, a_sh, 1.0)
        hcur = u + a * carry
        last = _SUBLANES - 1 if e == 0 else 0
        return hcur, hcur[last:last + 1, :]

    def body(i, carry):
        cf, cbk = carry
        hf, cf = tile_scan(0, i, cf)
        r0 = pl.multiple_of(i * _SUBLANES, _SUBLANES)
        h_scr[0, pl.ds(r0, _SUBLANES), :] = hf
        tb = nt - 1 - i
        hb, cbk = tile_scan(1, tb, cbk)
        rb = pl.multiple_of(tb * _SUBLANES, _SUBLANES)
        h_scr[1, pl.ds(rb, _SUBLANES), :] = hb
        return cf, cbk

    if zero_init:
        init = (jnp.zeros((1, width), F32), jnp.zeros((1, width), F32))
    else:
        init = (s0_ref[0, 0], s0_ref[0, 1])
    cf, cbk = lax.fori_loop(0, nt, body, init, unroll=4)

    for r0 in range(0, seq_len, blk):
        hsum = h_scr[0, r0:r0 + blk, :] + h_scr[1, r0:r0 + blk, :]
        y_ref[r0:r0 + blk, :] = (hsum * _gelu_tanh(gate_ref[r0:r0 + blk, :])).astype(y_ref.dtype)

    if emit_state:
        sout_ref[0, 0] = cf
        sout_ref[0, 1] = cbk


def _lru_scan(proj, w, s0, tok, *, latent, y_buf=None):
    width = proj.shape[1] // 2
    bw = width // LRU_BLOCKS
    if latent:
        nseq, seq_len, rb0 = tok.bs, tok.ls, tok.np_rows // tok.ls
    else:
        nseq, seq_len, rb0 = tok.bp, tok.lp, 0
    pair = pl.BlockSpec((2, 1, bw), lambda b, n: (0, 0, n))
    in_specs = [
        pl.BlockSpec((seq_len, bw), lambda b, n: (rb0 + b, n)),
        pl.BlockSpec((seq_len, bw), lambda b, n: (rb0 + b, LRU_BLOCKS + n)),
        pl.BlockSpec((4, bw), lambda b, n: (0, n)),
        pl.BlockSpec((1, bw), lambda b, n: (0, n)),
        pl.BlockSpec((2, 1, bw, bw), lambda b, n: (0, n, 0, 0)), pair,
        pl.BlockSpec((2, 1, bw, bw), lambda b, n: (0, n, 0, 0)), pair,
        pair,
    ]
    args = [proj, proj, w["conv_w"], w["conv_b"], w["w_a"], w["b_a"], w["w_i"], w["b_i"], w["lam"]]
    st_spec = pl.BlockSpec((1, 2, 1, bw), lambda b, n: (b, 0, 0, n))
    if latent:
        in_specs.append(st_spec)
        args.append(s0)
    y_spec = pl.BlockSpec((seq_len, bw), lambda b, n: (rb0 + b, n))
    y_shape = jax.ShapeDtypeStruct((tok.rows, width), BF16)
    aliases = {}
    if latent:
        out_specs, out_shape = [y_spec], [y_shape]
        aliases[len(args)] = 0
        spec, arg = _in_place([y_buf])
        in_specs += spec
        args += arg
    else:
        out_specs = [y_spec, st_spec]
        out_shape = [y_shape, jax.ShapeDtypeStruct((nseq, 2, 1, width), F32)]
    return pl.pallas_call(
        functools.partial(_lru_kernel, seq_len=seq_len, zero_init=not latent, emit_state=not latent,
                          n_alias=len(aliases)),
        grid=(nseq, LRU_BLOCKS),
        in_specs=in_specs,
        out_specs=out_specs,
        out_shape=out_shape,
        input_output_aliases=aliases,
        scratch_shapes=[
            pltpu.VMEM((2, seq_len, bw), F32),
            pltpu.VMEM((2, seq_len, bw), F32),
            pltpu.VMEM((2, seq_len, bw), F32),
        ],
        compiler_params=_cparams("arbitrary", "arbitrary"),
        name="lru_scan_latent" if latent else "lru_scan_prompt",
    )(*args)


def _prep_gla(w_g1, w_g2, b_g, norm_g):
    qkw = w_g2.shape[2]
    g1 = jnp.concatenate([w_g1[0], w_g1[1]], axis=1)
    g1 = jnp.pad(g1, ((0, 0), (0, _LANES - 2 * GLA_RANK)))
    w2 = jnp.zeros((2, _LANES, qkw), F32)
    w2 = w2.at[0, 0:GLA_RANK].set(w_g2[0]).at[1, GLA_RANK:2 * GLA_RANK].set(w_g2[1])
    return {
        "w_g1cat": g1.astype(BF16), "w2": w2.astype(BF16),
        "bg": b_g.reshape(2, 1, qkw), "ng": norm_g.reshape(1, -1),
    }


def _prep_ssd(w_in_dt, conv_w, conv_b, a_log, dt_bias, d_skip, norm_g):
    heads = a_log.shape[1]
    di = heads * SSD_HEADDIM
    hpg = heads // SSD_GROUPS
    gw = di // SSD_GROUPS
    main = 2 * di + 2 * SSD_GROUPS * SSD_STATE

    ncopy = 3
    pad = _LANES - ncopy * 2 * hpg

    def by_group(t):
        t = t.reshape(2, SSD_GROUPS, hpg).transpose(1, 0, 2).reshape(SSD_GROUPS, 1, 2 * hpg)
        return jnp.pad(jnp.tile(t, (1, 1, ncopy)), ((0, 0), (0, 0), (0, pad)))

    w_dt = w_in_dt.reshape(-1, 2, SSD_GROUPS, hpg).transpose(0, 2, 1, 3)
    w_dt = jnp.tile(w_dt.reshape(-1, SSD_GROUPS, 2 * hpg), (1, 1, ncopy))
    w_dt = jnp.pad(w_dt, ((0, 0), (0, 0), (0, pad))).reshape(-1, SSD_GROUPS * _LANES)
    lane = jnp.arange(_LANES)[:, None]
    chan = jnp.arange(gw)[None, :] // SSD_HEADDIM
    expand = jnp.stack([(lane % (2 * hpg) == e * hpg + chan) & (lane < ncopy * 2 * hpg)
                        for e in range(2)]).astype(BF16)
    return {
        "di": di, "w_dt": w_dt.astype(BF16), "n_main": main,
        "conv_w": conv_w, "conv_b": conv_b.reshape(1, -1),
        "dt_bias": by_group(dt_bias), "a_log": by_group(a_log),
        "d_skip": jnp.repeat(d_skip, SSD_HEADDIM).reshape(1, di), "ng": norm_g.reshape(1, di),
        "expand": expand,
    }


def _prep_lru(conv_w, conv_b, w_a, b_a, w_i, b_i, lam):
    width = conv_w.shape[1]
    return {
        "conv_w": conv_w, "conv_b": conv_b.reshape(1, width),
        "w_a": w_a.astype(BF16), "b_a": b_a.reshape(2, 1, width),
        "w_i": w_i.astype(BF16), "b_i": b_i.reshape(2, 1, width), "lam": lam.reshape(2, 1, width),
    }


def kernel(x_prompt, x_sample, state_gla, state_ssd, state_lru, c, c_ctx, w_mod, b_mod, norm_mix_g,
           norm_ffn_g, ffn_w_up, ffn_conv_w, ffn_conv_b, ffn_w_down, final_norm_g, gla_w_in, gla_w_g1,
           gla_w_g2, gla_b_g, gla_norm_g, gla_w_out, ssd_w_in, ssd_conv_w, ssd_conv_b, ssd_a_log,
           ssd_dt_bias, ssd_d, ssd_norm_g, ssd_w_out, lru_w_in, lru_conv_w, lru_conv_b, lru_w_a, lru_b_a,
           lru_w_i, lru_b_i, lru_lambda, lru_w_out):
    bp, lp, d = x_prompt.shape
    bs, ls, _ = x_sample.shape
    depth = w_mod.shape[0]
    tok = _Tokens(bp, lp, bs, ls)
    assert bs + 1 <= MOD_ROWS and ls // GRID_W * GRID_W == ls

    x = (x_prompt.reshape(bp * lp, d), x_sample.reshape(bs * ls, d))
    c_rows = jnp.concatenate([c_ctx[None], c, jnp.zeros((MOD_ROWS - 1 - bs, d), F32)], axis=0)
    mod = _mod_table(c_rows, w_mod, b_mod).reshape(depth * MOD_ROWS, MOD_CHUNKS, d)

    ffn_w_down_bf = ffn_w_down.astype(BF16)
    ssd_w_out_bf = ssd_w_out.astype(BF16)
    gla_states, ssd_new, lru_new = None, [], []
    for l in range(depth):
        kind, j = l % 3, l // 3
        h = _norm_mod(x, norm_mix_g[l], mod, l, tok, shift_idx=0, scale_idx=1)
        if kind == 0:
            w = _prep_gla(gla_w_g1[j], gla_w_g2[j], gla_b_g[j], gla_norm_g[j])
            y, gla_states = _gla_mixer(h, state_gla[:, j], gla_w_in, j, w, tok, gla_states)
            w_out = gla_w_out
        elif kind == 1:
            n_main = ssd_conv_w.shape[2] + ssd_d.shape[1] * SSD_HEADDIM
            w = _prep_ssd(ssd_w_in[j, :, n_main:], ssd_conv_w[j], ssd_conv_b[j], ssd_a_log[j],
                          ssd_dt_bias[j], ssd_d[j], ssd_norm_g[j])
            proj = _matmul(h, ssd_w_in, j, col_start=0, n_cols=n_main, bm=1024, bn=1024,
                           name="ssd_in_main")
            dtp = _matmul(h, w["w_dt"][None], 0, bm=1024, bn=SSD_GROUPS * _LANES, name="ssd_in_dt")
            gw = w["di"] // SSD_GROUPS
            s_lat0 = state_ssd[:, j].reshape(bs, 2, SSD_GROUPS, gw, SSD_STATE)
            y, s_new = _ssd_scan(proj, dtp, w, None, tok, latent=False)
            (y,) = _ssd_scan(proj, dtp, w, s_lat0, tok, latent=True, y_buf=y)
            ssd_new.append(s_new.reshape(bp, 2, -1, SSD_HEADDIM, SSD_STATE))
            w_out = ssd_w_out_bf
        else:
            w = _prep_lru(lru_conv_w[j], lru_conv_b[j], lru_w_a[j], lru_b_a[j],
                          lru_w_i[j], lru_b_i[j], lru_lambda[j])
            proj = _matmul(h, lru_w_in, j, bm=1024, bn=1024, name="lru_in")
            s_lat0 = state_lru[:, j].reshape(bs, 2, 1, -1)
            y, s_new = _lru_scan(proj, w, None, tok, latent=False)
            (y,) = _lru_scan(proj, w, s_lat0, tok, latent=True, y_buf=y)
            lru_new.append(s_new.reshape(bp, 2, -1))
            w_out = lru_w_out
        x = _matmul(y, w_out, j, bm=512 if isinstance(x, tuple) else 1024, bn=1024,
                    bk=min(w_out.shape[1], 2048), residual=(x, mod, l, tok, 2), name="mixer_out")
        h = _norm_mod(x, norm_ffn_g[l], mod, l, tok, shift_idx=3, scale_idx=4)
        g = _ffn_up(h, ffn_w_up, ffn_conv_w, ffn_conv_b, l, tok)
        x = _matmul(g, ffn_w_down_bf, l, bm=1024, bn=1024, bk=ffn_w_down.shape[1] // 2,
                    residual=(x, mod, l, tok, 5), name="ffn_down")
    y_prompt, y_sample = _final_norm(x, final_norm_g, tok)
    return (y_prompt.reshape(bp, lp, d), y_sample.reshape(bs, ls, d), gla_states,
            jnp.stack(ssd_new, axis=1), jnp.stack(lru_new, axis=1))
```

```python
import functools

import jax
import jax.numpy as jnp
from jax import lax
from jax.experimental import pallas as pl
from jax.experimental.pallas import tpu as pltpu

F32 = jnp.float32
BF16 = jnp.bfloat16

_V7X_VMEM_BYTES = 64 * 1024 * 1024
_VMEM_LIMIT = _V7X_VMEM_BYTES - 8 * 1024 * 1024
_SUBLANES = 8
_LANES = 128

NORM_EPS = 1e-6
MOD_CHUNKS = 6
MOD_ROWS = 8
GRID_W = 64

GLA_HEADS = 4
GLA_RANK = 16
GLA_GATE_NORM = 16.0
SSD_HEADDIM = 64
SSD_GROUPS = 8
SSD_STATE = 128
LRU_BLOCKS = 8
LRU_C = 8.0
CHUNK = 64


def _cparams(*sem):
    return pltpu.CompilerParams(dimension_semantics=sem, vmem_limit_bytes=_VMEM_LIMIT)


def _sigmoid(x):
    return 0.5 + 0.5 * jnp.tanh(0.5 * x)


def _silu_of_half(h):
    return h + h * jnp.tanh(h)


def _silu(x):
    return _silu_of_half(0.5 * x)


def _softplus(x):
    return jnp.maximum(x, 0.0) + jnp.log1p(jnp.exp(-jnp.abs(x)))


def _log_sigmoid(x):
    return jnp.minimum(x, 0.0) - jnp.log1p(jnp.exp(-jnp.abs(x)))


def _gelu_tanh(x):
    return 0.5 * x * (1.0 + jnp.tanh(0.7978845608028654 * (x + 0.044715 * (x * x * x))))


def _neg_expm1_2x(x):
    t = jnp.tanh(x)
    return -2.0 * t / (1.0 - t)


def _split3(x):
    x1 = x.astype(BF16)
    r1 = x - x1.astype(F32)
    x2 = r1.astype(BF16)
    x3 = (r1 - x2.astype(F32)).astype(BF16)
    return x1, x2, x3


def _dot(a, b):
    return jnp.dot(a, b, preferred_element_type=F32)


def _dot_nt(a, b):
    return lax.dot_general(a, b, (((1,), (1,)), ((), ())), preferred_element_type=F32)


def _dot_tn(a, b):
    return lax.dot_general(a, b, (((0,), (0,)), ((), ())), preferred_element_type=F32)


def _exact_dot_left01(m01, x):
    x1, x2, x3 = _split3(x)
    return _dot(m01, x1) + _dot(m01, x2) + _dot(m01, x3)


def _exact_select_copies(x, sel01, ncopy_lanes):
    t1 = x.astype(BF16).astype(F32)
    r1 = x - t1
    t2 = r1.astype(BF16).astype(F32)
    lane = lax.broadcasted_iota(jnp.int32, x.shape, 1)
    terms = jnp.where(lane < ncopy_lanes, t1, jnp.where(lane < 2 * ncopy_lanes, t2, r1 - t2))
    return _dot(terms.astype(BF16), sel01)


def _chunk_masks():
    ii = lax.broadcasted_iota(jnp.int32, (CHUNK, CHUNK), 0)
    jj = lax.broadcasted_iota(jnp.int32, (CHUNK, CHUNK), 1)
    masks = (jj <= ii, jj >= ii)
    tris = tuple(jnp.where(m, 1.0, 0.0).astype(BF16) for m in masks)
    return masks, tris


def _block_chunk_masks(rows):
    ii = lax.broadcasted_iota(jnp.int32, (rows, rows), 0)
    jj = lax.broadcasted_iota(jnp.int32, (rows, rows), 1)
    shift = CHUNK.bit_length() - 1
    same = jnp.where(jnp.right_shift(ii, shift) == jnp.right_shift(jj, shift), 1, 0)
    masks = (same * jnp.where(jj <= ii, 1, 0) > 0, same * jnp.where(jj >= ii, 1, 0) > 0)
    tris = tuple(jnp.where(m, 1.0, 0.0).astype(BF16) for m in masks)
    return masks, tris


def _shifted_rows(ref, r0, nrows, off, total):
    lo, hi = r0 + off, r0 + off + nrows
    clo, chi = max(lo, 0), min(hi, total)
    x = ref[clo:chi, :]
    parts = []
    if clo > lo:
        parts.append(jnp.zeros((clo - lo, x.shape[1]), x.dtype))
    parts.append(x)
    if hi > chi:
        parts.append(jnp.zeros((hi - chi, x.shape[1]), x.dtype))
    return x if len(parts) == 1 else jnp.concatenate(parts, axis=0)


def _conv4_block(ref, w_ref, b_ref, r0, nrows, total, scale=1.0):
    w, b = w_ref[...], b_ref[...]
    if scale != 1.0:
        w, b = scale * w, scale * b
    ext = _shifted_rows(ref, r0 - _SUBLANES, nrows + 2 * _SUBLANES, 0, total)
    n = nrows + 2 * _SUBLANES

    def tap(off):
        return pltpu.roll(ext, (-off) % n, 0)[_SUBLANES:_SUBLANES + nrows, :]

    acc = b + w[0:1, :] * tap(-1)
    acc = acc + w[1:2, :] * ext[_SUBLANES:_SUBLANES + nrows, :]
    acc = acc + w[2:3, :] * tap(1)
    acc = acc + w[3:4, :] * tap(2)
    return acc


def _mod_kernel(c_ref, w_ref, b_ref, o_ref):
    c = c_ref[...]
    s = _silu(c).astype(BF16)
    o_ref[0] = _dot(s, w_ref[0].astype(BF16)) + b_ref[0]


def _mod_table(c_rows, w_mod, b_mod):
    depth, d, n = w_mod.shape
    bn = 1024
    return pl.pallas_call(
        _mod_kernel,
        grid=(depth, n // bn),
        in_specs=[
            pl.BlockSpec((MOD_ROWS, d), lambda l, j: (0, 0)),
            pl.BlockSpec((1, d, bn), lambda l, j: (l, 0, j)),
            pl.BlockSpec((1, 1, bn), lambda l, j: (l, 0, j)),
        ],
        out_specs=pl.BlockSpec((1, MOD_ROWS, bn), lambda l, j: (l, 0, j)),
        out_shape=jax.ShapeDtypeStruct((depth, MOD_ROWS, n), F32),
        compiler_params=_cparams("arbitrary", "arbitrary"),
        name="adaln_table",
    )(c_rows, w_mod, b_mod.reshape(depth, 1, n))


class _Tokens:
    def __init__(self, n_prompt, l_prompt, n_latent, l_latent):
        self.bp, self.lp, self.bs, self.ls = n_prompt, l_prompt, n_latent, l_latent
        self.np_rows = n_prompt * l_prompt
        self.ns_rows = n_latent * l_latent
        self.rows = self.np_rows + self.ns_rows

    def mod_row(self, i, bm):
        r0 = i * bm
        return jnp.where(r0 < self.np_rows, 0, 1 + (r0 - self.np_rows) // self.ls)


_NORM_ROWS = 2 * _SUBLANES


def _rmsnorm_rows(x_ref, g_ref, r0):
    x = x_ref[pl.ds(r0, _NORM_ROWS), :]
    ms = jnp.mean(x * x, axis=-1, keepdims=True)
    return x * lax.rsqrt(ms + NORM_EPS) * g_ref[...]


def _row_specs(x, bm, bn, tok, row_of, col_of):
    if not isinstance(x, tuple):
        return [pl.BlockSpec((bm, bn), lambda *g: (row_of(*g), col_of(*g)))], [x]
    npt = tok.np_rows // bm
    return ([pl.BlockSpec((bm, bn), lambda *g: (jnp.minimum(row_of(*g), npt - 1), col_of(*g))),
             pl.BlockSpec((bm, bn), lambda *g: (jnp.maximum(row_of(*g) - npt, 0), col_of(*g)))],
            list(x))


def _on_row_source(x_refs, row_tile, n_prompt_tiles, fn):
    if len(x_refs) == 1:
        fn(x_refs[0])
        return

    @pl.when(row_tile < n_prompt_tiles)
    def _():
        fn(x_refs[0])

    @pl.when(row_tile >= n_prompt_tiles)
    def _():
        fn(x_refs[1])


def _norm_mod_kernel(*refs, shift_idx, scale_idx, n_prompt_tiles):
    *x_refs, g_ref, mod_ref, h_ref = refs

    def normalise(x_ref):
        def body(c, carry):
            r0 = pl.multiple_of(c * _NORM_ROWS, _NORM_ROWS)
            y = _rmsnorm_rows(x_ref, g_ref, r0)
            h = y * (1.0 + mod_ref[0, scale_idx:scale_idx + 1, :]) + mod_ref[0, shift_idx:shift_idx + 1, :]
            h_ref[pl.ds(r0, _NORM_ROWS), :] = h.astype(h_ref.dtype)
            return carry

        lax.fori_loop(0, x_ref.shape[0] // _NORM_ROWS, body, 0, unroll=8)

    _on_row_source(x_refs, pl.program_id(0), n_prompt_tiles, normalise)


def _norm_mod(x, g, mod, layer, tok, *, shift_idx, scale_idx):
    d = g.shape[0]
    bm = 1024
    x_specs, x_args = _row_specs(x, bm, d, tok, lambda i: i, lambda i: 0)
    return pl.pallas_call(
        functools.partial(_norm_mod_kernel, shift_idx=shift_idx, scale_idx=scale_idx,
                          n_prompt_tiles=tok.np_rows // bm),
        grid=(tok.rows // bm,),
        in_specs=x_specs + [
            pl.BlockSpec((1, d), lambda i: (0, 0)),
            pl.BlockSpec((1, MOD_CHUNKS, d), lambda i: (layer * MOD_ROWS + tok.mod_row(i, bm), 0, 0)),
        ],
        out_specs=pl.BlockSpec((bm, d), lambda i: (i, 0)),
        out_shape=jax.ShapeDtypeStruct((tok.rows, d), BF16),
        compiler_params=_cparams("arbitrary"),
        name="norm_adaln",
    )(*x_args, g.reshape(1, d), mod)


def _final_norm_kernel(x_ref, g_ref, op_ref, os_ref, *, n_prompt_tiles):
    def norm_into(o_ref):
        def body(c, carry):
            r0 = pl.multiple_of(c * _NORM_ROWS, _NORM_ROWS)
            o_ref[pl.ds(r0, _NORM_ROWS), :] = _rmsnorm_rows(x_ref, g_ref, r0)
            return carry

        lax.fori_loop(0, x_ref.shape[0] // _NORM_ROWS, body, 0, unroll=8)

    @pl.when(pl.program_id(0) < n_prompt_tiles)
    def _():
        norm_into(op_ref)

    @pl.when(pl.program_id(0) >= n_prompt_tiles)
    def _():
        norm_into(os_ref)


def _final_norm(x, g, tok):
    rows, d = x.shape
    bm = 1024
    npt = tok.np_rows // bm
    return pl.pallas_call(
        functools.partial(_final_norm_kernel, n_prompt_tiles=npt),
        grid=(rows // bm,),
        in_specs=[pl.BlockSpec((bm, d), lambda i: (i, 0)), pl.BlockSpec((1, d), lambda i: (0, 0))],
        out_specs=[pl.BlockSpec((bm, d), lambda i: (jnp.minimum(i, npt - 1), 0)),
                   pl.BlockSpec((bm, d), lambda i: (jnp.maximum(i - npt, 0), 0))],
        out_shape=[jax.ShapeDtypeStruct((tok.np_rows, d), F32),
                   jax.ShapeDtypeStruct((tok.ns_rows, d), F32)],
        compiler_params=_cparams("arbitrary"),
        name="final_norm",
    )(x, g.reshape(1, d))


def _mm_kernel(*refs, nk, gate_idx, n_prompt_tiles):
    if gate_idx is None:
        x_ref, w_ref, o_ref, acc_ref, wb_ref = refs
        res_refs, mod_ref = [], None
    else:
        x_ref, w_ref, *res_refs, mod_ref, o_ref, acc_ref, wb_ref = refs
    i, k = pl.program_id(1), pl.program_id(2)
    if w_ref.dtype == BF16:
        w = w_ref[0]
    elif nk == 1:
        @pl.when(i == 0)
        def _():
            wb_ref[...] = w_ref[0].astype(BF16)

        w = wb_ref[...]
    else:
        w = w_ref[0].astype(BF16)
    part = _dot(x_ref[...], w)

    def finish(acc):
        if gate_idx is None:
            o_ref[...] = acc.astype(o_ref.dtype)
            return

        def gated_residual(res_ref):
            o_ref[...] = res_ref[...] + mod_ref[0, gate_idx:gate_idx + 1, :] * acc

        _on_row_source(res_refs, i, n_prompt_tiles, gated_residual)

    if nk == 1:
        finish(part)
        return

    @pl.when(k == 0)
    def _():
        acc_ref[...] = part

    @pl.when(jnp.logical_and(k > 0, k < nk - 1))
    def _():
        acc_ref[...] += part

    @pl.when(k == nk - 1)
    def _():
        finish(acc_ref[...] + part)


def _matmul(x, w, layer, *, col_start=0, n_cols=None, bm, bn, bk=None, out_dtype=F32, residual=None,
            name="matmul"):
    m, kdim = x.shape
    n_cols = w.shape[2] - col_start if n_cols is None else n_cols
    bk = kdim if bk is None else bk
    nk = kdim // bk
    off = col_start // bn
    in_specs = [
        pl.BlockSpec((bm, bk), lambda j, i, k: (i, k)),
        pl.BlockSpec((1, bk, bn), lambda j, i, k: (layer, k, j + off)),
    ]
    args = [x, w]
    gate_idx, n_prompt_tiles = None, 0
    if residual is not None:
        res, mod, mod_layer, tok, gate_idx = residual
        n_prompt_tiles = tok.np_rows // bm
        res_specs, res_args = _row_specs(res, bm, bn, tok, lambda j, i, k: i, lambda j, i, k: j)
        in_specs += res_specs + [
            pl.BlockSpec((1, MOD_CHUNKS, bn),
                         lambda j, i, k: (mod_layer * MOD_ROWS + tok.mod_row(i, bm), 0, j)),
        ]
        args += res_args + [mod]
    cast_once = w.dtype != BF16 and nk == 1
    tiny = (2 * _SUBLANES, _LANES)
    return pl.pallas_call(
        functools.partial(_mm_kernel, nk=nk, gate_idx=gate_idx, n_prompt_tiles=n_prompt_tiles),
        grid=(n_cols // bn, m // bm, nk),
        in_specs=in_specs,
        out_specs=pl.BlockSpec((bm, bn), lambda j, i, k: (i, j)),
        out_shape=jax.ShapeDtypeStruct((m, n_cols), out_dtype),
        scratch_shapes=[pltpu.VMEM((bm, bn) if nk > 1 else tiny, F32),
                        pltpu.VMEM((bk, bn) if cast_once else tiny, BF16)],
        compiler_params=_cparams("arbitrary", "arbitrary", "arbitrary"),
        name=name,
    )(*args)


def _ffn_up_kernel(h_ref, wa_ref, wv_ref, cw_ref, cb_ref, g_ref, *, n_prompt_tiles, w_prompt, w_latent):
    i = pl.program_id(0)
    x = h_ref[...]
    act = _dot(x, wa_ref[0].astype(BF16))
    val = _dot(x, wv_ref[0].astype(BF16))
    t, tn = act.shape
    row = lax.broadcasted_iota(jnp.int32, (t, tn), 0)
    prev = pltpu.roll(act, 1, 0)
    nxt = pltpu.roll(act, t - 1, 0)

    def horizontal(width):
        col = row & (width - 1)
        return jnp.where(col > 0, prev, 0.0), jnp.where(col < width - 1, nxt, 0.0)

    half_w = 0.5 * cw_ref[...]
    half_b = 0.5 * cb_ref[...]

    def tap_row(kh, left, right):
        return (half_w[3 * kh:3 * kh + 1, :] * left + half_w[3 * kh + 1:3 * kh + 2, :] * act
                + half_w[3 * kh + 2:3 * kh + 3, :] * right)

    def gate(half_conv):
        g_ref[...] = (_silu_of_half(half_conv) * val).astype(g_ref.dtype)

    @pl.when(i < n_prompt_tiles)
    def _():
        left, right = horizontal(w_prompt)
        gate(tap_row(1, left, right) + half_b)

    @pl.when(i >= n_prompt_tiles)
    def _():
        left, right = horizontal(w_latent)
        zeros = jnp.zeros((w_latent, tn), F32)
        above = jnp.concatenate([zeros, tap_row(0, left, right)[:t - w_latent, :]], axis=0)
        below = jnp.concatenate([tap_row(2, left, right)[w_latent:, :], zeros], axis=0)
        gate(above + tap_row(1, left, right) + below + half_b)


def _ffn_up(h, w_up, conv_w, conv_b, layer, tok):
    rows, d = h.shape
    f = w_up.shape[2] // 2
    tm = tok.ls
    tn = 256
    assert tok.np_rows % tm == 0 and tm % tok.lp == 0
    nj = f // tn
    return pl.pallas_call(
        functools.partial(_ffn_up_kernel, n_prompt_tiles=tok.np_rows // tm,
                          w_prompt=tok.lp, w_latent=GRID_W),
        grid=(rows // tm, nj),
        in_specs=[
            pl.BlockSpec((tm, d), lambda i, j: (i, 0)),
            pl.BlockSpec((1, d, tn), lambda i, j: (layer, 0, j)),
            pl.BlockSpec((1, d, tn), lambda i, j: (layer, 0, j + nj)),
            pl.BlockSpec((9, tn), lambda i, j: (0, j)),
            pl.BlockSpec((1, tn), lambda i, j: (0, j)),
        ],
        out_specs=pl.BlockSpec((tm, tn), lambda i, j: (i, j)),
        out_shape=jax.ShapeDtypeStruct((rows, f), BF16),
        compiler_params=_cparams("arbitrary", "arbitrary"),
        name="ffn_up_conv",
    )(h, w_up, w_up, conv_w[layer].reshape(9, f), conv_b[layer].reshape(1, f))


def _gla_kernel(*refs, seq_len, zero_init, emit_state, n_alias, dk, dv):
    it = iter(refs)
    q_ref, k_ref, v_ref, r_ref, low_ref, w2_ref, bg_ref, ng_ref = (next(it) for _ in range(8))
    s0_ref = None if zero_init else next(it)
    for _ in range(n_alias):
        next(it)
    y_ref = next(it)
    sout_ref = next(it).at[0, 0] if emit_state else None
    g_scr, o_scr, s_scr = next(it), next(it), next(it)

    nc = seq_len // CHUNK
    blk = 256
    cpb = blk // CHUNK
    nb = seq_len // blk
    masks, tris = _block_chunk_masks(blk)
    mid_row = (CHUNK // 2, CHUNK // 2 - 1)
    end_row = (CHUNK - 1, 0)
    q_scale = dk ** -0.5

    low = low_ref[...].astype(BF16)
    for e in range(2):
        z = _dot(low, w2_ref[e]) + bg_ref[e]
        g_scr[e] = _log_sigmoid(z) * (1.0 / GLA_GATE_NORM)
        if zero_init:
            s_scr[e] = jnp.zeros((dv, dk), F32)
        else:
            s_scr[e] = s0_ref[0, e, 0].T

    def block_step(e, b):
        r0 = pl.multiple_of(b * blk, blk)
        gc = _exact_dot_left01(tris[e], g_scr[e, pl.ds(r0, blk), :])
        g_scr[e, pl.ds(r0, blk), :] = gc
        g_mid = jnp.concatenate(
            [jnp.broadcast_to(gc[c * CHUNK + mid_row[e]:c * CHUNK + mid_row[e] + 1, :], (CHUNK, dk))
             for c in range(cpb)], axis=0)
        q = q_ref[pl.ds(r0, blk), :] * q_scale
        k = k_ref[pl.ds(r0, blk), :]
        scores = _dot_nt((q * jnp.exp(gc - g_mid)).astype(BF16), (k * jnp.exp(g_mid - gc)).astype(BF16))
        o_scr[e, pl.ds(r0, blk), :] = _dot(jnp.where(masks[e], scores, 0.0).astype(BF16),
                                            v_ref[pl.ds(r0, blk), :])

    def block_body(b, carry):
        block_step(0, b)
        block_step(1, b)
        return carry

    lax.fori_loop(0, nb, block_body, 0, unroll=min(4, nb))

    def chunk_step(e, c):
        r0 = pl.multiple_of(c * CHUNK, CHUNK)
        gc = g_scr[e, pl.ds(r0, CHUNK), :]
        g_end = gc[end_row[e]:end_row[e] + 1, :]
        q = q_ref[pl.ds(r0, CHUNK), :] * q_scale
        k = k_ref[pl.ds(r0, CHUNK), :]
        v = v_ref[pl.ds(r0, CHUNK), :]
        s_t = s_scr[e]
        o_scr[e, pl.ds(r0, CHUNK), :] += _dot_nt((q * jnp.exp(gc)).astype(BF16), s_t.astype(BF16))
        u_t = _dot_tn(v, (k * jnp.exp(g_end - gc)).astype(BF16))
        s_scr[e] = s_t * jnp.exp(g_end) + u_t

    def body(i, carry):
        chunk_step(0, i)
        chunk_step(1, nc - 1 - i)
        return carry

    lax.fori_loop(0, nc, body, 0, unroll=min(8, nc))

    for r0 in range(0, seq_len, blk):
        o = o_scr[0, r0:r0 + blk, :] + o_scr[1, r0:r0 + blk, :]
        ms = jnp.mean(o * o, axis=-1, keepdims=True)
        y = o * lax.rsqrt(ms + NORM_EPS) * ng_ref[...]
        y_ref[r0:r0 + blk, :] = (y * _silu(r_ref[r0:r0 + blk, :])).astype(y_ref.dtype)

    if emit_state:
        for e in range(2):
            sout_ref[e, 0] = s_scr[e].T


def _in_place(buffers):
    return [pl.BlockSpec(memory_space=pl.ANY) for _ in buffers], list(buffers)


def _gla_scan(qk, v, r, low, w2, bg, ng, s0, tok, *, latent, y_buf=None, state_buf=None, layer=0,
              n_layers=1):
    dk = qk.shape[1] // (2 * GLA_HEADS)
    dv = v.shape[1] // GLA_HEADS
    if latent:
        nseq, seq_len, rb0 = tok.bs, tok.ls, tok.np_rows // tok.ls
    else:
        nseq, seq_len, rb0 = tok.bp, tok.lp, 0
    in_specs = [
        pl.BlockSpec((seq_len, dk), lambda b, h: (rb0 + b, h)),
        pl.BlockSpec((seq_len, dk), lambda b, h: (rb0 + b, GLA_HEADS + h)),
        pl.BlockSpec((seq_len, dv), lambda b, h: (rb0 + b, h)),
        pl.BlockSpec((seq_len, dv), lambda b, h: (rb0 + b, h)),
        pl.BlockSpec((seq_len, _LANES), lambda b, h: (rb0 + b, 0)),
        pl.BlockSpec((2, _LANES, dk), lambda b, h: (0, 0, h)),
        pl.BlockSpec((2, 1, dk), lambda b, h: (0, 0, h)),
        pl.BlockSpec((1, dv), lambda b, h: (0, 0)),
    ]
    args = [qk, qk, v, r, low, w2, bg, ng]
    if latent:
        in_specs.append(pl.BlockSpec((1, 2, 1, dk, dv), lambda b, h: (b, 0, h, 0, 0)))
        args.append(s0)
    y_spec = pl.BlockSpec((seq_len, dv), lambda b, h: (rb0 + b, h))
    y_shape = jax.ShapeDtypeStruct((tok.rows, v.shape[1]), BF16)
    if latent:
        out_specs, out_shape, bufs = [y_spec], [y_shape], [y_buf]
    else:
        out_specs = [y_spec, pl.BlockSpec((1, 1, 2, 1, dk, dv), lambda b, h: (b, layer, 0, h, 0, 0))]
        out_shape = [y_shape, jax.ShapeDtypeStruct((nseq, n_layers, 2, GLA_HEADS, dk, dv), F32)]
        bufs = [y_buf, state_buf]
    aliases = {}
    for o, buf in enumerate(bufs):
        if buf is not None:
            aliases[len(args)] = o
            spec, arg = _in_place([buf])
            in_specs += spec
            args += arg
    return pl.pallas_call(
        functools.partial(_gla_kernel, seq_len=seq_len, zero_init=not latent, emit_state=not latent,
                          n_alias=len(aliases), dk=dk, dv=dv),
        grid=(nseq, GLA_HEADS),
        in_specs=in_specs,
        out_specs=out_specs,
        out_shape=out_shape,
        input_output_aliases=aliases,
        scratch_shapes=[
            pltpu.VMEM((2, seq_len, dk), F32),
            pltpu.VMEM((2, seq_len, dv), F32),
            pltpu.VMEM((2, dv, dk), F32),
        ],
        compiler_params=_cparams("arbitrary", "arbitrary"),
        name="gla_scan_latent" if latent else "gla_scan_prompt",
    )(*args)


def _gla_mixer(h, s_lat0, w_in, layer, w, tok, state_buf):
    w_g1cat, w2, bg, ng = w["w_g1cat"], w["w2"], w["bg"], w["ng"]
    qkw = 2 * w2.shape[2]
    vw = (w_in.shape[2] - qkw) // 2
    qk = _matmul(h, w_in, layer, col_start=0, n_cols=qkw, bm=1024, bn=1024, name="gla_in_qk")
    v = _matmul(h, w_in, layer, col_start=qkw, n_cols=vw, bm=1024, bn=1024, out_dtype=BF16,
                name="gla_in_v")
    r = _matmul(h, w_in, layer, col_start=qkw + vw, n_cols=vw, bm=1024, bn=1024, name="gla_in_r")
    low = _matmul(h, w_g1cat[None], 0, bm=1024, bn=_LANES, name="gla_in_gate")
    y, states = _gla_scan(qk, v, r, low, w2, bg, ng, None, tok, latent=False, state_buf=state_buf,
                          layer=layer, n_layers=w_in.shape[0])
    (y,) = _gla_scan(qk, v, r, low, w2, bg, ng, s_lat0, tok, latent=True, y_buf=y)
    return y, states


def _ssd_kernel(*refs, seq_len, zero_init, emit_state, n_alias):
    it = iter(refs)
    (z_ref, x_ref, b_ref, c_ref, dt_ref, cwx_ref, cbx_ref, cwb_ref, cbb_ref, cwc_ref, cbc_ref,
     dtb_ref, alog_ref, dskip_ref, ng_ref, e_ref) = (next(it) for _ in range(16))
    s0_ref = None if zero_init else next(it)
    for _ in range(n_alias):
        next(it)
    y_ref = next(it)
    sout_ref = next(it) if emit_state else None
    xd_scr, cumx_scr, cumt_scr, bm_scr, cm_scr, y_scr, s_scr = (next(it) for _ in range(7))

    hpg = y_scr.shape[1] // SSD_HEADDIM
    nc = seq_len // CHUNK
    blk = 256
    cpb = blk // CHUNK
    masks, _ = _chunk_masks()
    _, blk_tris = _block_chunk_masks(blk)
    end_row = (CHUNK - 1, 0)
    a_row = -jnp.exp(alog_ref[0])

    for r0 in range(0, seq_len, blk):
        def conv_silu(ref, w_ref, bias_ref):
            return _silu_of_half(_conv4_block(ref, w_ref, bias_ref, r0, blk, seq_len, scale=0.5))

        xs = conv_silu(x_ref, cwx_ref, cbx_ref)
        y_scr[r0:r0 + blk, :] = dskip_ref[...] * xs
        bm_scr[r0:r0 + blk, :] = conv_silu(b_ref, cwb_ref, cbb_ref).astype(BF16)
        cm_scr[r0:r0 + blk, :] = conv_silu(c_ref, cwc_ref, cbc_ref).astype(BF16)
        dt = _softplus(dt_ref[r0:r0 + blk, :] + dtb_ref[0])
        for e in range(2):
            expand = e_ref[e]
            cum = _exact_dot_left01(blk_tris[e], dt * a_row)
            cum_t = cum.T
            for cc in range(cpb):
                cumt_scr[e, r0 // CHUNK + cc] = cum_t[0:2 * hpg, cc * CHUNK:(cc + 1) * CHUNK]
            cumx_scr[e, r0:r0 + blk, :] = _exact_select_copies(cum, expand, 2 * hpg)
            xd_scr[e, r0:r0 + blk, :] = xs * _exact_select_copies(dt, expand, 2 * hpg)

    for e in range(2):
        if zero_init:
            s_scr[e] = jnp.zeros(s_scr.shape[1:], F32)
        else:
            s_scr[e] = s0_ref[0, e, 0].T

    def chunk_step(e, c):
        r0 = pl.multiple_of(c * CHUNK, CHUNK)
        cum_x = cumx_scr[e, pl.ds(r0, CHUNK), :]
        cum_t = cumt_scr[e, c]
        xd = xd_scr[e, pl.ds(r0, CHUNK), :]
        xd_bf = xd.astype(BF16)
        bm = bm_scr[pl.ds(r0, CHUNK), :]
        cm = cm_scr[pl.ds(r0, CHUNK), :]
        cb = _dot_nt(cm, bm)
        parts = []
        for hh in range(hpg):
            lane = slice(hh * SSD_HEADDIM, (hh + 1) * SSD_HEADDIM)
            seg = cum_x[:, lane] - cum_t[e * hpg + hh:e * hpg + hh + 1, :]
            decay = jnp.where(masks[e], jnp.exp(jnp.where(masks[e], seg, 0.0)), 0.0)
            parts.append(_dot((cb * decay).astype(BF16), xd_bf[:, lane]))
        y = jnp.concatenate(parts, axis=1)
        s_t = s_scr[e]
        y = y + _dot(cm, s_t.astype(BF16)) * jnp.exp(cum_x)
        cum_end = cum_x[end_row[e]:end_row[e] + 1, :]
        st_t = _dot_tn(bm, (xd * jnp.exp(cum_end - cum_x)).astype(BF16))
        s_scr[e] = s_t * jnp.exp(cum_end) + st_t
        y_scr[pl.ds(r0, CHUNK), :] += y

    def body(i, carry):
        chunk_step(0, i)
        chunk_step(1, nc - 1 - i)
        return carry

    lax.fori_loop(0, nc, body, 0, unroll=min(8, nc))

    for r0 in range(0, seq_len, blk):
        y = y_scr[r0:r0 + blk, :] * _silu(z_ref[r0:r0 + blk, :])
        ms = jnp.mean(y * y, axis=-1, keepdims=True)
        y_ref[r0:r0 + blk, :] = (y * lax.rsqrt(ms + NORM_EPS) * ng_ref[...]).astype(y_ref.dtype)

    if emit_state:
        for e in range(2):
            sout_ref[0, e, 0] = s_scr[e].T


def _ssd_scan(proj, dtp, w, s0, tok, *, latent, y_buf=None):
    di = w["di"]
    gw = di // SSD_GROUPS
    if latent:
        nseq, seq_len, rb0 = tok.bs, tok.ls, tok.np_rows // tok.ls
    else:
        nseq, seq_len, rb0 = tok.bp, tok.lp, 0
    nx = di // gw
    nb = (2 * di) // SSD_STATE
    ncb = nb + SSD_GROUPS

    def rows(width, col):
        return pl.BlockSpec((seq_len, width), lambda b, g: (rb0 + b, col(g)))

    def vec(width, col, nrows=1):
        return pl.BlockSpec((nrows, width), lambda b, g: (0, col(g)))

    in_specs = [
        rows(gw, lambda g: g),
        rows(gw, lambda g: nx + g),
        rows(SSD_STATE, lambda g: nb + g),
        rows(SSD_STATE, lambda g: ncb + g),
        pl.BlockSpec((seq_len, _LANES), lambda b, g: (rb0 + b, g)),
        vec(gw, lambda g: g, 4), vec(gw, lambda g: g),
        vec(SSD_STATE, lambda g: di // SSD_STATE + g, 4), vec(SSD_STATE, lambda g: di // SSD_STATE + g),
        vec(SSD_STATE, lambda g: di // SSD_STATE + SSD_GROUPS + g, 4),
        vec(SSD_STATE, lambda g: di // SSD_STATE + SSD_GROUPS + g),
        pl.BlockSpec((1, 1, _LANES), lambda b, g: (g, 0, 0)),
        pl.BlockSpec((1, 1, _LANES), lambda b, g: (g, 0, 0)),
        vec(gw, lambda g: g),
        vec(gw, lambda g: g),
        pl.BlockSpec((2, _LANES, gw), lambda b, g: (0, 0, 0)),
    ]
    args = [proj, proj, proj, proj, dtp, w["conv_w"], w["conv_b"], w["conv_w"], w["conv_b"],
            w["conv_w"], w["conv_b"], w["dt_bias"], w["a_log"], w["d_skip"], w["ng"], w["expand"]]
    st_spec = pl.BlockSpec((1, 2, 1, gw, SSD_STATE), lambda b, g: (b, 0, g, 0, 0))
    if latent:
        in_specs.append(st_spec)
        args.append(s0)
    y_spec = pl.BlockSpec((seq_len, gw), lambda b, g: (rb0 + b, g))
    y_shape = jax.ShapeDtypeStruct((tok.rows, di), BF16)
    aliases = {}
    if latent:
        out_specs, out_shape = [y_spec], [y_shape]
        aliases[len(args)] = 0
        spec, arg = _in_place([y_buf])
        in_specs += spec
        args += arg
    else:
        out_specs = [y_spec, st_spec]
        out_shape = [y_shape, jax.ShapeDtypeStruct((nseq, 2, SSD_GROUPS, gw, SSD_STATE), F32)]

    return pl.pallas_call(
        functools.partial(_ssd_kernel, seq_len=seq_len, zero_init=not latent, emit_state=not latent,
                          n_alias=len(aliases)),
        grid=(nseq, SSD_GROUPS),
        in_specs=in_specs,
        out_specs=out_specs,
        out_shape=out_shape,
        input_output_aliases=aliases,
        scratch_shapes=[
            pltpu.VMEM((2, seq_len, gw), F32),
            pltpu.VMEM((2, seq_len, gw), F32),
            pltpu.VMEM((2, seq_len // CHUNK, 2 * gw // SSD_HEADDIM, CHUNK), F32),
            pltpu.VMEM((seq_len, SSD_STATE), BF16),
            pltpu.VMEM((seq_len, SSD_STATE), BF16),
            pltpu.VMEM((seq_len, gw), F32),
            pltpu.VMEM((2, SSD_STATE, gw), F32),
        ],
        compiler_params=_cparams("arbitrary", "arbitrary"),
        name="ssd_scan_latent" if latent else "ssd_scan_prompt",
    )(*args)


def _lru_kernel(*refs, seq_len, zero_init, emit_state, n_alias):
    it = iter(refs)
    (x_ref, gate_ref, cw_ref, cb_ref, wa_ref, ba_ref, wi_ref, bi_ref, lam_ref) = (next(it) for _ in range(9))
    s0_ref = None if zero_init else next(it)
    for _ in range(n_alias):
        next(it)
    y_ref = next(it)
    sout_ref = next(it) if emit_state else None
    a_scr, u_scr, h_scr = next(it), next(it), next(it)
    width = a_scr.shape[2]

    blk = 256
    for r0 in range(0, seq_len, blk):
        xc = _conv4_block(x_ref, cw_ref, cb_ref, r0, blk, seq_len)
        xc_bf = xc.astype(BF16)
        for e in range(2):
            rg = _sigmoid(_dot(xc_bf, wa_ref[e, 0]) + ba_ref[e])
            ig = _sigmoid(_dot(xc_bf, wi_ref[e, 0]) + bi_ref[e])
            log_a = -LRU_C * rg * _softplus(-lam_ref[e])
            a_scr[e, r0:r0 + blk, :] = jnp.exp(log_a)
            u_scr[e, r0:r0 + blk, :] = jnp.sqrt(_neg_expm1_2x(log_a)) * ig * xc

    nt = seq_len // _SUBLANES
    row = lax.broadcasted_iota(jnp.int32, (_SUBLANES, width), 0)

    def tile_scan(e, t, carry):
        r0 = pl.multiple_of(t * _SUBLANES, _SUBLANES)
        a = a_scr[e, pl.ds(r0, _SUBLANES), :]
        u = u_scr[e, pl.ds(r0, _SUBLANES), :]
        for s in (1, 2, 4):
            if e == 0:
                valid = row >= s
                a_sh, u_sh = pltpu.roll(a, s, 0), pltpu.roll(u, s, 0)
            else:
                valid = row < _SUBLANES - s
                a_sh, u_sh = pltpu.roll(a, _SUBLANES - s, 0), pltpu.roll(u, _SUBLANES - s, 0)
            u = u + a * jnp.where(valid, u_sh, 0.0)
            a = a * jnp.where(valid, a_sh, 1.0)
        hcur = u + a * carry
        last = _SUBLANES - 1 if e == 0 else 0
        return hcur, hcur[last:last + 1, :]

    def body(i, carry):
        cf, cbk = carry
        hf, cf = tile_scan(0, i, cf)
        r0 = pl.multiple_of(i * _SUBLANES, _SUBLANES)
        h_scr[0, pl.ds(r0, _SUBLANES), :] = hf
        tb = nt - 1 - i
        hb, cbk = tile_scan(1, tb, cbk)
        rb = pl.multiple_of(tb * _SUBLANES, _SUBLANES)
        h_scr[1, pl.ds(rb, _SUBLANES), :] = hb
        return cf, cbk

    if zero_init:
        init = (jnp.zeros((1, width), F32), jnp.zeros((1, width), F32))
    else:
        init = (s0_ref[0, 0], s0_ref[0, 1])
    cf, cbk = lax.fori_loop(0, nt, body, init, unroll=4)

    for r0 in range(0, seq_len, blk):
        hsum = h_scr[0, r0:r0 + blk, :] + h_scr[1, r0:r0 + blk, :]
        y_ref[r0:r0 + blk, :] = (hsum * _gelu_tanh(gate_ref[r0:r0 + blk, :])).astype(y_ref.dtype)

    if emit_state:
        sout_ref[0, 0] = cf
        sout_ref[0, 1] = cbk


def _lru_scan(proj, w, s0, tok, *, latent, y_buf=None):
    width = proj.shape[1] // 2
    bw = width // LRU_BLOCKS
    if latent:
        nseq, seq_len, rb0 = tok.bs, tok.ls, tok.np_rows // tok.ls
    else:
        nseq, seq_len, rb0 = tok.bp, tok.lp, 0
    pair = pl.BlockSpec((2, 1, bw), lambda b, n: (0, 0, n))
    in_specs = [
        pl.BlockSpec((seq_len, bw), lambda b, n: (rb0 + b, n)),
        pl.BlockSpec((seq_len, bw), lambda b, n: (rb0 + b, LRU_BLOCKS + n)),
        pl.BlockSpec((4, bw), lambda b, n: (0, n)),
        pl.BlockSpec((1, bw), lambda b, n: (0, n)),
        pl.BlockSpec((2, 1, bw, bw), lambda b, n: (0, n, 0, 0)), pair,
        pl.BlockSpec((2, 1, bw, bw), lambda b, n: (0, n, 0, 0)), pair,
        pair,
    ]
    args = [proj, proj, w["conv_w"], w["conv_b"], w["w_a"], w["b_a"], w["w_i"], w["b_i"], w["lam"]]
    st_spec = pl.BlockSpec((1, 2, 1, bw), lambda b, n: (b, 0, 0, n))
    if latent:
        in_specs.append(st_spec)
        args.append(s0)
    y_spec = pl.BlockSpec((seq_len, bw), lambda b, n: (rb0 + b, n))
    y_shape = jax.ShapeDtypeStruct((tok.rows, width), BF16)
    aliases = {}
    if latent:
        out_specs, out_shape = [y_spec], [y_shape]
        aliases[len(args)] = 0
        spec, arg = _in_place([y_buf])
        in_specs += spec
        args += arg
    else:
        out_specs = [y_spec, st_spec]
        out_shape = [y_shape, jax.ShapeDtypeStruct((nseq, 2, 1, width), F32)]
    return pl.pallas_call(
        functools.partial(_lru_kernel, seq_len=seq_len, zero_init=not latent, emit_state=not latent,
                          n_alias=len(aliases)),
        grid=(nseq, LRU_BLOCKS),
        in_specs=in_specs,
        out_specs=out_specs,
        out_shape=out_shape,
        input_output_aliases=aliases,
        scratch_shapes=[
            pltpu.VMEM((2, seq_len, bw), F32),
            pltpu.VMEM((2, seq_len, bw), F32),
            pltpu.VMEM((2, seq_len, bw), F32),
        ],
        compiler_params=_cparams("arbitrary", "arbitrary"),
        name="lru_scan_latent" if latent else "lru_scan_prompt",
    )(*args)


def _prep_gla(w_g1, w_g2, b_g, norm_g):
    qkw = w_g2.shape[2]
    g1 = jnp.concatenate([w_g1[0], w_g1[1]], axis=1)
    g1 = jnp.pad(g1, ((0, 0), (0, _LANES - 2 * GLA_RANK)))
    w2 = jnp.zeros((2, _LANES, qkw), F32)
    w2 = w2.at[0, 0:GLA_RANK].set(w_g2[0]).at[1, GLA_RANK:2 * GLA_RANK].set(w_g2[1])
    return {
        "w_g1cat": g1.astype(BF16), "w2": w2.astype(BF16),
        "bg": b_g.reshape(2, 1, qkw), "ng": norm_g.reshape(1, -1),
    }


def _prep_ssd(w_in_dt, conv_w, conv_b, a_log, dt_bias, d_skip, norm_g):
    heads = a_log.shape[1]
    di = heads * SSD_HEADDIM
    hpg = heads // SSD_GROUPS
    gw = di // SSD_GROUPS
    main = 2 * di + 2 * SSD_GROUPS * SSD_STATE

    ncopy = 3
    pad = _LANES - ncopy * 2 * hpg

    def by_group(t):
        t = t.reshape(2, SSD_GROUPS, hpg).transpose(1, 0, 2).reshape(SSD_GROUPS, 1, 2 * hpg)
        return jnp.pad(jnp.tile(t, (1, 1, ncopy)), ((0, 0), (0, 0), (0, pad)))

    w_dt = w_in_dt.reshape(-1, 2, SSD_GROUPS, hpg).transpose(0, 2, 1, 3)
    w_dt = jnp.tile(w_dt.reshape(-1, SSD_GROUPS, 2 * hpg), (1, 1, ncopy))
    w_dt = jnp.pad(w_dt, ((0, 0), (0, 0), (0, pad))).reshape(-1, SSD_GROUPS * _LANES)
    lane = jnp.arange(_LANES)[:, None]
    chan = jnp.arange(gw)[None, :] // SSD_HEADDIM
    expand = jnp.stack([(lane % (2 * hpg) == e * hpg + chan) & (lane < ncopy * 2 * hpg)
                        for e in range(2)]).astype(BF16)
    return {
        "di": di, "w_dt": w_dt.astype(BF16), "n_main": main,
        "conv_w": conv_w, "conv_b": conv_b.reshape(1, -1),
        "dt_bias": by_group(dt_bias), "a_log": by_group(a_log),
        "d_skip": jnp.repeat(d_skip, SSD_HEADDIM).reshape(1, di), "ng": norm_g.reshape(1, di),
        "expand": expand,
    }


def _prep_lru(conv_w, conv_b, w_a, b_a, w_i, b_i, lam):
    width = conv_w.shape[1]
    return {
        "conv_w": conv_w, "conv_b": conv_b.reshape(1, width),
        "w_a": w_a.astype(BF16), "b_a": b_a.reshape(2, 1, width),
        "w_i": w_i.astype(BF16), "b_i": b_i.reshape(2, 1, width), "lam": lam.reshape(2, 1, width),
    }


def kernel(x_prompt, x_sample, state_gla, state_ssd, state_lru, c, c_ctx, w_mod, b_mod, norm_mix_g,
           norm_ffn_g, ffn_w_up, ffn_conv_w, ffn_conv_b, ffn_w_down, final_norm_g, gla_w_in, gla_w_g1,
           gla_w_g2, gla_b_g, gla_norm_g, gla_w_out, ssd_w_in, ssd_conv_w, ssd_conv_b, ssd_a_log,
           ssd_dt_bias, ssd_d, ssd_norm_g, ssd_w_out, lru_w_in, lru_conv_w, lru_conv_b, lru_w_a, lru_b_a,
           lru_w_i, lru_b_i, lru_lambda, lru_w_out):
    bp, lp, d = x_prompt.shape
    bs, ls, _ = x_sample.shape
    depth = w_mod.shape[0]
    tok = _Tokens(bp, lp, bs, ls)
    assert bs + 1 <= MOD_ROWS and ls // GRID_W * GRID_W == ls

    x = (x_prompt.reshape(bp * lp, d), x_sample.reshape(bs * ls, d))
    c_rows = jnp.concatenate([c_ctx[None], c, jnp.zeros((MOD_ROWS - 1 - bs, d), F32)], axis=0)
    mod = _mod_table(c_rows, w_mod, b_mod).reshape(depth * MOD_ROWS, MOD_CHUNKS, d)

    ffn_w_down_bf = ffn_w_down.astype(BF16)
    ssd_w_out_bf = ssd_w_out.astype(BF16)
    gla_states, ssd_new, lru_new = None, [], []
    for l in range(depth):
        kind, j = l % 3, l // 3
        h = _norm_mod(x, norm_mix_g[l], mod, l, tok, shift_idx=0, scale_idx=1)
        if kind == 0:
            w = _prep_gla(gla_w_g1[j], gla_w_g2[j], gla_b_g[j], gla_norm_g[j])
            y, gla_states = _gla_mixer(h, state_gla[:, j], gla_w_in, j, w, tok, gla_states)
            w_out = gla_w_out
        elif kind == 1:
            n_main = ssd_conv_w.shape[2] + ssd_d.shape[1] * SSD_HEADDIM
            w = _prep_ssd(ssd_w_in[j, :, n_main:], ssd_conv_w[j], ssd_conv_b[j], ssd_a_log[j],
                          ssd_dt_bias[j], ssd_d[j], ssd_norm_g[j])
            proj = _matmul(h, ssd_w_in, j, col_start=0, n_cols=n_main, bm=1024, bn=1024,
                           name="ssd_in_main")
            dtp = _matmul(h, w["w_dt"][None], 0, bm=1024, bn=SSD_GROUPS * _LANES, name="ssd_in_dt")
            gw = w["di"] // SSD_GROUPS
            s_lat0 = state_ssd[:, j].reshape(bs, 2, SSD_GROUPS, gw, SSD_STATE)
            y, s_new = _ssd_scan(proj, dtp, w, None, tok, latent=False)
            (y,) = _ssd_scan(proj, dtp, w, s_lat0, tok, latent=True, y_buf=y)
            ssd_new.append(s_new.reshape(bp, 2, -1, SSD_HEADDIM, SSD_STATE))
            w_out = ssd_w_out_bf
        else:
            w = _prep_lru(lru_conv_w[j], lru_conv_b[j], lru_w_a[j], lru_b_a[j],
                          lru_w_i[j], lru_b_i[j], lru_lambda[j])
            proj = _matmul(h, lru_w_in, j, bm=1024, bn=1024, name="lru_in")
            s_lat0 = state_lru[:, j].reshape(bs, 2, 1, -1)
            y, s_new = _lru_scan(proj, w, None, tok, latent=False)
            (y,) = _lru_scan(proj, w, s_lat0, tok, latent=True, y_buf=y)
            lru_new.append(s_new.reshape(bp, 2, -1))
            w_out = lru_w_out
        x = _matmul(y, w_out, j, bm=512 if isinstance(x, tuple) else 1024, bn=1024,
                    bk=min(w_out.shape[1], 2048), residual=(x, mod, l, tok, 2), name="mixer_out")
        h = _norm_mod(x, norm_ffn_g[l], mod, l, tok, shift_idx=3, scale_idx=4)
        g = _ffn_up(h, ffn_w_up, ffn_conv_w, ffn_conv_b, l, tok)
        x = _matmul(g, ffn_w_down_bf, l, bm=1024, bn=1024, bk=ffn_w_down.shape[1] // 2,
                    residual=(x, mod, l, tok, 5), name="ffn_down")
    y_prompt, y_sample = _final_norm(x, final_norm_g, tok)
    return (y_prompt.reshape(bp, lp, d), y_sample.reshape(bs, ls, d), gla_states,
            jnp.stack(ssd_new, axis=1), jnp.stack(lru_new, axis=1))
```
